```python
import math
import jax
import jax.numpy as jnp
from jax import lax
import numpy as np

D_MODEL = 2048
BATCH = 2
SEQ = 4096
DEPTH = 4
DEC_BATCH = 8
DEC_SEQ = 4
PAST_LEN = 16384
PAGE_SIZE = 128

N_MIXERS = 3
N_A_LAYERS = (DEPTH + 2) // 3
N_B_LAYERS = (DEPTH + 1) // 3
N_C_LAYERS = DEPTH // 3
DN_ALPHA = (2.0 * DEPTH) ** 0.25
DN_BETA = (8.0 * DEPTH) ** -0.25
LN_EPS = 1e-5
RMS_EPS = 1e-6
D_FF = 5632
CHUNK = 64

A_WINDOWS = (128, 512, 2048)
A_DILATIONS = (1, 4, 16)
A_GROUPS = 3
A_HEADS = 8
A_HEAD_DIM = 128
A_ROT_DIM = A_HEAD_DIM // 4
A_SPAN = 128
A_BLOCK = 128
ROPE_THETA = 500000.0
A_QKV = 3 * A_GROUPS * A_HEADS * A_HEAD_DIM

B_QK_HEADS = 16
B_V_HEADS = 32
B_HEAD_DIM = 128
B_CONV = 4
B_KEY_DIM = B_QK_HEADS * B_HEAD_DIM
B_VAL_DIM = B_V_HEADS * B_HEAD_DIM
B_CONV_DIM = 2 * B_KEY_DIM + B_VAL_DIM
B_IN = B_CONV_DIM + B_VAL_DIM + 2 * B_V_HEADS

C_D_INNER = 2 * D_MODEL
C_HEAD_DIM = 64
C_HEADS = C_D_INNER // C_HEAD_DIM
C_GROUPS = 8
C_D_STATE = 128
C_CONV = 4
C_CONV_DIM = C_D_INNER + 2 * C_GROUPS * C_D_STATE
C_IN = C_D_INNER + C_CONV_DIM + C_HEADS

kernel_name = 'hybrid_dilated_delta_ssd_macaron_step'

F32 = jnp.float32


def layer_norm(x, g, b):
    xf = x.astype(F32)
    mu = jnp.mean(xf, -1, keepdims=True)
    xc = xf - mu
    var = jnp.mean(xc * xc, -1, keepdims=True)
    return (xc * lax.rsqrt(var + LN_EPS) * g.astype(F32) + b.astype(F32)).astype(x.dtype)


def post_norm(x, f, g, b):
    return layer_norm(DN_ALPHA * x + f, g, b)


def swiglu(x, w_gu, w_down):
    gate, up = jnp.split(x @ w_gu, 2, axis=-1)
    return (jax.nn.silu(gate) * up) @ w_down


def causal_conv(u, prev, w):
    K = w.shape[0]
    T = u.shape[1]
    full = jnp.concatenate([prev.astype(u.dtype), u], axis=1)
    out = full[:, 0:T] * w[0]
    for j in range(1, K):
        out = out + full[:, j:j + T] * w[j]
    return out, full[:, T:]


def _pad_time(t, tp):
    return jnp.pad(t, [(0, 0), (0, tp - t.shape[1])] + [(0, 0)] * (t.ndim - 2))


def _chunk(t, c):
    bn, tp = t.shape[:2]
    return jnp.moveaxis(t.reshape(bn, tp // c, c, *t.shape[2:]), 1, 0)


def rope_partial(x, pos):
    half = A_ROT_DIM // 2
    inv_freq = ROPE_THETA ** (-jnp.arange(half, dtype=F32) / half)
    ang = pos.astype(F32)[:, None] * inv_freq[None, :]
    cos = jnp.cos(ang)[:, None, :]
    sin = jnp.sin(ang)[:, None, :]
    xr = x[..., :A_ROT_DIM].astype(F32)
    x1, x2 = xr[..., :half], xr[..., half:]
    rot = jnp.concatenate([x1 * cos - x2 * sin, x2 * cos + x1 * sin], -1).astype(x.dtype)
    return jnp.concatenate([rot, x[..., A_ROT_DIM:]], -1)


def a_project(x, pos, w_qkv):
    bn, t, _ = x.shape
    qkv = (x @ w_qkv).reshape(bn, t, 3, A_GROUPS * A_HEADS, A_HEAD_DIM)
    q = rope_partial(qkv[:, :, 0], pos)
    k = rope_partial(qkv[:, :, 1], pos)
    v = qkv[:, :, 2]
    split = lambda z: z.reshape(bn, t, A_GROUPS, A_HEADS, A_HEAD_DIM)
    return split(q), split(k), split(v)


def _attend(s, mask):
    s = jnp.where(mask, s, -jnp.inf)
    m = jnp.max(s, -1, keepdims=True)
    p = jnp.exp(s - m)
    l = jnp.sum(p, -1, keepdims=True)
    return p / l, (m + jnp.log(l))[..., 0]


def dilated_group_prompt(q, k, v, dil):
    bn, t, h, dh = q.shape
    period = dil * A_BLOCK
    tp = -(-t // period) * period
    L = tp // dil
    nb = L // A_BLOCK

    def by_stride(z):
        z = _pad_time(z, tp).reshape(bn, L, dil, h, dh).transpose(0, 2, 1, 3, 4)
        return z.reshape(bn, dil, nb, A_BLOCK, h, dh)

    def with_prev(z):
        prev = jnp.pad(z, ((0, 0), (0, 0), (1, 0), (0, 0), (0, 0), (0, 0)))[:, :, :-1]
        return jnp.concatenate([prev, z], axis=3)

    qb = by_stride(q)
    kk = with_prev(by_stride(k))
    vv = with_prev(by_stride(v))
    s = jnp.einsum('brnqhd,brnkhd->brnhqk', qb, kk, preferred_element_type=F32) * (A_HEAD_DIM ** -0.5)
    qi = jnp.arange(A_BLOCK)[:, None]
    ki = jnp.arange(2 * A_BLOCK)[None, :]
    dist = A_BLOCK + qi - ki
    band = (dist >= 0) & (dist <= A_SPAN)
    has_prev = (jnp.arange(nb) > 0)[:, None, None] | (ki >= A_BLOCK)[None]
    mask = (band[None] & has_prev)[:, None]
    p, lse = _attend(s, mask)
    o = jnp.einsum('brnhqk,brnkhd->brnqhd', p, vv.astype(F32))
    o = o.reshape(bn, dil, L, h, dh).transpose(0, 2, 1, 3, 4).reshape(bn, tp, h, dh)[:, :t]
    lse = lse.transpose(0, 1, 2, 4, 3).reshape(bn, dil, L, h).transpose(0, 2, 1, 3).reshape(bn, tp, h)[:, :t]
    return o, lse


def dilated_group_sample(q, k_all, v_all, dil):
    t = q.shape[1]
    wb = k_all.shape[1] - t
    idx = wb + jnp.arange(t)[:, None] - dil * jnp.arange(A_SPAN + 1)[None, :]
    valid = idx >= 0
    idx = jnp.maximum(idx, 0)
    kg = k_all[:, idx]
    vg = v_all[:, idx]
    s = jnp.einsum('bthd,btkhd->bhtk', q, kg, preferred_element_type=F32) * (A_HEAD_DIM ** -0.5)
    p, lse = _attend(s, valid[None, None])
    o = jnp.einsum('bhtk,btkhd->bthd', p, vg.astype(F32))
    return o, lse.transpose(0, 2, 1)


def a_merge(outs, lses, w_o, dtype):
    o = jnp.stack(outs)
    wts = jax.nn.softmax(jnp.stack(lses), axis=0)
    o = jnp.einsum('gbthd,gbth->bthd', o, wts)
    bn, t = o.shape[:2]
    return o.reshape(bn, t, A_HEADS * A_HEAD_DIM).astype(dtype) @ w_o


def mixer_a_prompt(x, w_qkv, w_o):
    t = x.shape[1]
    q, k, v = a_project(x, jnp.arange(t), w_qkv)
    outs, lses, rows = [], [], []
    for g in range(A_GROUPS):
        o, lse = dilated_group_prompt(q[:, :, g], k[:, :, g], v[:, :, g], A_DILATIONS[g])
        outs.append(o)
        lses.append(lse)
        keep = min(A_WINDOWS[g], t)
        rows.append(jnp.stack([k[:, t - keep:, g], v[:, t - keep:, g]], axis=2))
    return a_merge(outs, lses, w_o, x.dtype), rows


def mixer_a_sample(x, bufs, w_qkv, w_o):
    t = x.shape[1]
    q, k, v = a_project(x, PAST_LEN + jnp.arange(t), w_qkv)
    outs, lses, rows = [], [], []
    for g in range(A_GROUPS):
        buf = bufs[g]
        wb = buf.shape[1]
        k_all = jnp.concatenate([buf[:, :, 0], k[:, :, g].astype(buf.dtype)], axis=1)
        v_all = jnp.concatenate([buf[:, :, 1], v[:, :, g].astype(buf.dtype)], axis=1)
        o, lse = dilated_group_sample(q[:, :, g], k_all, v_all, A_DILATIONS[g])
        outs.append(o)
        lses.append(lse)
        rows.append(jnp.stack([k_all, v_all], axis=2)[:, t:t + wb])
    return a_merge(outs, lses, w_o, x.dtype), rows


def l2norm(z):
    zf = z.astype(F32)
    return zf * lax.rsqrt(jnp.sum(zf * zf, -1, keepdims=True) + 1e-6)


def gated_delta_chunked(q, k, v, g, beta, s0):
    bn, t, h, dk = q.shape
    c = min(CHUNK, t)
    tp = -(-t // c) * c
    q = q * dk ** -0.5
    xs = [_chunk(_pad_time(z, tp), c) for z in (q, k, v, g, beta)]
    xs[3] = jnp.cumsum(xs[3], axis=2)
    ii = jnp.arange(c)[:, None]
    jj = jnp.arange(c)[None, :]
    incl = ii >= jj
    strict = ii > jj
    eye = jnp.eye(c, dtype=F32)

    def step(S, inp):
        q_c, k_c, v_c, g_c, b_c = inp
        gh = g_c.transpose(0, 2, 1)
        decay = jnp.exp(jnp.where(incl, gh[..., :, None] - gh[..., None, :], -jnp.inf))
        kb = k_c * b_c[..., None]
        lmat = jnp.where(strict, jnp.einsum('bihd,bjhd->bhij', kb, k_c) * decay, 0.0)
        rhs = jnp.concatenate([v_c * b_c[..., None], kb * jnp.exp(g_c)[..., None]], -1).transpose(0, 2, 1, 3)
        sol = lax.linalg.triangular_solve(lmat + eye, rhs, left_side=True, lower=True, unit_diagonal=True)
        dv = v_c.shape[-1]
        u, w = sol[..., :dv], sol[..., dv:]
        v_new = u - jnp.einsum('bhcd,bhde->bhce', w, S)
        qk = jnp.where(incl, jnp.einsum('bihd,bjhd->bhij', q_c, k_c) * decay, 0.0)
        o = (jnp.einsum('bihd,bhde->bihe', q_c * jnp.exp(g_c)[..., None], S)
             + jnp.einsum('bhij,bhje->bihe', qk, v_new))
        k_dec = k_c * jnp.exp(g_c[:, -1:] - g_c)[..., None]
        S = S * jnp.exp(gh[..., -1])[..., None, None] + jnp.einsum('bchd,bhce->bhde', k_dec, v_new)
        return S, o

    S, o = lax.scan(step, s0, tuple(xs))
    o = jnp.moveaxis(o, 0, 1).reshape(bn, tp, h, -1)[:, :t]
    return o, S


def mixer_b(x, s0, conv0, w_in, conv_w, a_log, dt_bias, norm_w, w_out):
    bn, t, _ = x.shape
    qkv, z, b_raw, a_raw = jnp.split(x @ w_in, [B_CONV_DIM, B_CONV_DIM + B_VAL_DIM,
                                                B_CONV_DIM + B_VAL_DIM + B_V_HEADS], axis=-1)
    qkv, conv_new = causal_conv(qkv, conv0, conv_w)
    qkv = jax.nn.silu(qkv)
    q, k, v = jnp.split(qkv, [B_KEY_DIM, 2 * B_KEY_DIM], axis=-1)
    rep = B_V_HEADS // B_QK_HEADS
    q = jnp.repeat(l2norm(q.reshape(bn, t, B_QK_HEADS, B_HEAD_DIM)), rep, axis=2)
    k = jnp.repeat(l2norm(k.reshape(bn, t, B_QK_HEADS, B_HEAD_DIM)), rep, axis=2)
    v = v.reshape(bn, t, B_V_HEADS, B_HEAD_DIM).astype(F32)
    beta = jax.nn.sigmoid(b_raw.astype(F32))
    g = -jnp.exp(a_log.astype(F32)) * jax.nn.softplus(a_raw.astype(F32) + dt_bias.astype(F32))
    o, s_new = gated_delta_chunked(q, k, v, g, beta, s0.astype(F32))
    o = o * lax.rsqrt(jnp.mean(o * o, -1, keepdims=True) + RMS_EPS) * norm_w.astype(F32)
    o = o * jax.nn.silu(z.reshape(bn, t, B_V_HEADS, B_HEAD_DIM).astype(F32))
    y = o.reshape(bn, t, B_VAL_DIM).astype(x.dtype) @ w_out
    return y, s_new, conv_new


def ssd_chunked(x, dt, a, bm, cm, h0):
    bn, t = x.shape[:2]
    c = min(CHUNK, t)
    tp = -(-t // c) * c
    la = dt * a
    xs = [_chunk(_pad_time(z, tp), c) for z in (x, dt, la, bm, cm)]
    xs[2] = jnp.cumsum(xs[2], axis=2)
    tri = (jnp.arange(c)[:, None] >= jnp.arange(c)[None, :])[None, :, :, None, None]

    def step(h, inp):
        x_c, dt_c, acs, b_c, c_c = inp
        decay = jnp.exp(jnp.where(tri, acs[:, :, None] - acs[:, None, :], -jnp.inf))
        cb = jnp.einsum('bign,bjgn->bijg', c_c, b_c)
        y = jnp.einsum('bijg,bijgh,bjghp->bighp', cb, decay, dt_c[..., None] * x_c)
        y = y + jnp.einsum('bign,bghpn->bighp', c_c, h) * jnp.exp(acs)[..., None]
        w_end = jnp.exp(acs[:, -1:] - acs) * dt_c
        h = (h * jnp.exp(acs[:, -1])[..., None, None]
             + jnp.einsum('bjgn,bjghp->bghpn', b_c, w_end[..., None] * x_c))
        return h, y

    h, y = lax.scan(step, h0, tuple(xs))
    y = jnp.moveaxis(y, 0, 1).reshape(bn, tp, *x.shape[2:])[:, :t]
    return y, h


def mixer_c(x, h0, conv0, w_in, conv_w, conv_b, dt_bias, a_log, d_skip, norm_w, w_out):
    bn, t, _ = x.shape
    hg = C_HEADS // C_GROUPS
    z, xbc, dt = jnp.split(x @ w_in, [C_D_INNER, C_D_INNER + C_CONV_DIM], axis=-1)
    xbc, conv_new = causal_conv(xbc, conv0, conv_w)
    xbc = jax.nn.silu(xbc + conv_b)
    xs, bm, cm = jnp.split(xbc, [C_D_INNER, C_D_INNER + C_GROUPS * C_D_STATE], axis=-1)
    xs = xs.reshape(bn, t, C_GROUPS, hg, C_HEAD_DIM).astype(F32)
    bm = bm.reshape(bn, t, C_GROUPS, C_D_STATE).astype(F32)
    cm = cm.reshape(bn, t, C_GROUPS, C_D_STATE).astype(F32)
    dt = jax.nn.softplus(dt.astype(F32) + dt_bias.astype(F32)).reshape(bn, t, C_GROUPS, hg)
    a = -jnp.exp(a_log.astype(F32)).reshape(C_GROUPS, hg)
    h0 = h0.astype(F32).reshape(bn, C_GROUPS, hg, C_HEAD_DIM, C_D_STATE)
    y, h_new = ssd_chunked(xs, dt, a, bm, cm, h0)
    y = y + d_skip.astype(F32).reshape(C_GROUPS, hg)[..., None] * xs
    y = y.reshape(bn, t, C_D_INNER) * jax.nn.silu(z.astype(F32))
    yg = y.reshape(bn, t, C_GROUPS, C_D_INNER // C_GROUPS)
    yg = yg * lax.rsqrt(jnp.mean(yg * yg, -1, keepdims=True) + RMS_EPS)
    y = yg.reshape(bn, t, C_D_INNER) * norm_w.astype(F32)
    out = y.astype(x.dtype) @ w_out
    return out, h_new.reshape(bn, C_HEADS, C_HEAD_DIM, C_D_STATE), conv_new


def setup_inputs(seed: int = 0) -> dict:
    key = jax.random.key(seed)
    ks = list(jax.random.split(key, 40))

    def nrm(shape, scale):
        return scale * jax.random.normal(ks.pop(), shape, F32)

    def dt_bias(shape):
        u = jax.random.uniform(ks.pop(), shape, F32)
        dt = jnp.exp(u * (math.log(0.1) - math.log(0.001)) + math.log(0.001))
        return dt + jnp.log(-jnp.expm1(-dt))

    def a_log(shape):
        return jnp.log(jax.random.uniform(ks.pop(), shape, F32, 1.0, 16.0))

    d = D_MODEL
    inp = {}
    inp['x_prompt'] = nrm((BATCH, SEQ, d), 1.0)
    inp['x_sample'] = nrm((DEC_BATCH, DEC_SEQ, d), 1.0)
    for w in A_WINDOWS:
        inp['cache_a_kv_w%d' % w] = nrm((N_A_LAYERS, DEC_BATCH, min(w, PAST_LEN), 2, A_HEADS, A_HEAD_DIM), 1.0)
    inp['state_b_ssm'] = nrm((N_B_LAYERS, DEC_BATCH, B_V_HEADS, B_HEAD_DIM, B_HEAD_DIM), 0.1)
    inp['state_b_conv'] = nrm((N_B_LAYERS, DEC_BATCH, B_CONV - 1, B_CONV_DIM), 1.0)
    inp['state_c_ssm'] = nrm((N_C_LAYERS, DEC_BATCH, C_HEADS, C_HEAD_DIM, C_D_STATE), 0.1)
    inp['state_c_conv'] = nrm((N_C_LAYERS, DEC_BATCH, C_CONV - 1, C_CONV_DIM), 1.0)
    inp['ln_g'] = 1.0 + nrm((DEPTH, 3, d), 0.02)
    inp['ln_b'] = nrm((DEPTH, 3, d), 0.02)
    inp['ffn1_w_gu'] = nrm((DEPTH, d, 2 * D_FF), d ** -0.5)
    inp['ffn1_w_down'] = nrm((DEPTH, D_FF, d), DN_BETA * D_FF ** -0.5)
    inp['ffn2_w_gu'] = nrm((DEPTH, d, 2 * D_FF), d ** -0.5)
    inp['ffn2_w_down'] = nrm((DEPTH, D_FF, d), DN_BETA * D_FF ** -0.5)
    inp['a_w_qkv'] = nrm((N_A_LAYERS, d, A_QKV), d ** -0.5)
    inp['a_w_o'] = nrm((N_A_LAYERS, A_HEADS * A_HEAD_DIM, d), DN_BETA * (A_HEADS * A_HEAD_DIM) ** -0.5)
    inp['b_w_in'] = nrm((N_B_LAYERS, d, B_IN), d ** -0.5)
    inp['b_conv_w'] = nrm((N_B_LAYERS, B_CONV, B_CONV_DIM), B_CONV ** -0.5)
    inp['b_a_log'] = a_log((N_B_LAYERS, B_V_HEADS))
    inp['b_dt_bias'] = dt_bias((N_B_LAYERS, B_V_HEADS))
    inp['b_norm_w'] = 1.0 + nrm((N_B_LAYERS, B_HEAD_DIM), 0.02)
    inp['b_w_out'] = nrm((N_B_LAYERS, B_VAL_DIM, d), DN_BETA * B_VAL_DIM ** -0.5)
    inp['c_w_in'] = nrm((N_C_LAYERS, d, C_IN), d ** -0.5)
    inp['c_conv_w'] = nrm((N_C_LAYERS, C_CONV, C_CONV_DIM), C_CONV ** -0.5)
    inp['c_conv_b'] = nrm((N_C_LAYERS, C_CONV_DIM), 0.02)
    inp['c_dt_bias'] = dt_bias((N_C_LAYERS, C_HEADS))
    inp['c_a_log'] = a_log((N_C_LAYERS, C_HEADS))
    inp['c_d'] = 1.0 + nrm((N_C_LAYERS, C_HEADS), 0.1)
    inp['c_norm_w'] = 1.0 + nrm((N_C_LAYERS, C_D_INNER), 0.02)
    inp['c_w_out'] = nrm((N_C_LAYERS, C_D_INNER, d), DN_BETA * C_D_INNER ** -0.5)
    return inp


def reference(x_prompt, x_sample, cache_a_kv_w128, cache_a_kv_w512, cache_a_kv_w2048,
              state_b_ssm, state_b_conv, state_c_ssm, state_c_conv,
              ln_g, ln_b, ffn1_w_gu, ffn1_w_down, ffn2_w_gu, ffn2_w_down,
              a_w_qkv, a_w_o,
              b_w_in, b_conv_w, b_a_log, b_dt_bias, b_norm_w, b_w_out,
              c_w_in, c_conv_w, c_conv_b, c_dt_bias, c_a_log, c_d, c_norm_w, c_w_out):
    a_bufs = (cache_a_kv_w128, cache_a_kv_w512, cache_a_kv_w2048)
    out_dtype = x_prompt.dtype
    xp, xs = x_prompt, x_sample
    bp = xp.shape[0]
    a_p = [[] for _ in range(A_GROUPS)]
    a_s = [[] for _ in range(A_GROUPS)]
    b_ssm_p, b_ssm_s, b_conv_p, b_conv_s = [], [], [], []
    c_ssm_p, c_ssm_s, c_conv_p, c_conv_s = [], [], [], []
    for i in range(DEPTH):
        kind, j = i % N_MIXERS, i // N_MIXERS
        xp = post_norm(xp, 0.5 * swiglu(xp, ffn1_w_gu[i], ffn1_w_down[i]), ln_g[i, 0], ln_b[i, 0])
        xs = post_norm(xs, 0.5 * swiglu(xs, ffn1_w_gu[i], ffn1_w_down[i]), ln_g[i, 0], ln_b[i, 0])
        if kind == 0:
            mp, rows_p = mixer_a_prompt(xp, a_w_qkv[j], a_w_o[j])
            ms, rows_s = mixer_a_sample(xs, [buf[j] for buf in a_bufs], a_w_qkv[j], a_w_o[j])
            for g in range(A_GROUPS):
                a_p[g].append(rows_p[g])
                a_s[g].append(rows_s[g])
        elif kind == 1:
            s0 = jnp.zeros((bp, B_V_HEADS, B_HEAD_DIM, B_HEAD_DIM), F32)
            cv0 = jnp.zeros((bp, B_CONV - 1, B_CONV_DIM), xp.dtype)
            mp, sp, cp = mixer_b(xp, s0, cv0, b_w_in[j], b_conv_w[j], b_a_log[j], b_dt_bias[j], b_norm_w[j], b_w_out[j])
            ms, ss, cs = mixer_b(xs, state_b_ssm[j], state_b_conv[j], b_w_in[j], b_conv_w[j], b_a_log[j],
                                 b_dt_bias[j], b_norm_w[j], b_w_out[j])
            b_ssm_p.append(sp)
            b_ssm_s.append(ss)
            b_conv_p.append(cp)
            b_conv_s.append(cs)
        else:
            h0 = jnp.zeros((bp, C_HEADS, C_HEAD_DIM, C_D_STATE), F32)
            cv0 = jnp.zeros((bp, C_CONV - 1, C_CONV_DIM), xp.dtype)
            mp, hp, cp = mixer_c(xp, h0, cv0, c_w_in[j], c_conv_w[j], c_conv_b[j], c_dt_bias[j], c_a_log[j],
                                 c_d[j], c_norm_w[j], c_w_out[j])
            ms, hs, cs = mixer_c(xs, state_c_ssm[j], state_c_conv[j], c_w_in[j], c_conv_w[j], c_conv_b[j],
                                 c_dt_bias[j], c_a_log[j], c_d[j], c_norm_w[j], c_w_out[j])
            c_ssm_p.append(hp)
            c_ssm_s.append(hs)
            c_conv_p.append(cp)
            c_conv_s.append(cs)
        xp = post_norm(xp, mp, ln_g[i, 1], ln_b[i, 1])
        xs = post_norm(xs, ms, ln_g[i, 1], ln_b[i, 1])
        xp = post_norm(xp, 0.5 * swiglu(xp, ffn2_w_gu[i], ffn2_w_down[i]), ln_g[i, 2], ln_b[i, 2])
        xs = post_norm(xs, 0.5 * swiglu(xs, ffn2_w_gu[i], ffn2_w_down[i]), ln_g[i, 2], ln_b[i, 2])

    st = lambda lst: jnp.stack(lst).astype(out_dtype)
    return (xp, xs,
            st(a_p[0]), st(a_s[0]), st(a_p[1]), st(a_s[1]), st(a_p[2]), st(a_s[2]),
            st(b_ssm_p), st(b_ssm_s), st(b_conv_p), st(b_conv_s),
            st(c_ssm_p), st(c_ssm_s), st(c_conv_p), st(c_conv_s))
```

```python
import functools

import jax
import jax.numpy as jnp
from jax import lax
from jax.experimental import pallas as pl
from jax.experimental.pallas import tpu as pltpu

F32 = jnp.float32
BF16 = jnp.bfloat16
HIGHEST = lax.Precision.HIGHEST

VMEM_LIMIT_BYTES = 56 * 1024 * 1024
LANES = 128
SUBLANES = 8
ROW_TILE = 512
NEG_BIG = -1e30

DEPTH = 4
DN_ALPHA = (2.0 * DEPTH) ** 0.25
LN_EPS = 1e-5
RMS_EPS = 1e-6
PAST_LEN = 16384

A_DILATIONS = (1, 4, 16)
A_GROUPS = 3
A_HEADS = 8
A_HEAD_DIM = 128
A_ROT_HALF = A_HEAD_DIM // 8
A_BLOCK = 128
A_GROUP_COLS = A_HEADS * A_HEAD_DIM
ROPE_THETA = 500000.0

B_QK_HEADS = 16
B_V_HEADS = 32
B_HEAD_DIM = 128
B_KEY_DIM = B_QK_HEADS * B_HEAD_DIM
B_VAL_DIM = B_V_HEADS * B_HEAD_DIM
B_CONV_DIM = 2 * B_KEY_DIM + B_VAL_DIM
B_CHUNK = 64
B_HEADS_PER_STEP = 2
B_INV_BLOCK = 16

C_D_INNER = 4096
C_HEADS = 64
C_HEAD_DIM = 64
C_GROUPS = 8
C_HEADS_PER_GROUP = C_HEADS // C_GROUPS
C_GROUP_COLS = C_D_INNER // C_GROUPS
C_D_STATE = 128
C_CONV_DIM = C_D_INNER + 2 * C_GROUPS * C_D_STATE
C_CHUNK = 128

CONV_TAPS = 4


def _params(n_axes):
    return pltpu.CompilerParams(
        dimension_semantics=("arbitrary",) * n_axes,
        vmem_limit_bytes=VMEM_LIMIT_BYTES,
    )


def _dot(a, b, precision=None):
    return jnp.dot(a, b, preferred_element_type=F32, precision=precision)


def _dot_nt(a, b, precision=None):
    return lax.dot_general(a, b, (((1,), (1,)), ((), ())),
                           preferred_element_type=F32, precision=precision)


def _dot_tn(a, b, precision=None):
    return lax.dot_general(a, b, (((0,), (0,)), ((), ())),
                           preferred_element_type=F32, precision=precision)


def _silu(x):
    return x * jax.nn.sigmoid(x)


def _softplus(x):
    return jnp.maximum(x, 0.0) + jnp.log(1.0 + jnp.exp(-jnp.abs(x)))


def _row_tiles(rows, tm):
    n_full, rem = divmod(rows, tm)
    return n_full + (1 if rem else 0), n_full, rem


def _per_row_tile(i, n_full, rem, tm, body):
    if rem == 0:
        body(tm)
        return

    @pl.when(i < n_full)
    def _():
        body(tm)

    @pl.when(i == n_full)
    def _():
        body(rem)


def _mm_kernel(x_ref, w_ref, o_ref, *, tm, n_full, rem):
    def body(r):
        o_ref[:r, :] = _dot(x_ref[:r, :], w_ref[...]).astype(o_ref.dtype)

    _per_row_tile(pl.program_id(0), n_full, rem, tm, body)


def matmul(x, w, *, tn, out_dtype=F32, tm=ROW_TILE):
    rows, k = x.shape
    n = w.shape[1]
    steps, n_full, rem = _row_tiles(rows, tm)
    return pl.pallas_call(
        functools.partial(_mm_kernel, tm=tm, n_full=n_full, rem=rem),
        grid=(steps, n // tn),
        in_specs=[pl.BlockSpec((tm, k), lambda i, j: (i, 0)),
                  pl.BlockSpec((k, tn), lambda i, j: (0, j))],
        out_specs=pl.BlockSpec((tm, tn), lambda i, j: (i, j)),
        out_shape=jax.ShapeDtypeStruct((rows, n), out_dtype),
        compiler_params=_params(2),
        name="matmul",
    )(x, w)


def _swiglu_kernel(x_ref, wg_ref, wu_ref, o_ref, *, tm, n_full, rem):
    def body(r):
        x = x_ref[:r, :]
        gate = _dot(x, wg_ref[...])
        up = _dot(x, wu_ref[...])
        o_ref[:r, :] = (_silu(gate) * up).astype(o_ref.dtype)

    _per_row_tile(pl.program_id(0), n_full, rem, tm, body)


def swiglu_up(x, w_gu, *, tn=512, tm=ROW_TILE):
    rows, k = x.shape
    f = w_gu.shape[1] // 2
    steps, n_full, rem = _row_tiles(rows, tm)
    nj = f // tn
    return pl.pallas_call(
        functools.partial(_swiglu_kernel, tm=tm, n_full=n_full, rem=rem),
        grid=(steps, nj),
        in_specs=[pl.BlockSpec((tm, k), lambda i, j: (i, 0)),
                  pl.BlockSpec((k, tn), lambda i, j: (0, j)),
                  pl.BlockSpec((k, tn), lambda i, j: (0, j + nj))],
        out_specs=pl.BlockSpec((tm, tn), lambda i, j: (i, j)),
        out_shape=jax.ShapeDtypeStruct((rows, f), BF16),
        compiler_params=_params(2),
        name="swiglu_up",
    )(x, w_gu, w_gu)


def _postnorm_kernel(a_ref, w_ref, x_ref, g_ref, b_ref, y_ref, yb_ref, acc_ref,
                     *, scale, nk, tm, n_full, rem):
    k = pl.program_id(1)

    def body(r):
        part = _dot(a_ref[:r, :], w_ref[...])

        @pl.when(k == 0)
        def _():
            acc_ref[:r, :] = part

        @pl.when(k > 0)
        def _():
            acc_ref[:r, :] += part

        @pl.when(k == nk - 1)
        def _():
            y = DN_ALPHA * x_ref[:r, :] + scale * acc_ref[:r, :]
            mu = jnp.mean(y, axis=-1, keepdims=True)
            yc = y - mu
            var = jnp.mean(yc * yc, axis=-1, keepdims=True)
            out = yc * lax.rsqrt(var + LN_EPS) * g_ref[...] + b_ref[...]
            y_ref[:r, :] = out
            yb_ref[:r, :] = out.astype(BF16)

    _per_row_tile(pl.program_id(0), n_full, rem, tm, body)


def matmul_postnorm(a, w, x, g, b, *, scale, tk, tm=ROW_TILE):
    rows, kdim = a.shape
    d = w.shape[1]
    nk = kdim // tk
    steps, n_full, rem = _row_tiles(rows, tm)
    return pl.pallas_call(
        functools.partial(_postnorm_kernel, scale=scale, nk=nk, tm=tm, n_full=n_full, rem=rem),
        grid=(steps, nk),
        in_specs=[pl.BlockSpec((tm, tk), lambda i, k: (i, k)),
                  pl.BlockSpec((tk, d), lambda i, k: (k, 0)),
                  pl.BlockSpec((tm, d), lambda i, k: (i, 0)),
                  pl.BlockSpec((1, d), lambda i, k: (0, 0)),
                  pl.BlockSpec((1, d), lambda i, k: (0, 0))],
        out_specs=[pl.BlockSpec((tm, d), lambda i, k: (i, 0)),
                   pl.BlockSpec((tm, d), lambda i, k: (i, 0))],
        out_shape=[jax.ShapeDtypeStruct((rows, d), F32),
                   jax.ShapeDtypeStruct((rows, d), BF16)],
        scratch_shapes=[pltpu.VMEM((tm, d), F32)],
        compiler_params=_params(2),
        name="matmul_postnorm",
    )(a, w, x, g.reshape(1, d), b.reshape(1, d))


def _rope_kernel(x_ref, cos_ref, sin_ref, o_ref, *, tm, n_full, rem):
    def body(r):
        cos = cos_ref[:r, :]
        sin = sin_ref[:r, :]
        lane = lax.broadcasted_iota(jnp.int32, (r, A_HEAD_DIM), 1)
        for h in range(A_HEADS):
            sl = slice(h * A_HEAD_DIM, (h + 1) * A_HEAD_DIM)
            x = x_ref[:r, sl]
            partner = jnp.where(lane < A_ROT_HALF,
                                pltpu.roll(x, A_HEAD_DIM - A_ROT_HALF, axis=1),
                                pltpu.roll(x, A_ROT_HALF, axis=1))
            o_ref[:r, sl] = x * cos + partner * sin

    _per_row_tile(pl.program_id(0), n_full, rem, tm, body)


def rope_qk(qkv, cos_full, sin_full, *, tm=ROW_TILE):
    rows = qkv.shape[0]
    n_blocks = 2 * A_GROUPS
    steps, n_full, rem = _row_tiles(rows, tm)
    return pl.pallas_call(
        functools.partial(_rope_kernel, tm=tm, n_full=n_full, rem=rem),
        grid=(steps, n_blocks),
        in_specs=[pl.BlockSpec((tm, A_GROUP_COLS), lambda i, j: (i, j)),
                  pl.BlockSpec((tm, A_HEAD_DIM), lambda i, j: (i, 0)),
                  pl.BlockSpec((tm, A_HEAD_DIM), lambda i, j: (i, 0))],
        out_specs=pl.BlockSpec((tm, A_GROUP_COLS), lambda i, j: (i, j)),
        out_shape=jax.ShapeDtypeStruct((rows, n_blocks * A_GROUP_COLS), F32),
        compiler_params=_params(2),
        name="rope_qk",
    )(qkv, cos_full, sin_full)


def _attn_prompt_kernel(q_ref, kp_ref, kc_ref, vp_ref, vc_ref, o_ref, lse_ref):
    lb = pl.program_id(2)
    n = A_BLOCK
    qi = lax.broadcasted_iota(jnp.int32, (n, n), 0)
    kj = lax.broadcasted_iota(jnp.int32, (n, n), 1)
    mask_cur = kj <= qi
    mask_prev = jnp.logical_and(kj >= qi, lb > 0)
    scale = A_HEAD_DIM ** -0.5
    lse_all = jnp.zeros((n, LANES), F32)
    for h in range(A_HEADS):
        sl = slice(h * A_HEAD_DIM, (h + 1) * A_HEAD_DIM)
        q = q_ref[:, sl].astype(BF16)
        s_c = jnp.where(mask_cur, _dot_nt(q, kc_ref[:, sl].astype(BF16)) * scale, NEG_BIG)
        s_p = jnp.where(mask_prev, _dot_nt(q, kp_ref[:, sl].astype(BF16)) * scale, NEG_BIG)
        m = jnp.maximum(jnp.max(s_c, axis=-1, keepdims=True), jnp.max(s_p, axis=-1, keepdims=True))
        p_c = jnp.exp(s_c - m)
        p_p = jnp.exp(s_p - m)
        l = jnp.sum(p_c, axis=-1, keepdims=True) + jnp.sum(p_p, axis=-1, keepdims=True)
        o = _dot(p_c.astype(BF16), vc_ref[:, sl].astype(BF16)) + _dot(p_p.astype(BF16), vp_ref[:, sl].astype(BF16))
        o_ref[:, sl] = o / l
        lse_all = jnp.where(kj == h, m + jnp.log(l), lse_all)
    lse_ref[...] = lse_all


def attn_prompt(qk, qkv, *, g, batch, seq):
    dil = A_DILATIONS[g]
    rows = qk.shape[0]
    nb = seq // dil // A_BLOCK
    qk_v = qk.reshape(rows // dil, dil * 2 * A_GROUPS * A_GROUP_COLS)
    qkv_v = qkv.reshape(rows // dil, dil * 3 * A_GROUPS * A_GROUP_COLS)
    cur = lambda b, r, lb: b * nb + lb
    prev = lambda b, r, lb: b * nb + jnp.maximum(lb - 1, 0)
    kv_spec = lambda row, per_row, off: pl.BlockSpec(
        (A_BLOCK, A_GROUP_COLS), lambda b, r, lb: (row(b, r, lb), r * per_row + off + g))
    o, lse = pl.pallas_call(
        _attn_prompt_kernel,
        grid=(batch, dil, nb),
        in_specs=[kv_spec(cur, 2 * A_GROUPS, 0),
                  kv_spec(prev, 2 * A_GROUPS, A_GROUPS),
                  kv_spec(cur, 2 * A_GROUPS, A_GROUPS),
                  kv_spec(prev, 3 * A_GROUPS, 2 * A_GROUPS),
                  kv_spec(cur, 3 * A_GROUPS, 2 * A_GROUPS)],
        out_specs=[pl.BlockSpec((A_BLOCK, A_GROUP_COLS), lambda b, r, lb: (b * nb + lb, r)),
                   pl.BlockSpec((A_BLOCK, LANES), lambda b, r, lb: (b * nb + lb, r))],
        out_shape=[jax.ShapeDtypeStruct((batch * seq // dil, dil * A_GROUP_COLS), F32),
                   jax.ShapeDtypeStruct((batch * seq // dil, dil * LANES), F32)],
        compiler_params=_params(3),
        name="attn_prompt",
    )(qk_v, qk_v, qk_v, qkv_v, qkv_v)
    return o.reshape(batch * seq, A_GROUP_COLS), lse.reshape(batch * seq, LANES)


def _attn_sample_kernel(q_ref, kn_ref, vn_ref, *refs, dil, ts):
    n_res = len(refs) // 2 - 1
    cache_refs, (o_ref, lse_ref) = refs[:2 * n_res], refs[2 * n_res:]
    scale = A_HEAD_DIM ** -0.5
    jc = lax.broadcasted_iota(jnp.int32, (A_BLOCK, 1), 0)
    un = lax.broadcasted_iota(jnp.int32, (ts, 1), 0)
    lane = lax.broadcasted_iota(jnp.int32, (1, LANES), 1)
    for t in range(ts):
        kc_ref, vc_ref = cache_refs[2 * (t % n_res)], cache_refs[2 * (t % n_res) + 1]
        valid_n = (un <= t) if dil == 1 else (un == t)
        lse_row = jnp.zeros((1, LANES), F32)
        for h in range(A_HEADS):
            sl = slice(h * A_HEAD_DIM, (h + 1) * A_HEAD_DIM)
            q = q_ref[0, t:t + 1, sl]
            s_c = jnp.sum(kc_ref[0, :, sl] * q, axis=-1, keepdims=True) * scale
            if dil == 1:
                s_c = jnp.where(jc >= t, s_c, NEG_BIG)
            s_n = jnp.where(valid_n, jnp.sum(kn_ref[0, :, sl] * q, axis=-1, keepdims=True) * scale, NEG_BIG)
            m = jnp.maximum(jnp.max(s_c, axis=0, keepdims=True), jnp.max(s_n, axis=0, keepdims=True))
            p_c = jnp.exp(s_c - m)
            p_n = jnp.exp(s_n - m)
            l = jnp.sum(p_c, axis=0, keepdims=True) + jnp.sum(p_n, axis=0, keepdims=True)
            o = (jnp.sum(p_c * vc_ref[0, :, sl], axis=0, keepdims=True)
                 + jnp.sum(p_n * vn_ref[0, :, sl], axis=0, keepdims=True))
            o_ref[0, t:t + 1, sl] = o / l
            lse_row = jnp.where(lane == h, m + jnp.log(l), lse_row)
        lse_ref[0, t:t + 1, :] = lse_row


def attn_sample(q, k_new, v_new, cache, *, g):
    dil = A_DILATIONS[g]
    db, ts, cols = q.shape
    wb = cache.shape[1]
    assert wb == A_BLOCK * dil and (dil == 1 or ts <= dil)
    cache_v = cache.reshape(db, wb // dil, dil * 2 * cols)
    n_res = 1 if dil == 1 else ts
    tok_spec = pl.BlockSpec((1, ts, cols), lambda b: (b, 0, 0))
    cache_specs = [pl.BlockSpec((1, A_BLOCK, cols), lambda b, blk=blk: (b, 0, blk)) for blk in range(2 * n_res)]
    return pl.pallas_call(
        functools.partial(_attn_sample_kernel, dil=dil, ts=ts),
        grid=(db,),
        in_specs=[tok_spec, tok_spec, tok_spec] + cache_specs,
        out_specs=[pl.BlockSpec((1, ts, cols), lambda b: (b, 0, 0)),
                   pl.BlockSpec((1, ts, LANES), lambda b: (b, 0, 0))],
        out_shape=[jax.ShapeDtypeStruct((db, ts, cols), F32),
                   jax.ShapeDtypeStruct((db, ts, LANES), F32)],
        compiler_params=_params(1),
        name="attn_sample",
    )(q, k_new, v_new, *([cache_v] * (2 * n_res)))


def _attn_merge_kernel(o0_ref, o1_ref, o2_ref, l0_ref, l1_ref, l2_ref, a_ref):
    l0, l1, l2 = l0_ref[...], l1_ref[...], l2_ref[...]
    m = jnp.maximum(jnp.maximum(l0, l1), l2)
    e0, e1, e2 = jnp.exp(l0 - m), jnp.exp(l1 - m), jnp.exp(l2 - m)
    den = e0 + e1 + e2
    w0, w1, w2 = e0 / den, e1 / den, e2 / den
    for h in range(A_HEADS):
        sl = slice(h * A_HEAD_DIM, (h + 1) * A_HEAD_DIM)
        a = (w0[:, h:h + 1] * o0_ref[:, sl] + w1[:, h:h + 1] * o1_ref[:, sl]
             + w2[:, h:h + 1] * o2_ref[:, sl])
        a_ref[:, sl] = a.astype(a_ref.dtype)


def attn_merge(outs, lses, *, tm):
    rows, cols = outs[0].shape
    o_spec = pl.BlockSpec((tm, cols), lambda i: (i, 0))
    l_spec = pl.BlockSpec((tm, LANES), lambda i: (i, 0))
    return pl.pallas_call(
        _attn_merge_kernel,
        grid=(rows // tm,),
        in_specs=[o_spec] * A_GROUPS + [l_spec] * A_GROUPS,
        out_specs=o_spec,
        out_shape=jax.ShapeDtypeStruct((rows, cols), BF16),
        compiler_params=_params(1),
        name="attn_merge",
    )(*outs, *lses)


def _conv_kernel(u_ref, prev_ref, w_ref, b_ref, o_ref, full_ref, *, tt):
    t = pl.program_id(2)

    @pl.when(t == 0)
    def _():
        full_ref[0:SUBLANES, :] = prev_ref[...]

    u = u_ref[...]
    full_ref[SUBLANES:SUBLANES + tt, :] = u
    acc = u * w_ref[CONV_TAPS - 1:CONV_TAPS, :] + b_ref[...]
    for s in range(1, CONV_TAPS):
        acc = acc + full_ref[SUBLANES - s:SUBLANES - s + tt, :] * w_ref[CONV_TAPS - 1 - s:CONV_TAPS - s, :]
    o_ref[...] = _silu(acc)
    full_ref[0:SUBLANES, :] = u[tt - SUBLANES:tt, :]


def conv_silu(u, prev, w, bias, *, nseq, seq, col0, cols, tt, tc=512):
    nt = seq // tt
    c0 = col0 // tc
    return pl.pallas_call(
        functools.partial(_conv_kernel, tt=tt),
        grid=(nseq, cols // tc, nt),
        in_specs=[pl.BlockSpec((tt, tc), lambda s, c, t: (s * nt + t, c0 + c)),
                  pl.BlockSpec((None, SUBLANES, tc), lambda s, c, t: (s, 0, c)),
                  pl.BlockSpec((SUBLANES, tc), lambda s, c, t: (0, c)),
                  pl.BlockSpec((1, tc), lambda s, c, t: (0, c))],
        out_specs=pl.BlockSpec((tt, tc), lambda s, c, t: (s * nt + t, c)),
        out_shape=jax.ShapeDtypeStruct((nseq * seq, cols), F32),
        scratch_shapes=[pltpu.VMEM((SUBLANES + tt, tc), F32)],
        compiler_params=_params(3),
        name="conv_silu",
    )(u, prev, w, bias)


def _unit_lower_inverse(a, c):
    blk = min(B_INV_BLOCK, c)
    ii = lax.broadcasted_iota(jnp.int32, (c, c), 0)
    jj = lax.broadcasted_iota(jnp.int32, (c, c), 1)
    eye = jnp.where(ii == jj, 1.0, 0.0).astype(F32)
    shift = blk.bit_length() - 1
    same = (ii >> shift) == (jj >> shift)
    p = jnp.where(same, -a, 0.0)
    x = eye + p
    for _ in range(shift - 1):
        p = _dot(p, p, HIGHEST)
        x = x + _dot(x, p, HIGHEST)
    size = blk
    while size < c:
        shift += 1
        same_next = (ii >> shift) == (jj >> shift)
        off = jnp.where(jnp.logical_and(same_next, jnp.logical_not(same)), a, 0.0)
        x = x - _dot(_dot(x, off, HIGHEST), x, HIGHEST)
        same = same_next
        size *= 2
    return x


def _gdn_kernel(q_ref, k_ref, v_ref, z_ref, ba_ref, alog_ref, dtb_ref, nw_ref, s0_ref,
                o_ref, s_ref, *, c, hps, t_valid):
    hb = pl.program_id(1)
    ci = pl.program_id(2)
    dk = B_HEAD_DIM

    @pl.when(ci == 0)
    def _():
        s_ref[...] = s0_ref[...]

    ba = ba_ref[...]
    row = lax.broadcasted_iota(jnp.int32, (c, LANES), 0) + ci * c
    valid = row < t_valid
    beta_all = jnp.where(valid, jax.nn.sigmoid(ba), 0.0)
    g_all = jnp.where(valid, -jnp.exp(alog_ref[pl.ds(hb, 1), :]) * _softplus(ba + dtb_ref[pl.ds(hb, 1), :]), 0.0)
    ii = lax.broadcasted_iota(jnp.int32, (c, c), 0)
    jj = lax.broadcasted_iota(jnp.int32, (c, c), 1)
    incl = ii >= jj
    strict = ii > jj
    gc_all = _dot(jnp.where(incl, 1.0, 0.0).astype(F32), g_all, HIGHEST)
    gc_pad = jnp.concatenate([gc_all, jnp.zeros((LANES - c, LANES), F32)], axis=0) if c < LANES else gc_all
    gc_t = gc_pad.T
    nw = nw_ref[...]

    for qh in range(hps // 2):
        qs = slice(qh * dk, (qh + 1) * dk)
        q = q_ref[:, qs]
        k = k_ref[:, qs]
        q = q * lax.rsqrt(jnp.sum(q * q, axis=-1, keepdims=True) + 1e-6) * (dk ** -0.5)
        k = k * lax.rsqrt(jnp.sum(k * k, axis=-1, keepdims=True) + 1e-6)
        kk = _dot_nt(k, k, HIGHEST)
        qk = _dot_nt(q.astype(BF16), k.astype(BF16))
        for e in range(2):
            h = 2 * qh + e
            hs = slice(h * dk, (h + 1) * dk)
            beta = beta_all[:, h:h + 1]
            gc = gc_all[:, hps + h:hps + h + 1]
            gr = gc_t[hps + h:hps + h + 1, :c]
            g_last = gc_all[c - 1:c, hps + h:hps + h + 1]
            decay = jnp.exp(jnp.where(incl, gc - gr, NEG_BIG))
            a = jnp.where(strict, kk * decay, 0.0) * beta
            t_inv = _unit_lower_inverse(a, c)
            eg = jnp.exp(gc)
            v = v_ref[:, hs]
            rhs = jnp.concatenate([v * beta, k * (beta * eg)], axis=1)
            uw = _dot(t_inv, rhs, HIGHEST)
            u = uw[:, :dk]
            w = uw[:, dk:]
            s = s_ref[h]
            s_b = s.astype(BF16)
            v_new = u - _dot(w.astype(BF16), s_b)
            v_new_b = v_new.astype(BF16)
            o = (_dot((q * eg).astype(BF16), s_b)
                 + _dot(jnp.where(incl, qk * decay, 0.0).astype(BF16), v_new_b))
            k_dec = k * jnp.exp(g_last - gc)
            s_ref[h] = s * jnp.exp(g_last) + _dot_tn(k_dec.astype(BF16), v_new_b)
            o = o * lax.rsqrt(jnp.mean(o * o, axis=-1, keepdims=True) + RMS_EPS) * nw
            o_ref[:, hs] = (o * _silu(z_ref[:, hs])).astype(o_ref.dtype)


def gdn_scan(qkv_c, z_arr, z_block0, ba, alog_pad, dtb_pad, norm_w, s0, *, nseq, seq, t_valid):
    c = B_CHUNK
    hps = B_HEADS_PER_STEP
    nc = seq // c
    ng = B_V_HEADS // hps
    qw = hps // 2 * B_HEAD_DIM
    vw = hps * B_HEAD_DIM
    row = lambda s, h, ci: s * nc + ci
    return pl.pallas_call(
        functools.partial(_gdn_kernel, c=c, hps=hps, t_valid=t_valid),
        grid=(nseq, ng, nc),
        in_specs=[pl.BlockSpec((c, qw), lambda s, h, ci: (row(s, h, ci), h)),
                  pl.BlockSpec((c, qw), lambda s, h, ci: (row(s, h, ci), B_KEY_DIM // qw + h)),
                  pl.BlockSpec((c, vw), lambda s, h, ci: (row(s, h, ci), 2 * B_KEY_DIM // vw + h)),
                  pl.BlockSpec((c, vw), lambda s, h, ci: (row(s, h, ci), z_block0 + h)),
                  pl.BlockSpec((c, LANES), lambda s, h, ci: (row(s, h, ci), h)),
                  pl.BlockSpec((ng, LANES), lambda s, h, ci: (0, 0)),
                  pl.BlockSpec((ng, LANES), lambda s, h, ci: (0, 0)),
                  pl.BlockSpec((1, B_HEAD_DIM), lambda s, h, ci: (0, 0)),
                  pl.BlockSpec((None, hps, B_HEAD_DIM, B_HEAD_DIM), lambda s, h, ci: (s, h, 0, 0))],
        out_specs=[pl.BlockSpec((c, vw), lambda s, h, ci: (row(s, h, ci), h)),
                   pl.BlockSpec((None, hps, B_HEAD_DIM, B_HEAD_DIM), lambda s, h, ci: (s, h, 0, 0))],
        out_shape=[jax.ShapeDtypeStruct((nseq * seq, B_VAL_DIM), BF16),
                   jax.ShapeDtypeStruct((nseq, B_V_HEADS, B_HEAD_DIM, B_HEAD_DIM), F32)],
        compiler_params=_params(3),
        name="gdn_scan",
    )(qkv_c, qkv_c, qkv_c, z_arr, ba, alog_pad, dtb_pad, norm_w, s0)


def _ssd_kernel(x_ref, b_ref, c_ref, z_ref, dt_ref, dtb_ref, alog_ref, dskip_ref, nw_ref, h0_ref,
                y_ref, hout_ref, ht_ref, *, c, nc, t_valid):
    g = pl.program_id(1)
    ci = pl.program_id(2)
    hpg = C_HEADS_PER_GROUP
    p = C_HEAD_DIM
    gcols = hpg * p

    @pl.when(ci == 0)
    def _():
        ht_ref[...] = h0_ref[...].reshape(gcols, C_D_STATE).T

    row = lax.broadcasted_iota(jnp.int32, (c, LANES), 0) + ci * c
    dt = jnp.where(row < t_valid, _softplus(dt_ref[...] + dtb_ref[pl.ds(g, 1), :]), 0.0)
    la = dt * -jnp.exp(alog_ref[pl.ds(g, 1), :])
    ii = lax.broadcasted_iota(jnp.int32, (c, c), 0)
    jj = lax.broadcasted_iota(jnp.int32, (c, c), 1)
    incl = ii >= jj
    acs = _dot(jnp.where(incl, 1.0, 0.0).astype(F32), la, HIGHEST)
    acs_t = acs.T
    el = lax.broadcasted_iota(jnp.int32, (LANES, gcols), 0)
    ej = lax.broadcasted_iota(jnp.int32, (LANES, gcols), 1)
    expand = jnp.where(el == (ej >> (p.bit_length() - 1)), 1.0, 0.0).astype(F32)
    dt_e = _dot(dt, expand, HIGHEST)
    acs_e = _dot(acs, expand, HIGHEST)
    dskip_e = _dot(jnp.broadcast_to(dskip_ref[pl.ds(g, 1), :], (SUBLANES, LANES)), expand, HIGHEST)[0:1, :]

    x = x_ref[...]
    bm = b_ref[...].astype(BF16)
    cm = c_ref[...].astype(BF16)
    xdt = (x * dt_e).astype(BF16)
    cb = _dot_nt(cm, bm)
    ht = ht_ref[...]
    y = _dot(cm, ht.astype(BF16)) * jnp.exp(acs_e) + dskip_e * x
    lane = lax.broadcasted_iota(jnp.int32, (c, 2 * p), 1)
    pairs = []
    for j in range(hpg // 2):
        xs = xdt[:, j * 2 * p:(j + 1) * 2 * p]
        ys = []
        for e in range(2):
            hd = 2 * j + e
            decay = jnp.exp(jnp.where(incl, acs[:, hd:hd + 1] - acs_t[hd:hd + 1, :], NEG_BIG))
            ys.append(_dot((cb * decay).astype(BF16), xs))
        pairs.append(jnp.where(lane < p, ys[0], ys[1]))
    y = y + jnp.concatenate(pairs, axis=1)
    y = y * _silu(z_ref[...])
    y = y * lax.rsqrt(jnp.mean(y * y, axis=-1, keepdims=True) + RMS_EPS) * nw_ref[...]
    y_ref[...] = y.astype(y_ref.dtype)

    last_e = acs_e[c - 1:c, :]
    xw = (x * (jnp.exp(last_e - acs_e) * dt_e)).astype(BF16)
    ht_new = ht * jnp.exp(last_e) + _dot_tn(bm, xw)
    ht_ref[...] = ht_new

    @pl.when(ci == nc - 1)
    def _():
        hout_ref[...] = ht_new.T.reshape(hpg, p, C_D_STATE)


def ssd_scan(xc, z_arr, dt, dtb_pad, alog_pad, dskip_pad, norm_w, h0, *, nseq, seq, t_valid):
    c = C_CHUNK
    nc = seq // c
    gc = C_GROUP_COLS
    b0 = C_D_INNER // C_D_STATE
    row = lambda s, g, ci: s * nc + ci
    par = pl.BlockSpec((C_GROUPS, LANES), lambda s, g, ci: (0, 0))
    return pl.pallas_call(
        functools.partial(_ssd_kernel, c=c, nc=nc, t_valid=t_valid),
        grid=(nseq, C_GROUPS, nc),
        in_specs=[pl.BlockSpec((c, gc), lambda s, g, ci: (row(s, g, ci), g)),
                  pl.BlockSpec((c, C_D_STATE), lambda s, g, ci: (row(s, g, ci), b0 + g)),
                  pl.BlockSpec((c, C_D_STATE), lambda s, g, ci: (row(s, g, ci), b0 + C_GROUPS + g)),
                  pl.BlockSpec((c, gc), lambda s, g, ci: (row(s, g, ci), g)),
                  pl.BlockSpec((c, LANES), lambda s, g, ci: (row(s, g, ci), g)),
                  par, par, par,
                  pl.BlockSpec((1, gc), lambda s, g, ci: (0, g)),
                  pl.BlockSpec((None, C_HEADS_PER_GROUP, C_HEAD_DIM, C_D_STATE), lambda s, g, ci: (s, g, 0, 0))],
        out_specs=[pl.BlockSpec((c, gc), lambda s, g, ci: (row(s, g, ci), g)),
                   pl.BlockSpec((None, C_HEADS_PER_GROUP, C_HEAD_DIM, C_D_STATE), lambda s, g, ci: (s, g, 0, 0))],
        out_shape=[jax.ShapeDtypeStruct((nseq * seq, C_D_INNER), BF16),
                   jax.ShapeDtypeStruct((nseq, C_HEADS, C_HEAD_DIM, C_D_STATE), F32)],
        scratch_shapes=[pltpu.VMEM((C_D_STATE, gc), F32)],
        compiler_params=_params(3),
        name="ssd_scan",
    )(xc, xc, xc, z_arr, dt, dtb_pad, alog_pad, dskip_pad, norm_w, h0)


def _rope_tables(mp, seq, n_sample, ts):
    pos = jnp.concatenate([jnp.arange(mp) % seq, PAST_LEN + jnp.arange(n_sample) % ts]).astype(F32)
    inv_freq = ROPE_THETA ** (-jnp.arange(A_ROT_HALF, dtype=F32) / A_ROT_HALF)
    ang = pos[:, None] * inv_freq[None, :]
    cos, sin = jnp.cos(ang), jnp.sin(ang)
    rest = A_HEAD_DIM - 2 * A_ROT_HALF
    rows = pos.shape[0]
    cos_full = jnp.concatenate([cos, cos, jnp.ones((rows, rest), F32)], axis=1)
    sin_full = jnp.concatenate([-sin, sin, jnp.zeros((rows, rest), F32)], axis=1)
    return cos_full, sin_full


def _lane_groups(cols, group):
    lead = cols.shape[:-1]
    n = cols.shape[-1]
    c = cols.reshape(*lead, n // group, group)
    c = jnp.pad(c, [(0, 0)] * (c.ndim - 1) + [(0, LANES - group)])
    return c.reshape(*lead, n // group * LANES)


def _pad_seq_rows(x, nseq, ts, seq):
    cols = x.shape[-1]
    x = jnp.pad(x.reshape(nseq, ts, cols), ((0, 0), (0, seq - ts), (0, 0)))
    return x.reshape(nseq * seq, cols)


def _conv_prev(state):
    return jnp.pad(state, ((0, 0), (SUBLANES - state.shape[1], 0), (0, 0)))


def _conv_taps(w):
    return jnp.pad(w, ((0, SUBLANES - w.shape[0]), (0, 0)))


def kernel(x_prompt, x_sample, cache_a_kv_w128, cache_a_kv_w512, cache_a_kv_w2048, state_b_ssm, state_b_conv, state_c_ssm, state_c_conv, ln_g, ln_b, ffn1_w_gu, ffn1_w_down, ffn2_w_gu, ffn2_w_down, a_w_qkv, a_w_o, b_w_in, b_conv_w, b_a_log, b_dt_bias, b_norm_w, b_w_out, c_w_in, c_conv_w, c_conv_b, c_dt_bias, c_a_log, c_d, c_norm_w, c_w_out):
    a_bufs = (cache_a_kv_w128, cache_a_kv_w512, cache_a_kv_w2048)
    bp, seq, d = x_prompt.shape
    db, ts, _ = x_sample.shape
    mp, ms = bp * seq, db * ts
    x = jnp.concatenate([x_prompt.reshape(mp, d), x_sample.reshape(ms, d)], axis=0)
    xb = x.astype(BF16)
    dff = ffn1_w_down.shape[1]
    cos_full, sin_full = _rope_tables(mp, seq, ms, ts)

    def ffn(x, xb, w_gu, w_down, g, b):
        h = swiglu_up(xb, w_gu.astype(BF16))
        return matmul_postnorm(h, w_down.astype(BF16), x, g, b, scale=0.5, tk=dff // 4)

    def with_sample_rows(a_p, a_s, rows):
        return jnp.concatenate([a_p, a_s.reshape(db, rows, -1)[:, :ts].reshape(ms, -1)], axis=0)

    outs = {}

    def mixer_a(xb, j):
        qkv = matmul(xb, a_w_qkv[j].astype(BF16), tn=512)
        qk = rope_qk(qkv, cos_full, sin_full)
        gw = A_GROUP_COLS
        o_p, l_p, o_s, l_s = [], [], [], []
        for g in range(A_GROUPS):
            o, lse = attn_prompt(qk, qkv, g=g, batch=bp, seq=seq)
            o_p.append(o)
            l_p.append(lse)
            col = lambda sec: slice((sec * A_GROUPS + g) * gw, (sec * A_GROUPS + g + 1) * gw)
            q_s = qk[mp:, col(0)].reshape(db, ts, gw)
            k_s = qk[mp:, col(1)].reshape(db, ts, gw)
            v_s = qkv[mp:, col(2)].reshape(db, ts, gw)
            o, lse = attn_sample(q_s, k_s, v_s, a_bufs[g][j], g=g)
            o_s.append(o.reshape(ms, gw))
            l_s.append(lse.reshape(ms, LANES))
            keep = min(A_BLOCK * A_DILATIONS[g], seq)
            k_p = qk[:mp, col(1)].reshape(bp, seq, A_HEADS, A_HEAD_DIM)[:, seq - keep:]
            v_p = qkv[:mp, col(2)].reshape(bp, seq, A_HEADS, A_HEAD_DIM)[:, seq - keep:]
            outs.setdefault("a_p%d" % g, []).append(jnp.stack([k_p, v_p], axis=2))
            new_rows = jnp.stack([k_s, v_s], axis=2).reshape(db, ts, 2, A_HEADS, A_HEAD_DIM)
            outs.setdefault("a_s%d" % g, []).append(jnp.concatenate([a_bufs[g][j][:, ts:], new_rows], axis=1))
        a_p = attn_merge(o_p, l_p, tm=ROW_TILE)
        a_s = attn_merge(o_s, l_s, tm=ms)
        return jnp.concatenate([a_p, a_s], axis=0), a_w_o[j], a_w_o.shape[1]

    def mixer_b(xb, j):
        hps = B_HEADS_PER_STEP
        w_in = b_w_in[j]
        main = B_CONV_DIM + B_VAL_DIM
        pb = matmul(xb, w_in[:, :main].astype(BF16), tn=512)
        w_tail = jnp.concatenate([w_in[:, main:main + B_V_HEADS].reshape(d, -1, hps),
                                  w_in[:, main + B_V_HEADS:].reshape(d, -1, hps)], axis=2)
        ba = matmul(xb, _lane_groups(w_tail.reshape(d, -1), 2 * hps).astype(BF16), tn=B_V_HEADS // hps * LANES)
        gate_lanes = lambda v: _lane_groups(
            jnp.concatenate([jnp.zeros((B_V_HEADS // hps, hps), F32), v.reshape(-1, hps)], axis=1).reshape(-1), 2 * hps
        ).reshape(-1, LANES)
        alog_pad, dtb_pad = gate_lanes(b_a_log[j]), gate_lanes(b_dt_bias[j])
        taps = _conv_taps(b_conv_w[j])
        no_bias = jnp.zeros((1, B_CONV_DIM), F32)
        nw = b_norm_w[j].reshape(1, B_HEAD_DIM)
        qkv_c = conv_silu(pb, jnp.zeros((bp, SUBLANES, B_CONV_DIM), F32), taps, no_bias,
                          nseq=bp, seq=seq, col0=0, cols=B_CONV_DIM, tt=512)
        a_p, s_p = gdn_scan(qkv_c, pb, B_CONV_DIM // (hps * B_HEAD_DIM), ba, alog_pad, dtb_pad, nw,
                            jnp.zeros((bp, B_V_HEADS, B_HEAD_DIM, B_HEAD_DIM), F32),
                            nseq=bp, seq=seq, t_valid=seq)
        pb_s = _pad_seq_rows(pb[mp:], db, ts, B_CHUNK)
        ba_s = _pad_seq_rows(ba[mp:], db, ts, B_CHUNK)
        qkv_cs = conv_silu(pb_s, _conv_prev(state_b_conv[j]), taps, no_bias,
                           nseq=db, seq=B_CHUNK, col0=0, cols=B_CONV_DIM, tt=B_CHUNK)
        a_s, s_s = gdn_scan(qkv_cs, pb_s, B_CONV_DIM // (hps * B_HEAD_DIM), ba_s, alog_pad, dtb_pad, nw,
                            state_b_ssm[j], nseq=db, seq=B_CHUNK, t_valid=ts)
        outs.setdefault("b_ssm_p", []).append(s_p)
        outs.setdefault("b_ssm_s", []).append(s_s)
        outs.setdefault("b_conv_p", []).append(pb[:mp, :B_CONV_DIM].reshape(bp, seq, -1)[:, seq - 3:])
        outs.setdefault("b_conv_s", []).append(pb[mp:, :B_CONV_DIM].reshape(db, ts, -1)[:, ts - 3:])
        return with_sample_rows(a_p, a_s, B_CHUNK), b_w_out[j], 1024

    def mixer_c(xb, j):
        w_in = c_w_in[j]
        main = C_D_INNER + C_CONV_DIM
        pc = matmul(xb, w_in[:, :main].astype(BF16), tn=512)
        dt = matmul(xb, _lane_groups(w_in[:, main:], C_HEADS_PER_GROUP).astype(BF16), tn=C_GROUPS * LANES)
        head_lanes = lambda v: _lane_groups(v, C_HEADS_PER_GROUP).reshape(C_GROUPS, LANES)
        dtb_pad, alog_pad, dskip_pad = head_lanes(c_dt_bias[j]), head_lanes(c_a_log[j]), head_lanes(c_d[j])
        taps = _conv_taps(c_conv_w[j])
        bias = c_conv_b[j].reshape(1, C_CONV_DIM)
        nw = c_norm_w[j].reshape(1, C_D_INNER)
        xc = conv_silu(pc, jnp.zeros((bp, SUBLANES, C_CONV_DIM), F32), taps, bias,
                       nseq=bp, seq=seq, col0=C_D_INNER, cols=C_CONV_DIM, tt=512)
        a_p, h_p = ssd_scan(xc, pc, dt, dtb_pad, alog_pad, dskip_pad, nw,
                            jnp.zeros((bp, C_HEADS, C_HEAD_DIM, C_D_STATE), F32),
                            nseq=bp, seq=seq, t_valid=seq)
        pc_s = _pad_seq_rows(pc[mp:], db, ts, C_CHUNK)
        dt_s = _pad_seq_rows(dt[mp:], db, ts, C_CHUNK)
        xc_s = conv_silu(pc_s, _conv_prev(state_c_conv[j]), taps, bias,
                         nseq=db, seq=C_CHUNK, col0=C_D_INNER, cols=C_CONV_DIM, tt=C_CHUNK)
        a_s, h_s = ssd_scan(xc_s, pc_s, dt_s, dtb_pad, alog_pad, dskip_pad, nw, state_c_ssm[j],
                            nseq=db, seq=C_CHUNK, t_valid=ts)
        outs.setdefault("c_ssm_p", []).append(h_p)
        outs.setdefault("c_ssm_s", []).append(h_s)
        outs.setdefault("c_conv_p", []).append(pc[:mp].reshape(bp, seq, -1)[:, seq - 3:, C_D_INNER:])
        outs.setdefault("c_conv_s", []).append(pc[mp:].reshape(db, ts, -1)[:, ts - 3:, C_D_INNER:])
        return with_sample_rows(a_p, a_s, C_CHUNK), c_w_out[j], 1024

    mixers = (mixer_a, mixer_b, mixer_c)
    for i in range(DEPTH):
        x, xb = ffn(x, xb, ffn1_w_gu[i], ffn1_w_down[i], ln_g[i, 0], ln_b[i, 0])
        a, w_out, tk = mixers[i % 3](xb, i // 3)
        x, xb = matmul_postnorm(a, w_out.astype(BF16), x, ln_g[i, 1], ln_b[i, 1], scale=1.0, tk=tk)
        x, xb = ffn(x, xb, ffn2_w_gu[i], ffn2_w_down[i], ln_g[i, 2], ln_b[i, 2])

    st = lambda name: jnp.stack(outs[name])
    return (x[:mp].reshape(bp, seq, d), x[mp:].reshape(db, ts, d),
            st("a_p0"), st("a_s0"), st("a_p1"), st("a_s1"), st("a_p2"), st("a_s2"),
            st("b_ssm_p"), st("b_ssm_s"), st("b_conv_p"), st("b_conv_s"),
            st("c_ssm_p"), st("c_ssm_s"), st("c_conv_p"), st("c_conv_s"))
```

```python
import functools

import jax
import jax.numpy as jnp
from jax import lax
from jax.experimental import pallas as pl
from jax.experimental.pallas import tpu as pltpu

F32 = jnp.float32
BF16 = jnp.bfloat16

VMEM_LIMIT_BYTES = 56 * 1024 * 1024
LANES = 128
SUBLANES = 8
ROW_TILE = 512
NEG_BIG = -1e30

DEPTH = 4
DN_ALPHA = (2.0 * DEPTH) ** 0.25
LN_EPS = 1e-5
RMS_EPS = 1e-6
PAST_LEN = 16384

A_DILATIONS = (1, 4, 16)
A_GROUPS = 3
A_HEADS = 8
A_HEAD_DIM = 128
A_ROT_HALF = A_HEAD_DIM // 8
A_BLOCK = 128
A_GROUP_COLS = A_HEADS * A_HEAD_DIM
A_SECTIONS = 3
A_SPLIT_TILE = 2048
ROPE_THETA = 500000.0

B_QK_HEADS = 16
B_V_HEADS = 32
B_HEAD_DIM = 128
B_KEY_DIM = B_QK_HEADS * B_HEAD_DIM
B_VAL_DIM = B_V_HEADS * B_HEAD_DIM
B_CONV_DIM = 2 * B_KEY_DIM + B_VAL_DIM
B_CHUNK = 64
B_HEADS_PER_STEP = 8
B_INV_BLOCK = 16
B_INV_PASSES = 1

C_D_INNER = 4096
C_HEADS = 64
C_HEAD_DIM = 64
C_GROUPS = 8
C_HEADS_PER_GROUP = C_HEADS // C_GROUPS
C_GROUP_COLS = C_D_INNER // C_GROUPS
C_D_STATE = 128
C_CONV_DIM = C_D_INNER + 2 * C_GROUPS * C_D_STATE
C_CHUNK = 128

CONV_TAPS = 4


def _params(n_axes):
    return pltpu.CompilerParams(
        dimension_semantics=("arbitrary",) * n_axes,
        vmem_limit_bytes=VMEM_LIMIT_BYTES,
    )


def _dot(a, b):
    return jnp.dot(a, b, preferred_element_type=F32)


def _dot_nt(a, b):
    return lax.dot_general(a, b, (((1,), (1,)), ((), ())), preferred_element_type=F32)


def _dot_tn(a, b):
    return lax.dot_general(a, b, (((0,), (0,)), ((), ())), preferred_element_type=F32)


def _split3(x):
    hi = x.astype(BF16)
    r = x - hi.astype(F32)
    mid = r.astype(BF16)
    lo = (r - mid.astype(F32)).astype(BF16)
    return hi, mid, lo


def _select_rows(sel, x):
    return sum(_dot(sel, part) for part in _split3(x))


def _select_cols(x, sel):
    return sum(_dot(part, sel) for part in _split3(x))


def _mm(a, b, passes):
    a_hi, b_hi = a.astype(BF16), b.astype(BF16)
    out = _dot(a_hi, b_hi)
    if passes == 3:
        a_lo = (a - a_hi.astype(F32)).astype(BF16)
        b_lo = (b - b_hi.astype(F32)).astype(BF16)
        out = out + _dot(a_hi, b_lo) + _dot(a_lo, b_hi)
    return out


def _silu(x):
    return x * jax.nn.sigmoid(x)


def _softplus(x):
    return jnp.maximum(x, 0.0) + jnp.log(1.0 + jnp.exp(-jnp.abs(x)))


def _row_tiles(rows, tm):
    n_full, rem = divmod(rows, tm)
    return n_full + (1 if rem else 0), n_full, rem


def _per_row_tile(i, n_full, rem, tm, body):
    if rem == 0:
        body(tm)
        return

    @pl.when(i < n_full)
    def _():
        body(tm)

    @pl.when(i == n_full)
    def _():
        body(rem)


def _mm_kernel(x_ref, w_ref, o_ref, *, tm, n_full, rem):
    def body(r):
        o_ref[:r, :] = _dot(x_ref[:r, :], w_ref[...]).astype(o_ref.dtype)

    _per_row_tile(pl.program_id(0), n_full, rem, tm, body)


def matmul(x, w, *, tn, out_dtype=F32, tm=2 * ROW_TILE):
    rows, k = x.shape
    n = w.shape[1]
    steps, n_full, rem = _row_tiles(rows, tm)
    return pl.pallas_call(
        functools.partial(_mm_kernel, tm=tm, n_full=n_full, rem=rem),
        grid=(steps, n // tn),
        in_specs=[pl.BlockSpec((tm, k), lambda i, j: (i, 0)),
                  pl.BlockSpec((k, tn), lambda i, j: (0, j))],
        out_specs=pl.BlockSpec((tm, tn), lambda i, j: (i, j)),
        out_shape=jax.ShapeDtypeStruct((rows, n), out_dtype),
        compiler_params=_params(2),
        name="matmul",
    )(x, w)


def _swiglu_kernel(x_ref, wg_ref, wu_ref, o_ref, *, tm, n_full, rem):
    def body(r):
        x = x_ref[:r, :]
        gate = _dot(x, wg_ref[...])
        up = _dot(x, wu_ref[...])
        o_ref[:r, :] = (_silu(gate) * up).astype(o_ref.dtype)

    _per_row_tile(pl.program_id(0), n_full, rem, tm, body)


def swiglu_up(x, w_gu, *, tn=512, tm=2 * ROW_TILE):
    rows, k = x.shape
    f = w_gu.shape[1] // 2
    steps, n_full, rem = _row_tiles(rows, tm)
    nj = f // tn
    return pl.pallas_call(
        functools.partial(_swiglu_kernel, tm=tm, n_full=n_full, rem=rem),
        grid=(steps, nj),
        in_specs=[pl.BlockSpec((tm, k), lambda i, j: (i, 0)),
                  pl.BlockSpec((k, tn), lambda i, j: (0, j)),
                  pl.BlockSpec((k, tn), lambda i, j: (0, j + nj))],
        out_specs=pl.BlockSpec((tm, tn), lambda i, j: (i, j)),
        out_shape=jax.ShapeDtypeStruct((rows, f), BF16),
        compiler_params=_params(2),
        name="swiglu_up",
    )(x, w_gu, w_gu)


def _postnorm_kernel(a_ref, w_ref, x_ref, g_ref, b_ref, y_ref, yb_ref, acc_ref,
                     *, scale, nk, tm, n_full, rem):
    k = pl.program_id(1)

    def body(r):
        part = _dot(a_ref[:r, :], w_ref[...])

        @pl.when(k == 0)
        def _():
            acc_ref[:r, :] = part

        @pl.when(k > 0)
        def _():
            acc_ref[:r, :] += part

        @pl.when(k == nk - 1)
        def _():
            y = DN_ALPHA * x_ref[:r, :] + scale * acc_ref[:r, :]
            mu = jnp.mean(y, axis=-1, keepdims=True)
            yc = y - mu
            var = jnp.mean(yc * yc, axis=-1, keepdims=True)
            out = yc * lax.rsqrt(var + LN_EPS) * g_ref[...] + b_ref[...]
            y_ref[:r, :] = out
            yb_ref[:r, :] = out.astype(BF16)

    _per_row_tile(pl.program_id(0), n_full, rem, tm, body)


def matmul_postnorm(a, w, x, g, b, *, scale, tk, tm=ROW_TILE):
    rows, kdim = a.shape
    d = w.shape[1]
    nk = kdim // tk
    steps, n_full, rem = _row_tiles(rows, tm)
    return pl.pallas_call(
        functools.partial(_postnorm_kernel, scale=scale, nk=nk, tm=tm, n_full=n_full, rem=rem),
        grid=(steps, nk),
        in_specs=[pl.BlockSpec((tm, tk), lambda i, k: (i, k)),
                  pl.BlockSpec((tk, d), lambda i, k: (k, 0)),
                  pl.BlockSpec((tm, d), lambda i, k: (i, 0)),
                  pl.BlockSpec((1, d), lambda i, k: (0, 0)),
                  pl.BlockSpec((1, d), lambda i, k: (0, 0))],
        out_specs=[pl.BlockSpec((tm, d), lambda i, k: (i, 0)),
                   pl.BlockSpec((tm, d), lambda i, k: (i, 0))],
        out_shape=[jax.ShapeDtypeStruct((rows, d), F32),
                   jax.ShapeDtypeStruct((rows, d), BF16)],
        scratch_shapes=[pltpu.VMEM((tm, d), F32)],
        compiler_params=_params(2),
        name="matmul_postnorm",
    )(a, w, x, g.reshape(1, d), b.reshape(1, d))


def _strided_rows(r, n, stride):
    return pl.ds(r, n, stride=stride) if stride > 1 else pl.ds(0, n)


def _qkv_split_kernel(x_ref, cos_ref, sin_ref, o_ref, *, dil, n):
    rotated = pl.program_id(2) < (A_SECTIONS - 1) * A_HEADS
    lane = lax.broadcasted_iota(jnp.int32, (n, A_HEAD_DIM), 1)

    def emit(rotate):
        for r in range(dil):
            rows = _strided_rows(r, n, dil)
            x = x_ref[rows, :]
            if rotate:
                partner = jnp.where(lane < A_ROT_HALF,
                                    pltpu.roll(x, A_HEAD_DIM - A_ROT_HALF, axis=1),
                                    pltpu.roll(x, A_ROT_HALF, axis=1))
                x = x * cos_ref[rows, :] + partner * sin_ref[rows, :]
            o_ref[r] = x

    @pl.when(rotated)
    def _():
        emit(True)

    @pl.when(jnp.logical_not(rotated))
    def _():
        emit(False)


def qkv_split(qkv, cos_full, sin_full, *, g, dil, row0, batch, seq, tile):
    n = tile // dil
    tps = seq // tile
    r0 = row0 // tile
    in_col = lambda c: (c // A_HEADS * A_GROUPS + g) * A_HEADS + c % A_HEADS
    return pl.pallas_call(
        functools.partial(_qkv_split_kernel, dil=dil, n=n),
        grid=(batch, tps, A_SECTIONS * A_HEADS),
        in_specs=[pl.BlockSpec((tile, A_HEAD_DIM), lambda b, i, c: (r0 + b * tps + i, in_col(c))),
                  pl.BlockSpec((tile, A_HEAD_DIM), lambda b, i, c: (r0 + b * tps + i, 0)),
                  pl.BlockSpec((tile, A_HEAD_DIM), lambda b, i, c: (r0 + b * tps + i, 0))],
        out_specs=pl.BlockSpec((None, dil, n, A_HEAD_DIM), lambda b, i, c: (b, 0, i, c)),
        out_shape=jax.ShapeDtypeStruct((batch, dil, seq // dil, A_SECTIONS * A_GROUP_COLS), F32),
        compiler_params=_params(3),
        name="qkv_split",
    )(qkv, cos_full, sin_full)


def _attn_prompt_kernel(q_ref, kp_ref, kc_ref, vp_ref, vc_ref, o_ref, lse_ref):
    lb = pl.program_id(2)
    n = A_BLOCK
    qi = lax.broadcasted_iota(jnp.int32, (n, n), 0)
    kj = lax.broadcasted_iota(jnp.int32, (n, n), 1)
    mask_cur = kj <= qi
    mask_prev = jnp.logical_and(kj >= qi, lb > 0)
    scale = A_HEAD_DIM ** -0.5
    lse_all = jnp.zeros((n, LANES), F32)
    for h in range(A_HEADS):
        sl = slice(h * A_HEAD_DIM, (h + 1) * A_HEAD_DIM)
        q = q_ref[:, sl].astype(BF16)
        s_c = jnp.where(mask_cur, _dot_nt(q, kc_ref[:, sl].astype(BF16)) * scale, NEG_BIG)
        s_p = jnp.where(mask_prev, _dot_nt(q, kp_ref[:, sl].astype(BF16)) * scale, NEG_BIG)
        m = jnp.maximum(jnp.max(s_c, axis=-1, keepdims=True), jnp.max(s_p, axis=-1, keepdims=True))
        p_c = jnp.exp(s_c - m)
        p_p = jnp.exp(s_p - m)
        l = jnp.sum(p_c, axis=-1, keepdims=True) + jnp.sum(p_p, axis=-1, keepdims=True)
        o = _dot(p_c.astype(BF16), vc_ref[:, sl].astype(BF16)) + _dot(p_p.astype(BF16), vp_ref[:, sl].astype(BF16))
        o_ref[:, sl] = o / l
        lse_all = jnp.where(kj == h, m + jnp.log(l), lse_all)
    lse_ref[...] = lse_all


def attn_prompt(qkv_r):
    batch, dil, length, _ = qkv_r.shape
    nb = length // A_BLOCK
    spec = lambda sec, back: pl.BlockSpec(
        (None, None, A_BLOCK, A_GROUP_COLS), lambda b, r, lb: (b, r, jnp.maximum(lb - back, 0), sec))
    return pl.pallas_call(
        _attn_prompt_kernel,
        grid=(batch, dil, nb),
        in_specs=[spec(0, 0), spec(1, 1), spec(1, 0), spec(2, 1), spec(2, 0)],
        out_specs=[pl.BlockSpec((None, None, A_BLOCK, A_GROUP_COLS), lambda b, r, lb: (b, r, lb, 0)),
                   pl.BlockSpec((None, None, A_BLOCK, LANES), lambda b, r, lb: (b, r, lb, 0))],
        out_shape=[jax.ShapeDtypeStruct((batch, dil, length, A_GROUP_COLS), F32),
                   jax.ShapeDtypeStruct((batch, dil, length, LANES), F32)],
        compiler_params=_params(3),
        name="attn_prompt",
    )(qkv_r, qkv_r, qkv_r, qkv_r, qkv_r)


def _attn_sample_kernel(q_ref, kn_ref, vn_ref, *refs, dil, ts):
    n_res = len(refs) // 2 - 1
    cache_refs, (o_ref, lse_ref) = refs[:2 * n_res], refs[2 * n_res:]
    scale = A_HEAD_DIM ** -0.5
    jc = lax.broadcasted_iota(jnp.int32, (A_BLOCK, 1), 0)
    un = lax.broadcasted_iota(jnp.int32, (ts, 1), 0)
    lane = lax.broadcasted_iota(jnp.int32, (1, LANES), 1)
    for t in range(ts):
        kc_ref, vc_ref = cache_refs[2 * (t % n_res)], cache_refs[2 * (t % n_res) + 1]
        valid_n = (un <= t) if dil == 1 else (un == t)
        lse_row = jnp.zeros((1, LANES), F32)
        for h in range(A_HEADS):
            sl = slice(h * A_HEAD_DIM, (h + 1) * A_HEAD_DIM)
            q = q_ref[0, t:t + 1, sl]
            s_c = jnp.sum(kc_ref[:, sl] * q, axis=-1, keepdims=True) * scale
            if dil == 1:
                s_c = jnp.where(jc >= t, s_c, NEG_BIG)
            s_n = jnp.where(valid_n, jnp.sum(kn_ref[0, :, sl] * q, axis=-1, keepdims=True) * scale, NEG_BIG)
            m = jnp.maximum(jnp.max(s_c, axis=0, keepdims=True), jnp.max(s_n, axis=0, keepdims=True))
            p_c = jnp.exp(s_c - m)
            p_n = jnp.exp(s_n - m)
            l = jnp.sum(p_c, axis=0, keepdims=True) + jnp.sum(p_n, axis=0, keepdims=True)
            o = (jnp.sum(p_c * vc_ref[:, sl], axis=0, keepdims=True)
                 + jnp.sum(p_n * vn_ref[0, :, sl], axis=0, keepdims=True))
            o_ref[0, t:t + 1, sl] = o / l
            lse_row = jnp.where(lane == h, m + jnp.log(l), lse_row)
        lse_ref[0, t:t + 1, :] = lse_row


def attn_sample(q, k_new, v_new, caches, *, layer, dil):
    db, ts, cols = q.shape
    wb = caches.shape[2]
    assert wb == A_BLOCK * dil and (dil == 1 or ts <= dil)
    cache_v = caches.reshape(caches.shape[0], db, wb // dil, dil * 2 * cols)
    n_res = 1 if dil == 1 else ts
    tok_spec = pl.BlockSpec((1, ts, cols), lambda b: (b, 0, 0))
    cache_specs = [pl.BlockSpec((None, None, A_BLOCK, cols), lambda b, blk=blk: (layer, b, 0, blk))
                   for blk in range(2 * n_res)]
    return pl.pallas_call(
        functools.partial(_attn_sample_kernel, dil=dil, ts=ts),
        grid=(db,),
        in_specs=[tok_spec, tok_spec, tok_spec] + cache_specs,
        out_specs=[pl.BlockSpec((1, ts, cols), lambda b: (b, 0, 0)),
                   pl.BlockSpec((1, ts, LANES), lambda b: (b, 0, 0))],
        out_shape=[jax.ShapeDtypeStruct((db, ts, cols), F32),
                   jax.ShapeDtypeStruct((db, ts, LANES), F32)],
        compiler_params=_params(1),
        name="attn_sample",
    )(q, k_new, v_new, *([cache_v] * (2 * n_res)))


def _merge_groups(head_out, lses, a_ref):
    m = functools.reduce(jnp.maximum, lses)
    es = [jnp.exp(l - m) for l in lses]
    den = sum(es)
    ws = [e / den for e in es]
    for h in range(A_HEADS):
        a = sum(w[:, h:h + 1] * head_out(g, h) for g, w in enumerate(ws))
        a_ref[:, h * A_HEAD_DIM:(h + 1) * A_HEAD_DIM] = a.astype(a_ref.dtype)


def _attn_merge_rows_kernel(*refs):
    o_refs = refs[:A_GROUPS]
    head_out = lambda g, h: o_refs[g][:, h * A_HEAD_DIM:(h + 1) * A_HEAD_DIM]
    _merge_groups(head_out, [l[...] for l in refs[A_GROUPS:2 * A_GROUPS]], refs[2 * A_GROUPS])


def attn_merge_rows(outs, lses):
    rows, cols = outs[0].shape
    o_spec = pl.BlockSpec((rows, cols), lambda i: (0, 0))
    l_spec = pl.BlockSpec((rows, LANES), lambda i: (0, 0))
    return pl.pallas_call(
        _attn_merge_rows_kernel,
        grid=(1,),
        in_specs=[o_spec] * A_GROUPS + [l_spec] * A_GROUPS,
        out_specs=o_spec,
        out_shape=jax.ShapeDtypeStruct((rows, cols), BF16),
        compiler_params=_params(1),
        name="attn_merge_rows",
    )(*outs, *lses)


def _attn_merge_prompt_kernel(*refs, tile):
    o_refs, l_refs = refs[:A_GROUPS], refs[A_GROUPS:2 * A_GROUPS]
    a_ref, o_nat, l_nat = refs[2 * A_GROUPS:]
    lses = []
    for g, dil in enumerate(A_DILATIONS):
        if dil == 1:
            lses.append(l_refs[g][0])
            continue
        n = tile // dil
        for r in range(dil):
            rows = pl.ds(r, n, stride=dil)
            l_nat[g, rows, :] = l_refs[g][r]
            for h in range(A_HEADS):
                o_nat[g * A_HEADS + h, rows, :] = o_refs[g][r, :, h * A_HEAD_DIM:(h + 1) * A_HEAD_DIM]
        lses.append(l_nat[g])

    def head_out(g, h):
        if A_DILATIONS[g] == 1:
            return o_refs[g][0, :, h * A_HEAD_DIM:(h + 1) * A_HEAD_DIM]
        return o_nat[g * A_HEADS + h]

    _merge_groups(head_out, lses, a_ref)


def attn_merge_prompt(outs, lses, *, tile=ROW_TILE):
    batch, _, seq, cols = outs[0].shape
    tps = seq // tile
    in_specs = []
    for width in (cols, LANES):
        for dil in A_DILATIONS:
            in_specs.append(pl.BlockSpec((None, dil, tile // dil, width), lambda b, i: (b, 0, i, 0)))
    return pl.pallas_call(
        functools.partial(_attn_merge_prompt_kernel, tile=tile),
        grid=(batch, tps),
        in_specs=in_specs,
        out_specs=pl.BlockSpec((tile, cols), lambda b, i: (b * tps + i, 0)),
        out_shape=jax.ShapeDtypeStruct((batch * seq, cols), BF16),
        scratch_shapes=[pltpu.VMEM((A_GROUPS * A_HEADS, tile, A_HEAD_DIM), F32),
                        pltpu.VMEM((A_GROUPS, tile, LANES), F32)],
        compiler_params=_params(2),
        name="attn_merge_prompt",
    )(*outs, *lses)


def _conv_kernel(u_ref, prev_ref, w_ref, b_ref, o_ref, full_ref, *, tt):
    t = pl.program_id(2)

    @pl.when(t == 0)
    def _():
        full_ref[0:SUBLANES, :] = prev_ref[...]

    u = u_ref[...]
    full_ref[SUBLANES:SUBLANES + tt, :] = u
    acc = u * w_ref[CONV_TAPS - 1:CONV_TAPS, :] + b_ref[...]
    for s in range(1, CONV_TAPS):
        acc = acc + full_ref[SUBLANES - s:SUBLANES - s + tt, :] * w_ref[CONV_TAPS - 1 - s:CONV_TAPS - s, :]
    o_ref[...] = _silu(acc)
    full_ref[0:SUBLANES, :] = u[tt - SUBLANES:tt, :]


def conv_silu(u, prev, w, bias, *, nseq, seq, col0, cols, tt, tc=512):
    nt = seq // tt
    c0 = col0 // tc
    return pl.pallas_call(
        functools.partial(_conv_kernel, tt=tt),
        grid=(nseq, cols // tc, nt),
        in_specs=[pl.BlockSpec((tt, tc), lambda s, c, t: (s * nt + t, c0 + c)),
                  pl.BlockSpec((None, SUBLANES, tc), lambda s, c, t: (s, 0, c)),
                  pl.BlockSpec((SUBLANES, tc), lambda s, c, t: (0, c)),
                  pl.BlockSpec((1, tc), lambda s, c, t: (0, c))],
        out_specs=pl.BlockSpec((tt, tc), lambda s, c, t: (s * nt + t, c)),
        out_shape=jax.ShapeDtypeStruct((nseq * seq, cols), F32),
        scratch_shapes=[pltpu.VMEM((SUBLANES + tt, tc), F32)],
        compiler_params=_params(3),
        name="conv_silu",
    )(u, prev, w, bias)


def _unit_lower_inverses(mats, n, top):
    blk = min(B_INV_BLOCK, top)
    ii = lax.broadcasted_iota(jnp.int32, (n, n), 0)
    jj = lax.broadcasted_iota(jnp.int32, (n, n), 1)
    eye = jnp.where(ii == jj, 1.0, 0.0).astype(F32)
    shift = blk.bit_length() - 1
    same = (ii >> shift) == (jj >> shift)
    ps = [jnp.where(same, -a, 0.0) for a in mats]
    xs = [eye + p for p in ps]
    for _ in range(shift - 1):
        ps = [_mm(p, p, B_INV_PASSES) for p in ps]
        xs = [x + _mm(x, p, B_INV_PASSES) for x, p in zip(xs, ps)]
    size = blk
    while size < top:
        shift += 1
        same_next = (ii >> shift) == (jj >> shift)
        sel = jnp.logical_and(same_next, jnp.logical_not(same))
        ys = [_mm(x, jnp.where(sel, a, 0.0), B_INV_PASSES) for x, a in zip(xs, mats)]
        xs = [x - _mm(y, x, B_INV_PASSES) for x, y in zip(xs, ys)]
        same = same_next
        size *= 2
    return xs


def _gdn_kernel(q_ref, k_ref, v_ref, z_ref, ba_ref, alog_ref, dtb_ref, nw_ref, s0_ref,
                o_ref, s_ref, *, c, hps, t_valid):
    hb = pl.program_id(1)
    ci = pl.program_id(2)
    dk = B_HEAD_DIM
    c2 = 2 * c
    assert c2 == LANES

    @pl.when(ci == 0)
    def _():
        s_ref[...] = s0_ref[...]

    ba = ba_ref[...]
    valid = lax.broadcasted_iota(jnp.int32, (c, LANES), 0) + ci * c < t_valid
    beta_all = jnp.where(valid, jax.nn.sigmoid(ba), 0.0)
    g_all = jnp.where(valid, -jnp.exp(alog_ref[pl.ds(hb, 1), :]) * _softplus(ba + dtb_ref[pl.ds(hb, 1), :]), 0.0)
    ci_, cj_ = lax.broadcasted_iota(jnp.int32, (c, c), 0), lax.broadcasted_iota(jnp.int32, (c, c), 1)
    gc_all = _select_rows(jnp.where(ci_ >= cj_, 1.0, 0.0).astype(BF16), g_all)
    gc_t = jnp.concatenate([gc_all, gc_all], axis=0).T
    ii = lax.broadcasted_iota(jnp.int32, (c2, c2), 0)
    jj = lax.broadcasted_iota(jnp.int32, (c2, c2), 1)
    shift = c.bit_length() - 1
    same = (ii >> shift) == (jj >> shift)
    incl = jnp.logical_and(same, ii >= jj)
    strict = jnp.logical_and(same, ii > jj)
    first = lax.broadcasted_iota(jnp.int32, (1, c2), 1) < c
    top = lax.broadcasted_iota(jnp.int32, (c2, 1), 0) < c
    nw = nw_ref[...]

    def stack_cols(arr, l0, l1):
        return jnp.concatenate([arr[:, l0:l0 + 1], arr[:, l1:l1 + 1]], axis=0)

    def own_half(r):
        return jnp.where(top, r[:, :dk], r[:, dk:])

    pairs = range(hps // 2)
    cols = lambda h: slice(h * dk, (h + 1) * dk)

    def prepare(qh):
        h0, h1 = 2 * qh, 2 * qh + 1
        q = q_ref[:, cols(qh)]
        k = k_ref[:, cols(qh)]
        q = q * lax.rsqrt(jnp.sum(q * q, axis=-1, keepdims=True) + 1e-6) * (dk ** -0.5)
        k = k * lax.rsqrt(jnp.sum(k * k, axis=-1, keepdims=True) + 1e-6)
        q2 = jnp.concatenate([q, q], axis=0)
        k2 = jnp.concatenate([k, k], axis=0)
        k2_b = k2.astype(BF16)
        beta = stack_cols(beta_all, h0, h1)
        gc = stack_cols(gc_all, hps + h0, hps + h1)
        gr = jnp.where(first, gc_t[hps + h0:hps + h0 + 1, :], gc_t[hps + h1:hps + h1 + 1, :])
        gl0 = gc_all[c - 1:c, hps + h0:hps + h0 + 1]
        gl1 = gc_all[c - 1:c, hps + h1:hps + h1 + 1]
        decay = jnp.exp(jnp.where(incl, gc - gr, NEG_BIG))
        eg = jnp.exp(gc)
        v2 = jnp.concatenate([v_ref[:, cols(h0)], v_ref[:, cols(h1)]], axis=0)
        return dict(
            a=jnp.where(strict, _dot_nt(k2_b, k2_b) * decay, 0.0) * beta,
            qk=jnp.where(incl, _dot_nt(q2.astype(BF16), k2_b) * decay, 0.0).astype(BF16),
            rhs=jnp.concatenate([v2 * beta, k2 * (beta * eg)], axis=1),
            q_eg=(q2 * eg).astype(BF16),
            k_dec=(k2 * jnp.exp(jnp.where(top, gl0, gl1) - gc)).astype(BF16),
            dec_cat=jnp.concatenate([jnp.broadcast_to(jnp.exp(gl0), (1, dk)),
                                     jnp.broadcast_to(jnp.exp(gl1), (1, dk))], axis=1))

    st = [prepare(qh) for qh in pairs]
    t_inv = _unit_lower_inverses([s["a"] for s in st], c2, c)
    uw = [_mm(t, s["rhs"], B_INV_PASSES) for t, s in zip(t_inv, st)]
    s_cat = [jnp.concatenate([s_ref[2 * qh], s_ref[2 * qh + 1]], axis=1) for qh in pairs]
    s_cat_b = [s.astype(BF16) for s in s_cat]
    v_new = [r[:, :dk] - own_half(_dot(r[:, dk:].astype(BF16), sb)) for r, sb in zip(uw, s_cat_b)]
    v_new_b = [v.astype(BF16) for v in v_new]
    outs = [own_half(_dot(s["q_eg"], sb)) + _dot(s["qk"], vb) for s, sb, vb in zip(st, s_cat_b, v_new_b)]
    for qh in pairs:
        h0, h1 = 2 * qh, 2 * qh + 1
        v = v_new[qh]
        v_blk = jnp.concatenate([jnp.where(top, v, 0.0), jnp.where(top, 0.0, v)], axis=1).astype(BF16)
        s_new = s_cat[qh] * st[qh]["dec_cat"] + _dot_tn(st[qh]["k_dec"], v_blk)
        s_ref[h0] = s_new[:, :dk]
        s_ref[h1] = s_new[:, dk:]
        o = outs[qh]
        o = o * lax.rsqrt(jnp.mean(o * o, axis=-1, keepdims=True) + RMS_EPS) * nw
        o = (o * _silu(jnp.concatenate([z_ref[:, cols(h0)], z_ref[:, cols(h1)]], axis=0))).astype(o_ref.dtype)
        o_ref[:, cols(h0)] = o[:c]
        o_ref[:, cols(h1)] = o[c:]


def gdn_scan(qkv_c, z_arr, z_col0, ba, alog_pad, dtb_pad, norm_w, s0, *, nseq, seq, t_valid):
    c = B_CHUNK
    hps = B_HEADS_PER_STEP
    nc = seq // c
    ng = B_V_HEADS // hps
    qw = hps // 2 * B_HEAD_DIM
    vw = hps * B_HEAD_DIM
    par_rows = alog_pad.shape[0]
    row = lambda s, h, ci: s * nc + ci
    return pl.pallas_call(
        functools.partial(_gdn_kernel, c=c, hps=hps, t_valid=t_valid),
        grid=(nseq, ng, nc),
        in_specs=[pl.BlockSpec((c, qw), lambda s, h, ci: (row(s, h, ci), h)),
                  pl.BlockSpec((c, qw), lambda s, h, ci: (row(s, h, ci), B_KEY_DIM // qw + h)),
                  pl.BlockSpec((c, vw), lambda s, h, ci: (row(s, h, ci), 2 * B_KEY_DIM // vw + h)),
                  pl.BlockSpec((c, vw), lambda s, h, ci: (row(s, h, ci), z_col0 // vw + h)),
                  pl.BlockSpec((c, LANES), lambda s, h, ci: (row(s, h, ci), h)),
                  pl.BlockSpec((par_rows, LANES), lambda s, h, ci: (0, 0)),
                  pl.BlockSpec((par_rows, LANES), lambda s, h, ci: (0, 0)),
                  pl.BlockSpec((1, B_HEAD_DIM), lambda s, h, ci: (0, 0)),
                  pl.BlockSpec((None, hps, B_HEAD_DIM, B_HEAD_DIM), lambda s, h, ci: (s, h, 0, 0))],
        out_specs=[pl.BlockSpec((c, vw), lambda s, h, ci: (row(s, h, ci), h)),
                   pl.BlockSpec((None, hps, B_HEAD_DIM, B_HEAD_DIM), lambda s, h, ci: (s, h, 0, 0))],
        out_shape=[jax.ShapeDtypeStruct((nseq * seq, B_VAL_DIM), BF16),
                   jax.ShapeDtypeStruct((nseq, B_V_HEADS, B_HEAD_DIM, B_HEAD_DIM), F32)],
        compiler_params=_params(3),
        name="gdn_scan",
    )(qkv_c, qkv_c, qkv_c, z_arr, ba, alog_pad, dtb_pad, norm_w, s0)


def _ssd_kernel(x_ref, b_ref, c_ref, z_ref, dt_ref, dtb_ref, alog_ref, dskip_ref, nw_ref, h0_ref,
                y_ref, hout_ref, ht_ref, *, c, nc, t_valid):
    g = pl.program_id(1)
    ci = pl.program_id(2)
    hpg = C_HEADS_PER_GROUP
    p = C_HEAD_DIM
    gcols = hpg * p

    @pl.when(ci == 0)
    def _():
        ht_ref[...] = h0_ref[...].reshape(gcols, C_D_STATE).T

    row = lax.broadcasted_iota(jnp.int32, (c, LANES), 0) + ci * c
    dt = jnp.where(row < t_valid, _softplus(dt_ref[...] + dtb_ref[pl.ds(g, 1), :]), 0.0)
    la = dt * -jnp.exp(alog_ref[pl.ds(g, 1), :])
    ii = lax.broadcasted_iota(jnp.int32, (c, c), 0)
    jj = lax.broadcasted_iota(jnp.int32, (c, c), 1)
    incl = ii >= jj
    acs = _select_rows(jnp.where(incl, 1.0, 0.0).astype(BF16), la)
    acs_t = acs.T
    el = lax.broadcasted_iota(jnp.int32, (LANES, gcols), 0)
    ej = lax.broadcasted_iota(jnp.int32, (LANES, gcols), 1)
    expand = jnp.where(el == (ej >> (p.bit_length() - 1)), 1.0, 0.0).astype(BF16)
    dt_e = _select_cols(dt, expand)
    acs_e = _select_cols(acs, expand)
    dskip_e = _select_cols(jnp.broadcast_to(dskip_ref[pl.ds(g, 1), :], (SUBLANES, LANES)), expand)[0:1, :]

    x = x_ref[...]
    bm = b_ref[...].astype(BF16)
    cm = c_ref[...].astype(BF16)
    xdt = (x * dt_e).astype(BF16)
    cb = _dot_nt(cm, bm)
    ht = ht_ref[...]
    y = _dot(cm, ht.astype(BF16)) * jnp.exp(acs_e) + dskip_e * x
    lane = lax.broadcasted_iota(jnp.int32, (c, 2 * p), 1)
    pairs = []
    for j in range(hpg // 2):
        xs = xdt[:, j * 2 * p:(j + 1) * 2 * p]
        ys = []
        for e in range(2):
            hd = 2 * j + e
            decay = jnp.exp(jnp.where(incl, acs[:, hd:hd + 1] - acs_t[hd:hd + 1, :], NEG_BIG))
            ys.append(_dot((cb * decay).astype(BF16), xs))
        pairs.append(jnp.where(lane < p, ys[0], ys[1]))
    y = y + jnp.concatenate(pairs, axis=1)
    y = y * _silu(z_ref[...])
    y = y * lax.rsqrt(jnp.mean(y * y, axis=-1, keepdims=True) + RMS_EPS) * nw_ref[...]
    y_ref[...] = y.astype(y_ref.dtype)

    last_e = acs_e[c - 1:c, :]
    xw = (x * (jnp.exp(last_e - acs_e) * dt_e)).astype(BF16)
    ht_new = ht * jnp.exp(last_e) + _dot_tn(bm, xw)
    ht_ref[...] = ht_new

    @pl.when(ci == nc - 1)
    def _():
        hout_ref[...] = ht_new.T.reshape(hpg, p, C_D_STATE)


def ssd_scan(xc, z_arr, dt, dtb_pad, alog_pad, dskip_pad, norm_w, h0, *, nseq, seq, t_valid):
    c = C_CHUNK
    nc = seq // c
    gc = C_GROUP_COLS
    b0 = C_D_INNER // C_D_STATE
    row = lambda s, g, ci: s * nc + ci
    par = pl.BlockSpec((C_GROUPS, LANES), lambda s, g, ci: (0, 0))
    return pl.pallas_call(
        functools.partial(_ssd_kernel, c=c, nc=nc, t_valid=t_valid),
        grid=(nseq, C_GROUPS, nc),
        in_specs=[pl.BlockSpec((c, gc), lambda s, g, ci: (row(s, g, ci), g)),
                  pl.BlockSpec((c, C_D_STATE), lambda s, g, ci: (row(s, g, ci), b0 + g)),
                  pl.BlockSpec((c, C_D_STATE), lambda s, g, ci: (row(s, g, ci), b0 + C_GROUPS + g)),
                  pl.BlockSpec((c, gc), lambda s, g, ci: (row(s, g, ci), g)),
                  pl.BlockSpec((c, LANES), lambda s, g, ci: (row(s, g, ci), g)),
                  par, par, par,
                  pl.BlockSpec((1, gc), lambda s, g, ci: (0, g)),
                  pl.BlockSpec((None, C_HEADS_PER_GROUP, C_HEAD_DIM, C_D_STATE), lambda s, g, ci: (s, g, 0, 0))],
        out_specs=[pl.BlockSpec((c, gc), lambda s, g, ci: (row(s, g, ci), g)),
                   pl.BlockSpec((None, C_HEADS_PER_GROUP, C_HEAD_DIM, C_D_STATE), lambda s, g, ci: (s, g, 0, 0))],
        out_shape=[jax.ShapeDtypeStruct((nseq * seq, C_D_INNER), BF16),
                   jax.ShapeDtypeStruct((nseq, C_HEADS, C_HEAD_DIM, C_D_STATE), F32)],
        scratch_shapes=[pltpu.VMEM((C_D_STATE, gc), F32)],
        compiler_params=_params(3),
        name="ssd_scan",
    )(xc, xc, xc, z_arr, dt, dtb_pad, alog_pad, dskip_pad, norm_w, h0)


def _rope_tables(mp, seq, n_sample, ts):
    pos = jnp.concatenate([jnp.arange(mp) % seq, PAST_LEN + jnp.arange(n_sample) % ts]).astype(F32)
    inv_freq = ROPE_THETA ** (-jnp.arange(A_ROT_HALF, dtype=F32) / A_ROT_HALF)
    ang = pos[:, None] * inv_freq[None, :]
    cos, sin = jnp.cos(ang), jnp.sin(ang)
    rest = A_HEAD_DIM - 2 * A_ROT_HALF
    rows = pos.shape[0]
    cos_full = jnp.concatenate([cos, cos, jnp.ones((rows, rest), F32)], axis=1)
    sin_full = jnp.concatenate([-sin, sin, jnp.zeros((rows, rest), F32)], axis=1)
    return cos_full, sin_full


def _lane_groups(cols, group):
    lead = cols.shape[:-1]
    n = cols.shape[-1]
    c = cols.reshape(*lead, n // group, group)
    c = jnp.pad(c, [(0, 0)] * (c.ndim - 1) + [(0, LANES - group)])
    return c.reshape(*lead, n // group * LANES)


def _pad_seq_rows(x, nseq, ts, seq):
    cols = x.shape[-1]
    x = jnp.pad(x.reshape(nseq, ts, cols), ((0, 0), (0, seq - ts), (0, 0)))
    return x.reshape(nseq * seq, cols)


def _conv_prev(state):
    return jnp.pad(state, ((0, 0), (SUBLANES - state.shape[1], 0), (0, 0)))


def _conv_taps(w):
    return jnp.pad(w, ((0, SUBLANES - w.shape[0]), (0, 0)))


def _last_rows(x, nseq, seq, n, col0, cols):
    return jnp.stack([x[(s + 1) * seq - n:(s + 1) * seq, col0:col0 + cols] for s in range(nseq)])


def kernel(x_prompt, x_sample, cache_a_kv_w128, cache_a_kv_w512, cache_a_kv_w2048, state_b_ssm, state_b_conv, state_c_ssm, state_c_conv, ln_g, ln_b, ffn1_w_gu, ffn1_w_down, ffn2_w_gu, ffn2_w_down, a_w_qkv, a_w_o, b_w_in, b_conv_w, b_a_log, b_dt_bias, b_norm_w, b_w_out, c_w_in, c_conv_w, c_conv_b, c_dt_bias, c_a_log, c_d, c_norm_w, c_w_out):
    a_bufs = (cache_a_kv_w128, cache_a_kv_w512, cache_a_kv_w2048)
    bp, seq, d = x_prompt.shape
    db, ts, _ = x_sample.shape
    mp, ms = bp * seq, db * ts
    assert seq % (A_BLOCK * max(A_DILATIONS)) == 0 and mp % ms == 0
    x = jnp.concatenate([x_prompt.reshape(mp, d), x_sample.reshape(ms, d)], axis=0)
    xb = x.astype(BF16)
    dff = ffn1_w_down.shape[1]
    cos_full, sin_full = _rope_tables(mp, seq, ms, ts)

    def ffn(x, xb, w_gu, w_down, g, b):
        h = swiglu_up(xb, w_gu.astype(BF16))
        return matmul_postnorm(h, w_down.astype(BF16), x, g, b, scale=0.5, tk=dff // 4)

    def with_sample_rows(a_p, a_s, rows):
        return jnp.concatenate([a_p, a_s.reshape(db, rows, -1)[:, :ts].reshape(ms, -1)], axis=0)

    outs = {}

    def mixer_a(xb, j):
        qkv = matmul(xb, a_w_qkv[j].astype(BF16), tn=1024)
        gw = A_GROUP_COLS
        o_p, l_p, o_s, l_s = [], [], [], []
        for g, dil in enumerate(A_DILATIONS):
            qkv_r = qkv_split(qkv, cos_full, sin_full, g=g, dil=dil, row0=0, batch=bp, seq=seq, tile=A_SPLIT_TILE)
            o, lse = attn_prompt(qkv_r)
            o_p.append(o)
            l_p.append(lse)
            new = qkv_split(qkv, cos_full, sin_full, g=g, dil=1, row0=mp, batch=1, seq=ms, tile=ms)
            q_s, k_s, v_s = (new[0, 0, :, sec * gw:(sec + 1) * gw].reshape(db, ts, gw) for sec in range(A_SECTIONS))
            o, lse = attn_sample(q_s, k_s, v_s, a_bufs[g], layer=j, dil=dil)
            o_s.append(o.reshape(ms, gw))
            l_s.append(lse.reshape(ms, LANES))
            length = seq // dil
            tail = qkv_r[:, :, length - A_BLOCK:, gw:].transpose(0, 2, 1, 3)
            outs.setdefault("a_p%d" % g, []).append(tail.reshape(bp, A_BLOCK * dil, 2, A_HEADS, A_HEAD_DIM))
            outs.setdefault("a_new%d" % g, []).append(
                jnp.stack([k_s, v_s], axis=2).reshape(db, ts, 2, A_HEADS, A_HEAD_DIM))
        a_p = attn_merge_prompt(o_p, l_p)
        a_s = attn_merge_rows(o_s, l_s)
        return jnp.concatenate([a_p, a_s], axis=0), a_w_o[j], a_w_o.shape[1]

    def mixer_b(xb, j):
        hps = B_HEADS_PER_STEP
        ng = B_V_HEADS // hps
        w_in = b_w_in[j]
        main = B_CONV_DIM + B_VAL_DIM
        pb = matmul(xb, w_in[:, :main].astype(BF16), tn=1024)
        w_tail = jnp.concatenate([w_in[:, main:main + B_V_HEADS].reshape(d, ng, hps),
                                  w_in[:, main + B_V_HEADS:].reshape(d, ng, hps)], axis=2)
        ba = matmul(xb, _lane_groups(w_tail.reshape(d, -1), 2 * hps).astype(BF16), tn=ng * LANES)
        gate_lanes = lambda v: jnp.pad(_lane_groups(
            jnp.concatenate([jnp.zeros((ng, hps), F32), v.reshape(ng, hps)], axis=1).reshape(-1), 2 * hps
        ).reshape(ng, LANES), ((0, SUBLANES - ng), (0, 0)))
        alog_pad, dtb_pad = gate_lanes(b_a_log[j]), gate_lanes(b_dt_bias[j])
        taps = _conv_taps(b_conv_w[j])
        no_bias = jnp.zeros((1, B_CONV_DIM), F32)
        nw = b_norm_w[j].reshape(1, B_HEAD_DIM)
        qkv_c = conv_silu(pb, jnp.zeros((bp, SUBLANES, B_CONV_DIM), F32), taps, no_bias,
                          nseq=bp, seq=seq, col0=0, cols=B_CONV_DIM, tt=512)
        a_p, s_p = gdn_scan(qkv_c, pb, B_CONV_DIM, ba, alog_pad, dtb_pad, nw,
                            jnp.zeros((bp, B_V_HEADS, B_HEAD_DIM, B_HEAD_DIM), F32),
                            nseq=bp, seq=seq, t_valid=seq)
        pb_s = _pad_seq_rows(pb[mp:], db, ts, B_CHUNK)
        ba_s = _pad_seq_rows(ba[mp:], db, ts, B_CHUNK)
        qkv_cs = conv_silu(pb_s, _conv_prev(state_b_conv[j]), taps, no_bias,
                           nseq=db, seq=B_CHUNK, col0=0, cols=B_CONV_DIM, tt=B_CHUNK)
        a_s, s_s = gdn_scan(qkv_cs, pb_s, B_CONV_DIM, ba_s, alog_pad, dtb_pad, nw,
                            state_b_ssm[j], nseq=db, seq=B_CHUNK, t_valid=ts)
        outs.setdefault("b_ssm_p", []).append(s_p)
        outs.setdefault("b_ssm_s", []).append(s_s)
        outs.setdefault("b_conv_p", []).append(_last_rows(pb, bp, seq, CONV_TAPS - 1, 0, B_CONV_DIM))
        outs.setdefault("b_conv_s", []).append(_last_rows(pb[mp:], db, ts, CONV_TAPS - 1, 0, B_CONV_DIM))
        return with_sample_rows(a_p, a_s, B_CHUNK), b_w_out[j], 1024

    def mixer_c(xb, j):
        w_in = c_w_in[j]
        main = C_D_INNER + C_CONV_DIM
        pc = matmul(xb, w_in[:, :main].astype(BF16), tn=1024)
        dt = matmul(xb, _lane_groups(w_in[:, main:], C_HEADS_PER_GROUP).astype(BF16), tn=C_GROUPS * LANES)
        head_lanes = lambda v: _lane_groups(v, C_HEADS_PER_GROUP).reshape(C_GROUPS, LANES)
        dtb_pad, alog_pad, dskip_pad = head_lanes(c_dt_bias[j]), head_lanes(c_a_log[j]), head_lanes(c_d[j])
        taps = _conv_taps(c_conv_w[j])
        bias = c_conv_b[j].reshape(1, C_CONV_DIM)
        nw = c_norm_w[j].reshape(1, C_D_INNER)
        xc = conv_silu(pc, jnp.zeros((bp, SUBLANES, C_CONV_DIM), F32), taps, bias,
                       nseq=bp, seq=seq, col0=C_D_INNER, cols=C_CONV_DIM, tt=512)
        a_p, h_p = ssd_scan(xc, pc, dt, dtb_pad, alog_pad, dskip_pad, nw,
                            jnp.zeros((bp, C_HEADS, C_HEAD_DIM, C_D_STATE), F32),
                            nseq=bp, seq=seq, t_valid=seq)
        pc_s = _pad_seq_rows(pc[mp:], db, ts, C_CHUNK)
        dt_s = _pad_seq_rows(dt[mp:], db, ts, C_CHUNK)
        xc_s = conv_silu(pc_s, _conv_prev(state_c_conv[j]), taps, bias,
                         nseq=db, seq=C_CHUNK, col0=C_D_INNER, cols=C_CONV_DIM, tt=C_CHUNK)
        a_s, h_s = ssd_scan(xc_s, pc_s, dt_s, dtb_pad, alog_pad, dskip_pad, nw, state_c_ssm[j],
                            nseq=db, seq=C_CHUNK, t_valid=ts)
        outs.setdefault("c_ssm_p", []).append(h_p)
        outs.setdefault("c_ssm_s", []).append(h_s)
        outs.setdefault("c_conv_p", []).append(_last_rows(pc, bp, seq, CONV_TAPS - 1, C_D_INNER, C_CONV_DIM))
        outs.setdefault("c_conv_s", []).append(_last_rows(pc[mp:], db, ts, CONV_TAPS - 1, C_D_INNER, C_CONV_DIM))
        return with_sample_rows(a_p, a_s, C_CHUNK), c_w_out[j], 1024

    mixers = (mixer_a, mixer_b, mixer_c)
    for i in range(DEPTH):
        x, xb = ffn(x, xb, ffn1_w_gu[i], ffn1_w_down[i], ln_g[i, 0], ln_b[i, 0])
        a, w_out, tk = mixers[i % 3](xb, i // 3)
        x, xb = matmul_postnorm(a, w_out.astype(BF16), x, ln_g[i, 1], ln_b[i, 1], scale=1.0, tk=tk)
        x, xb = ffn(x, xb, ffn2_w_gu[i], ffn2_w_down[i], ln_g[i, 2], ln_b[i, 2])

    st = lambda name: jnp.stack(outs[name])
    a_s = [jnp.concatenate([a_bufs[g][:, :, ts:], st("a_new%d" % g)], axis=2) for g in range(A_GROUPS)]
    return (x[:mp].reshape(bp, seq, d), x[mp:].reshape(db, ts, d),
            st("a_p0"), a_s[0], st("a_p1"), a_s[1], st("a_p2"), a_s[2],
            st("b_ssm_p"), st("b_ssm_s"), st("b_conv_p"), st("b_conv_s"),
            st("c_ssm_p"), st("c_ssm_s"), st("c_conv_p"), st("c_conv_s"))
```

```python
import functools

import jax
import jax.numpy as jnp
from jax import lax
from jax.experimental import pallas as pl
from jax.experimental.pallas import tpu as pltpu

F32 = jnp.float32
BF16 = jnp.bfloat16

VMEM_LIMIT_BYTES = 56 * 1024 * 1024
LANES = 128
SUBLANES = 8
ROW_TILE = 512
NEG_BIG = -1e30

DEPTH = 4
DN_ALPHA = (2.0 * DEPTH) ** 0.25
LN_EPS = 1e-5
RMS_EPS = 1e-6
PAST_LEN = 16384

A_DILATIONS = (1, 4, 16)
A_GROUPS = 3
A_HEADS = 8
A_HEAD_DIM = 128
A_ROT_HALF = A_HEAD_DIM // 8
A_BLOCK = 128
A_GROUP_COLS = A_HEADS * A_HEAD_DIM
A_SECTIONS = 3
A_SPLIT_TILE = 2048
ROPE_THETA = 500000.0

B_QK_HEADS = 16
B_V_HEADS = 32
B_HEAD_DIM = 128
B_KEY_DIM = B_QK_HEADS * B_HEAD_DIM
B_VAL_DIM = B_V_HEADS * B_HEAD_DIM
B_CONV_DIM = 2 * B_KEY_DIM + B_VAL_DIM
B_CHUNK = 64
B_HEADS_PER_STEP = 8
B_INV_BLOCK = 16
B_INV_PASSES = 1

C_D_INNER = 4096
C_HEADS = 64
C_HEAD_DIM = 64
C_GROUPS = 8
C_HEADS_PER_GROUP = C_HEADS // C_GROUPS
C_GROUP_COLS = C_D_INNER // C_GROUPS
C_D_STATE = 128
C_CONV_DIM = C_D_INNER + 2 * C_GROUPS * C_D_STATE
C_CHUNK = 128

CONV_TAPS = 4


def _params(n_axes):
    return pltpu.CompilerParams(
        dimension_semantics=("arbitrary",) * n_axes,
        vmem_limit_bytes=VMEM_LIMIT_BYTES,
    )


def _dot(a, b):
    return jnp.dot(a, b, preferred_element_type=F32)


def _dot_nt(a, b):
    return lax.dot_general(a, b, (((1,), (1,)), ((), ())), preferred_element_type=F32)


def _dot_tn(a, b):
    return lax.dot_general(a, b, (((0,), (0,)), ((), ())), preferred_element_type=F32)


def _split3(x):
    hi = x.astype(BF16)
    r = x - hi.astype(F32)
    mid = r.astype(BF16)
    lo = (r - mid.astype(F32)).astype(BF16)
    return hi, mid, lo


def _select_rows(sel, x):
    return sum(_dot(sel, part) for part in _split3(x))


def _select_cols(x, sel):
    return sum(_dot(part, sel) for part in _split3(x))


def _mm(a, b, passes):
    a_hi, b_hi = a.astype(BF16), b.astype(BF16)
    out = _dot(a_hi, b_hi)
    if passes == 3:
        a_lo = (a - a_hi.astype(F32)).astype(BF16)
        b_lo = (b - b_hi.astype(F32)).astype(BF16)
        out = out + _dot(a_hi, b_lo) + _dot(a_lo, b_hi)
    return out


def _silu(x):
    return x * jax.nn.sigmoid(x)


def _softplus(x):
    return jnp.maximum(x, 0.0) + jnp.log(1.0 + jnp.exp(-jnp.abs(x)))


def _row_tiles(rows, tm):
    n_full, rem = divmod(rows, tm)
    return n_full + (1 if rem else 0), n_full, rem


def _per_row_tile(i, n_full, rem, tm, body):
    if rem == 0:
        body(tm)
        return

    @pl.when(i < n_full)
    def _():
        body(tm)

    @pl.when(i == n_full)
    def _():
        body(rem)


def _mm_kernel(x_ref, w_ref, o_ref, *, tm, n_full, rem):
    def body(r):
        o_ref[:r, :] = _dot(x_ref[:r, :], w_ref[...]).astype(o_ref.dtype)

    _per_row_tile(pl.program_id(0), n_full, rem, tm, body)


def matmul(x, w, *, layer, n, tn, out_dtype=F32, tm=2 * ROW_TILE):
    rows, k = x.shape
    steps, n_full, rem = _row_tiles(rows, tm)
    return pl.pallas_call(
        functools.partial(_mm_kernel, tm=tm, n_full=n_full, rem=rem),
        grid=(steps, n // tn),
        in_specs=[pl.BlockSpec((tm, k), lambda i, j: (i, 0)),
                  pl.BlockSpec((None, k, tn), lambda i, j: (layer, 0, j))],
        out_specs=pl.BlockSpec((tm, tn), lambda i, j: (i, j)),
        out_shape=jax.ShapeDtypeStruct((rows, n), out_dtype),
        compiler_params=_params(2),
        name="matmul",
    )(x, w)


def _accumulate_then_postnorm(k, nk, r, part, scale, x_ref, g_ref, b_ref, y_ref, yb_ref, acc_ref):
    @pl.when(k == 0)
    def _():
        acc_ref[:r, :] = part

    @pl.when(k > 0)
    def _():
        acc_ref[:r, :] += part

    @pl.when(k == nk - 1)
    def _():
        y = DN_ALPHA * x_ref[:r, :] + scale * acc_ref[:r, :]
        mu = jnp.mean(y, axis=-1, keepdims=True)
        yc = y - mu
        var = jnp.mean(yc * yc, axis=-1, keepdims=True)
        out = yc * lax.rsqrt(var + LN_EPS) * g_ref[...] + b_ref[...]
        y_ref[:r, :] = out
        yb_ref[:r, :] = out.astype(BF16)


def _ffn_kernel(xb_ref, wg_ref, wu_ref, wd_ref, x_ref, g_ref, b_ref, y_ref, yb_ref, acc_ref,
                *, nf, tm, n_full, rem):
    j = pl.program_id(1)

    def body(r):
        xb = xb_ref[:r, :]
        gate = _dot(xb, wg_ref[...])
        up = _dot(xb, wu_ref[...])
        part = _dot((_silu(gate) * up).astype(BF16), wd_ref[...])
        _accumulate_then_postnorm(j, nf, r, part, 0.5, x_ref, g_ref, b_ref, y_ref, yb_ref, acc_ref)

    _per_row_tile(pl.program_id(0), n_full, rem, tm, body)


def ffn_postnorm(x, xb, w_gu, w_down, g, b, *, layer, tf=512, tm=ROW_TILE):
    rows, d = x.shape
    f = w_down.shape[1]
    nf = f // tf
    steps, n_full, rem = _row_tiles(rows, tm)
    row_spec = lambda: pl.BlockSpec((tm, d), lambda i, j: (i, 0))
    vec_spec = pl.BlockSpec((1, d), lambda i, j: (0, 0))
    return pl.pallas_call(
        functools.partial(_ffn_kernel, nf=nf, tm=tm, n_full=n_full, rem=rem),
        grid=(steps, nf),
        in_specs=[row_spec(),
                  pl.BlockSpec((None, d, tf), lambda i, j: (layer, 0, j)),
                  pl.BlockSpec((None, d, tf), lambda i, j: (layer, 0, j + nf)),
                  pl.BlockSpec((None, tf, d), lambda i, j: (layer, j, 0)),
                  row_spec(), vec_spec, vec_spec],
        out_specs=[row_spec(), row_spec()],
        out_shape=[jax.ShapeDtypeStruct((rows, d), F32),
                   jax.ShapeDtypeStruct((rows, d), BF16)],
        scratch_shapes=[pltpu.VMEM((tm, d), F32)],
        compiler_params=_params(2),
        name="ffn_postnorm",
    )(xb, w_gu, w_gu, w_down, x, g.reshape(1, d), b.reshape(1, d))


def _postnorm_kernel(a_ref, w_ref, x_ref, g_ref, b_ref, y_ref, yb_ref, acc_ref,
                     *, nk, tm, n_full, rem):
    k = pl.program_id(1)

    def body(r):
        part = _dot(a_ref[:r, :], w_ref[...])
        _accumulate_then_postnorm(k, nk, r, part, 1.0, x_ref, g_ref, b_ref, y_ref, yb_ref, acc_ref)

    _per_row_tile(pl.program_id(0), n_full, rem, tm, body)


def matmul_postnorm(a, w, x, g, b, *, layer, tk, tm=ROW_TILE):
    rows, kdim = a.shape
    d = w.shape[2]
    nk = kdim // tk
    steps, n_full, rem = _row_tiles(rows, tm)
    return pl.pallas_call(
        functools.partial(_postnorm_kernel, nk=nk, tm=tm, n_full=n_full, rem=rem),
        grid=(steps, nk),
        in_specs=[pl.BlockSpec((tm, tk), lambda i, k: (i, k)),
                  pl.BlockSpec((None, tk, d), lambda i, k: (layer, k, 0)),
                  pl.BlockSpec((tm, d), lambda i, k: (i, 0)),
                  pl.BlockSpec((1, d), lambda i, k: (0, 0)),
                  pl.BlockSpec((1, d), lambda i, k: (0, 0))],
        out_specs=[pl.BlockSpec((tm, d), lambda i, k: (i, 0)),
                   pl.BlockSpec((tm, d), lambda i, k: (i, 0))],
        out_shape=[jax.ShapeDtypeStruct((rows, d), F32),
                   jax.ShapeDtypeStruct((rows, d), BF16)],
        scratch_shapes=[pltpu.VMEM((tm, d), F32)],
        compiler_params=_params(2),
        name="matmul_postnorm",
    )(a, w, x, g.reshape(1, d), b.reshape(1, d))


def _strided_rows(r, n, stride):
    return pl.ds(r, n, stride=stride) if stride > 1 else pl.ds(0, n)


def _qkv_split_kernel(x_ref, cos_ref, sin_ref, o_ref, *, dil, n):
    rotated = pl.program_id(2) < (A_SECTIONS - 1) * A_HEADS
    lane = lax.broadcasted_iota(jnp.int32, (n, A_HEAD_DIM), 1)

    def emit(rotate):
        for r in range(dil):
            rows = _strided_rows(r, n, dil)
            x = x_ref[rows, :]
            if rotate:
                partner = jnp.where(lane < A_ROT_HALF,
                                    pltpu.roll(x, A_HEAD_DIM - A_ROT_HALF, axis=1),
                                    pltpu.roll(x, A_ROT_HALF, axis=1))
                x = x * cos_ref[rows, :] + partner * sin_ref[rows, :]
            o_ref[r] = x

    @pl.when(rotated)
    def _():
        emit(True)

    @pl.when(jnp.logical_not(rotated))
    def _():
        emit(False)


def qkv_split(qkv, cos_full, sin_full, *, g, dil, row0, batch, seq, tile):
    n = tile // dil
    tps = seq // tile
    r0 = row0 // tile
    in_col = lambda c: (c // A_HEADS * A_GROUPS + g) * A_HEADS + c % A_HEADS
    return pl.pallas_call(
        functools.partial(_qkv_split_kernel, dil=dil, n=n),
        grid=(batch, tps, A_SECTIONS * A_HEADS),
        in_specs=[pl.BlockSpec((tile, A_HEAD_DIM), lambda b, i, c: (r0 + b * tps + i, in_col(c))),
                  pl.BlockSpec((tile, A_HEAD_DIM), lambda b, i, c: (r0 + b * tps + i, 0)),
                  pl.BlockSpec((tile, A_HEAD_DIM), lambda b, i, c: (r0 + b * tps + i, 0))],
        out_specs=pl.BlockSpec((None, dil, n, A_HEAD_DIM), lambda b, i, c: (b, 0, i, c)),
        out_shape=jax.ShapeDtypeStruct((batch, dil, seq // dil, A_SECTIONS * A_GROUP_COLS), F32),
        compiler_params=_params(3),
        name="qkv_split",
    )(qkv, cos_full, sin_full)


def _attn_prompt_kernel(q_ref, kp_ref, kc_ref, vp_ref, vc_ref, o_ref, lse_ref):
    lb = pl.program_id(2)
    n = A_BLOCK
    qi = lax.broadcasted_iota(jnp.int32, (n, n), 0)
    kj = lax.broadcasted_iota(jnp.int32, (n, n), 1)
    mask_cur = kj <= qi
    mask_prev = jnp.logical_and(kj >= qi, lb > 0)
    scale = A_HEAD_DIM ** -0.5
    lse_all = jnp.zeros((n, LANES), F32)
    for h in range(A_HEADS):
        sl = slice(h * A_HEAD_DIM, (h + 1) * A_HEAD_DIM)
        q = q_ref[:, sl].astype(BF16)
        s_c = jnp.where(mask_cur, _dot_nt(q, kc_ref[:, sl].astype(BF16)) * scale, NEG_BIG)
        s_p = jnp.where(mask_prev, _dot_nt(q, kp_ref[:, sl].astype(BF16)) * scale, NEG_BIG)
        m = jnp.maximum(jnp.max(s_c, axis=-1, keepdims=True), jnp.max(s_p, axis=-1, keepdims=True))
        p_c = jnp.exp(s_c - m)
        p_p = jnp.exp(s_p - m)
        l = jnp.sum(p_c, axis=-1, keepdims=True) + jnp.sum(p_p, axis=-1, keepdims=True)
        o = _dot(p_c.astype(BF16), vc_ref[:, sl].astype(BF16)) + _dot(p_p.astype(BF16), vp_ref[:, sl].astype(BF16))
        o_ref[:, sl] = o / l
        lse_all = jnp.where(kj == h, m + jnp.log(l), lse_all)
    lse_ref[...] = lse_all


def attn_prompt(qkv_r):
    batch, dil, length, _ = qkv_r.shape
    nb = length // A_BLOCK
    spec = lambda sec, back: pl.BlockSpec(
        (None, None, A_BLOCK, A_GROUP_COLS), lambda b, r, lb: (b, r, jnp.maximum(lb - back, 0), sec))
    return pl.pallas_call(
        _attn_prompt_kernel,
        grid=(batch, dil, nb),
        in_specs=[spec(0, 0), spec(1, 1), spec(1, 0), spec(2, 1), spec(2, 0)],
        out_specs=[pl.BlockSpec((None, None, A_BLOCK, A_GROUP_COLS), lambda b, r, lb: (b, r, lb, 0)),
                   pl.BlockSpec((None, None, A_BLOCK, LANES), lambda b, r, lb: (b, r, lb, 0))],
        out_shape=[jax.ShapeDtypeStruct((batch, dil, length, A_GROUP_COLS), F32),
                   jax.ShapeDtypeStruct((batch, dil, length, LANES), F32)],
        compiler_params=_params(3),
        name="attn_prompt",
    )(qkv_r, qkv_r, qkv_r, qkv_r, qkv_r)


def _attn_sample_kernel(q_ref, kn_ref, vn_ref, *refs, dil, ts):
    n_res = len(refs) // 2 - 1
    cache_refs, (o_ref, lse_ref) = refs[:2 * n_res], refs[2 * n_res:]
    scale = A_HEAD_DIM ** -0.5
    jc = lax.broadcasted_iota(jnp.int32, (A_BLOCK, 1, 1), 0)
    un = lax.broadcasted_iota(jnp.int32, (ts, 1, 1), 0)
    kn = kn_ref[...]
    vn = vn_ref[...]
    for t in range(ts):
        kc_ref, vc_ref = cache_refs[2 * (t % n_res)], cache_refs[2 * (t % n_res) + 1]
        valid_n = (un <= t) if dil == 1 else (un == t)
        q = q_ref[t:t + 1]
        s_c = jnp.sum(kc_ref[...] * q, axis=-1, keepdims=True) * scale
        if dil == 1:
            s_c = jnp.where(jc >= t, s_c, NEG_BIG)
        s_n = jnp.where(valid_n, jnp.sum(kn * q, axis=-1, keepdims=True) * scale, NEG_BIG)
        m = jnp.maximum(jnp.max(s_c, axis=0, keepdims=True), jnp.max(s_n, axis=0, keepdims=True))
        p_c = jnp.exp(s_c - m)
        p_n = jnp.exp(s_n - m)
        l = jnp.sum(p_c, axis=0, keepdims=True) + jnp.sum(p_n, axis=0, keepdims=True)
        o = jnp.sum(p_c * vc_ref[...], axis=0, keepdims=True) + jnp.sum(p_n * vn, axis=0, keepdims=True)
        o_ref[t:t + 1] = o / l
        lse_ref[t:t + 1] = jnp.broadcast_to(m + jnp.log(l), (1, A_HEADS, A_HEAD_DIM))


def attn_sample(q, k_new, v_new, caches, *, layer, dil):
    db, ts = q.shape[:2]
    wb = caches.shape[2]
    assert wb == A_BLOCK * dil and (dil == 1 or ts <= dil)
    cache_v = caches.reshape(caches.shape[0], db, A_BLOCK, dil, 2, A_HEADS, A_HEAD_DIM)
    n_res = 1 if dil == 1 else ts
    tok_spec = pl.BlockSpec((None, ts, A_HEADS, A_HEAD_DIM), lambda b: (b, 0, 0, 0))
    cache_specs = [pl.BlockSpec((None, None, A_BLOCK, None, None, A_HEADS, A_HEAD_DIM),
                                lambda b, r=r, kv=kv: (layer, b, 0, r, kv, 0, 0))
                   for r in range(n_res) for kv in range(2)]
    out = jax.ShapeDtypeStruct((db, ts, A_HEADS, A_HEAD_DIM), F32)
    return pl.pallas_call(
        functools.partial(_attn_sample_kernel, dil=dil, ts=ts),
        grid=(db,),
        in_specs=[tok_spec, tok_spec, tok_spec] + cache_specs,
        out_specs=[tok_spec, tok_spec],
        out_shape=[out, out],
        compiler_params=_params(1),
        name="attn_sample",
    )(q, k_new, v_new, *([cache_v] * (2 * n_res)))


def _merge_groups(head_out, lses, a_ref):
    m = functools.reduce(jnp.maximum, lses)
    es = [jnp.exp(l - m) for l in lses]
    den = sum(es)
    ws = [e / den for e in es]
    for h in range(A_HEADS):
        a = sum(w[:, h:h + 1] * head_out(g, h) for g, w in enumerate(ws))
        a_ref[:, h * A_HEAD_DIM:(h + 1) * A_HEAD_DIM] = a.astype(a_ref.dtype)


def _attn_merge_rows_kernel(*refs):
    outs = [r[...] for r in refs[:A_GROUPS]]
    lses = [r[...] for r in refs[A_GROUPS:2 * A_GROUPS]]
    m = functools.reduce(jnp.maximum, lses)
    es = [jnp.exp(l - m) for l in lses]
    den = sum(es)
    refs[2 * A_GROUPS][...] = sum(e / den * o for e, o in zip(es, outs))


def attn_merge_rows(outs, lses):
    spec = pl.BlockSpec(outs[0].shape, lambda i: (0, 0, 0, 0))
    return pl.pallas_call(
        _attn_merge_rows_kernel,
        grid=(1,),
        in_specs=[spec] * (2 * A_GROUPS),
        out_specs=spec,
        out_shape=jax.ShapeDtypeStruct(outs[0].shape, F32),
        compiler_params=_params(1),
        name="attn_merge_rows",
    )(*outs, *lses)


def _attn_merge_prompt_kernel(*refs, tile):
    o_refs, l_refs = refs[:A_GROUPS], refs[A_GROUPS:2 * A_GROUPS]
    a_ref, o_nat, l_nat = refs[2 * A_GROUPS:]
    lses = []
    for g, dil in enumerate(A_DILATIONS):
        if dil == 1:
            lses.append(l_refs[g][0])
            continue
        n = tile // dil
        for r in range(dil):
            rows = pl.ds(r, n, stride=dil)
            l_nat[g, rows, :] = l_refs[g][r]
            for h in range(A_HEADS):
                o_nat[g * A_HEADS + h, rows, :] = o_refs[g][r, :, h * A_HEAD_DIM:(h + 1) * A_HEAD_DIM]
        lses.append(l_nat[g])

    def head_out(g, h):
        if A_DILATIONS[g] == 1:
            return o_refs[g][0, :, h * A_HEAD_DIM:(h + 1) * A_HEAD_DIM]
        return o_nat[g * A_HEADS + h]

    _merge_groups(head_out, lses, a_ref)


def attn_merge_prompt(outs, lses, *, spare_rows=0, tile=ROW_TILE):
    batch, _, seq, cols = outs[0].shape
    tps = seq // tile
    in_specs = []
    for width in (cols, LANES):
        for dil in A_DILATIONS:
            in_specs.append(pl.BlockSpec((None, dil, tile // dil, width), lambda b, i: (b, 0, i, 0)))
    return pl.pallas_call(
        functools.partial(_attn_merge_prompt_kernel, tile=tile),
        grid=(batch, tps),
        in_specs=in_specs,
        out_specs=pl.BlockSpec((tile, cols), lambda b, i: (b * tps + i, 0)),
        out_shape=jax.ShapeDtypeStruct((batch * seq + spare_rows, cols), BF16),
        scratch_shapes=[pltpu.VMEM((A_GROUPS * A_HEADS, tile, A_HEAD_DIM), F32),
                        pltpu.VMEM((A_GROUPS, tile, LANES), F32)],
        compiler_params=_params(2),
        name="attn_merge_prompt",
    )(*outs, *lses)


def _conv_kernel(u_ref, prev_ref, w_ref, b_ref, o_ref, full_ref, *, tt):
    t = pl.program_id(2)

    @pl.when(t == 0)
    def _():
        full_ref[0:SUBLANES, :] = prev_ref[...]

    u = u_ref[...]
    full_ref[SUBLANES:SUBLANES + tt, :] = u
    acc = u * w_ref[CONV_TAPS - 1:CONV_TAPS, :] + b_ref[...]
    for s in range(1, CONV_TAPS):
        acc = acc + full_ref[SUBLANES - s:SUBLANES - s + tt, :] * w_ref[CONV_TAPS - 1 - s:CONV_TAPS - s, :]
    o_ref[...] = _silu(acc)
    full_ref[0:SUBLANES, :] = u[tt - SUBLANES:tt, :]


def conv_silu(u, prev, w, bias, *, nseq, seq, col0, cols, tt, tc=512):
    nt = seq // tt
    c0 = col0 // tc
    return pl.pallas_call(
        functools.partial(_conv_kernel, tt=tt),
        grid=(nseq, cols // tc, nt),
        in_specs=[pl.BlockSpec((tt, tc), lambda s, c, t: (s * nt + t, c0 + c)),
                  pl.BlockSpec((None, SUBLANES, tc), lambda s, c, t: (s, 0, c)),
                  pl.BlockSpec((SUBLANES, tc), lambda s, c, t: (0, c)),
                  pl.BlockSpec((1, tc), lambda s, c, t: (0, c))],
        out_specs=pl.BlockSpec((tt, tc), lambda s, c, t: (s * nt + t, c)),
        out_shape=jax.ShapeDtypeStruct((nseq * seq, cols), F32),
        scratch_shapes=[pltpu.VMEM((SUBLANES + tt, tc), F32)],
        compiler_params=_params(3),
        name="conv_silu",
    )(u, prev, w, bias)


def _unit_lower_inverses(mats, n, top):
    blk = min(B_INV_BLOCK, top)
    ii = lax.broadcasted_iota(jnp.int32, (n, n), 0)
    jj = lax.broadcasted_iota(jnp.int32, (n, n), 1)
    eye = jnp.where(ii == jj, 1.0, 0.0).astype(F32)
    shift = blk.bit_length() - 1
    same = (ii >> shift) == (jj >> shift)
    ps = [jnp.where(same, -a, 0.0) for a in mats]
    xs = [eye + p for p in ps]
    for _ in range(shift - 1):
        ps = [_mm(p, p, B_INV_PASSES) for p in ps]
        xs = [x + _mm(x, p, B_INV_PASSES) for x, p in zip(xs, ps)]
    size = blk
    while size < top:
        shift += 1
        same_next = (ii >> shift) == (jj >> shift)
        sel = jnp.logical_and(same_next, jnp.logical_not(same))
        ys = [_mm(x, jnp.where(sel, a, 0.0), B_INV_PASSES) for x, a in zip(xs, mats)]
        xs = [x - _mm(y, x, B_INV_PASSES) for x, y in zip(xs, ys)]
        same = same_next
        size *= 2
    return xs


def _gdn_kernel(q_ref, k_ref, v_ref, z_ref, ba_ref, alog_ref, dtb_ref, nw_ref, s0_ref,
                o_ref, s_ref, *, c, hps, t_valid):
    hb = pl.program_id(1)
    ci = pl.program_id(2)
    dk = B_HEAD_DIM
    c2 = 2 * c
    assert c2 == LANES

    @pl.when(ci == 0)
    def _():
        s_ref[...] = s0_ref[...]

    ba = ba_ref[...]
    valid = lax.broadcasted_iota(jnp.int32, (c, LANES), 0) + ci * c < t_valid
    beta_all = jnp.where(valid, jax.nn.sigmoid(ba), 0.0)
    g_all = jnp.where(valid, -jnp.exp(alog_ref[pl.ds(hb, 1), :]) * _softplus(ba + dtb_ref[pl.ds(hb, 1), :]), 0.0)
    ci_, cj_ = lax.broadcasted_iota(jnp.int32, (c, c), 0), lax.broadcasted_iota(jnp.int32, (c, c), 1)
    gc_all = _select_rows(jnp.where(ci_ >= cj_, 1.0, 0.0).astype(BF16), g_all)
    gc_t = jnp.concatenate([gc_all, gc_all], axis=0).T
    ii = lax.broadcasted_iota(jnp.int32, (c2, c2), 0)
    jj = lax.broadcasted_iota(jnp.int32, (c2, c2), 1)
    shift = c.bit_length() - 1
    same = (ii >> shift) == (jj >> shift)
    incl = jnp.logical_and(same, ii >= jj)
    strict = jnp.logical_and(same, ii > jj)
    first = lax.broadcasted_iota(jnp.int32, (1, c2), 1) < c
    top = lax.broadcasted_iota(jnp.int32, (c2, 1), 0) < c
    nw = nw_ref[...]

    def stack_cols(arr, l0, l1):
        return jnp.concatenate([arr[:, l0:l0 + 1], arr[:, l1:l1 + 1]], axis=0)

    def own_half(r):
        return jnp.where(top, r[:, :dk], r[:, dk:])

    pairs = range(hps // 2)
    cols = lambda h: slice(h * dk, (h + 1) * dk)

    def prepare(qh):
        h0, h1 = 2 * qh, 2 * qh + 1
        q = q_ref[:, cols(qh)]
        k = k_ref[:, cols(qh)]
        q = q * lax.rsqrt(jnp.sum(q * q, axis=-1, keepdims=True) + 1e-6) * (dk ** -0.5)
        k = k * lax.rsqrt(jnp.sum(k * k, axis=-1, keepdims=True) + 1e-6)
        q2 = jnp.concatenate([q, q], axis=0)
        k2 = jnp.concatenate([k, k], axis=0)
        k2_b = k2.astype(BF16)
        beta = stack_cols(beta_all, h0, h1)
        gc = stack_cols(gc_all, hps + h0, hps + h1)
        gr = jnp.where(first, gc_t[hps + h0:hps + h0 + 1, :], gc_t[hps + h1:hps + h1 + 1, :])
        gl0 = gc_all[c - 1:c, hps + h0:hps + h0 + 1]
        gl1 = gc_all[c - 1:c, hps + h1:hps + h1 + 1]
        decay = jnp.exp(jnp.where(incl, gc - gr, NEG_BIG))
        eg = jnp.exp(gc)
        v2 = jnp.concatenate([v_ref[:, cols(h0)], v_ref[:, cols(h1)]], axis=0)
        return dict(
            a=jnp.where(strict, _dot_nt(k2_b, k2_b) * decay, 0.0) * beta,
            qk=jnp.where(incl, _dot_nt(q2.astype(BF16), k2_b) * decay, 0.0).astype(BF16),
            rhs=jnp.concatenate([v2 * beta, k2 * (beta * eg)], axis=1),
            q_eg=(q2 * eg).astype(BF16),
            k_dec=(k2 * jnp.exp(jnp.where(top, gl0, gl1) - gc)).astype(BF16),
            dec_cat=jnp.concatenate([jnp.broadcast_to(jnp.exp(gl0), (1, dk)),
                                     jnp.broadcast_to(jnp.exp(gl1), (1, dk))], axis=1))

    st = [prepare(qh) for qh in pairs]
    t_inv = _unit_lower_inverses([s["a"] for s in st], c2, c)
    uw = [_mm(t, s["rhs"], B_INV_PASSES) for t, s in zip(t_inv, st)]
    s_cat = [jnp.concatenate([s_ref[2 * qh], s_ref[2 * qh + 1]], axis=1) for qh in pairs]
    s_cat_b = [s.astype(BF16) for s in s_cat]
    v_new = [r[:, :dk] - own_half(_dot(r[:, dk:].astype(BF16), sb)) for r, sb in zip(uw, s_cat_b)]
    v_new_b = [v.astype(BF16) for v in v_new]
    outs = [own_half(_dot(s["q_eg"], sb)) + _dot(s["qk"], vb) for s, sb, vb in zip(st, s_cat_b, v_new_b)]
    for qh in pairs:
        h0, h1 = 2 * qh, 2 * qh + 1
        v = v_new[qh]
        v_blk = jnp.concatenate([jnp.where(top, v, 0.0), jnp.where(top, 0.0, v)], axis=1).astype(BF16)
        s_new = s_cat[qh] * st[qh]["dec_cat"] + _dot_tn(st[qh]["k_dec"], v_blk)
        s_ref[h0] = s_new[:, :dk]
        s_ref[h1] = s_new[:, dk:]
        o = outs[qh]
        o = o * lax.rsqrt(jnp.mean(o * o, axis=-1, keepdims=True) + RMS_EPS) * nw
        o = (o * _silu(jnp.concatenate([z_ref[:, cols(h0)], z_ref[:, cols(h1)]], axis=0))).astype(o_ref.dtype)
        o_ref[:, cols(h0)] = o[:c]
        o_ref[:, cols(h1)] = o[c:]


def gdn_scan(qkv_c, z_arr, z_col0, ba, alog_pad, dtb_pad, norm_w, s0, *, nseq, seq, t_valid, spare_rows=0):
    c = B_CHUNK
    hps = B_HEADS_PER_STEP
    nc = seq // c
    ng = B_V_HEADS // hps
    qw = hps // 2 * B_HEAD_DIM
    vw = hps * B_HEAD_DIM
    par_rows = alog_pad.shape[0]
    row = lambda s, h, ci: s * nc + ci
    return pl.pallas_call(
        functools.partial(_gdn_kernel, c=c, hps=hps, t_valid=t_valid),
        grid=(nseq, ng, nc),
        in_specs=[pl.BlockSpec((c, qw), lambda s, h, ci: (row(s, h, ci), h)),
                  pl.BlockSpec((c, qw), lambda s, h, ci: (row(s, h, ci), B_KEY_DIM // qw + h)),
                  pl.BlockSpec((c, vw), lambda s, h, ci: (row(s, h, ci), 2 * B_KEY_DIM // vw + h)),
                  pl.BlockSpec((c, vw), lambda s, h, ci: (row(s, h, ci), z_col0 // vw + h)),
                  pl.BlockSpec((c, LANES), lambda s, h, ci: (row(s, h, ci), h)),
                  pl.BlockSpec((par_rows, LANES), lambda s, h, ci: (0, 0)),
                  pl.BlockSpec((par_rows, LANES), lambda s, h, ci: (0, 0)),
                  pl.BlockSpec((1, B_HEAD_DIM), lambda s, h, ci: (0, 0)),
                  pl.BlockSpec((None, hps, B_HEAD_DIM, B_HEAD_DIM), lambda s, h, ci: (s, h, 0, 0))],
        out_specs=[pl.BlockSpec((c, vw), lambda s, h, ci: (row(s, h, ci), h)),
                   pl.BlockSpec((None, hps, B_HEAD_DIM, B_HEAD_DIM), lambda s, h, ci: (s, h, 0, 0))],
        out_shape=[jax.ShapeDtypeStruct((nseq * seq + spare_rows, B_VAL_DIM), BF16),
                   jax.ShapeDtypeStruct((nseq, B_V_HEADS, B_HEAD_DIM, B_HEAD_DIM), F32)],
        compiler_params=_params(3),
        name="gdn_scan",
    )(qkv_c, qkv_c, qkv_c, z_arr, ba, alog_pad, dtb_pad, norm_w, s0)


def _ssd_kernel(x_ref, b_ref, c_ref, z_ref, dt_ref, dtb_ref, alog_ref, dskip_ref, nw_ref, h0_ref,
                y_ref, hout_ref, ht_ref, *, c, nc, t_valid):
    g = pl.program_id(1)
    ci = pl.program_id(2)
    hpg = C_HEADS_PER_GROUP
    p = C_HEAD_DIM
    gcols = hpg * p

    @pl.when(ci == 0)
    def _():
        ht_ref[...] = h0_ref[...].reshape(gcols, C_D_STATE).T

    row = lax.broadcasted_iota(jnp.int32, (c, LANES), 0) + ci * c
    dt = jnp.where(row < t_valid, _softplus(dt_ref[...] + dtb_ref[pl.ds(g, 1), :]), 0.0)
    la = dt * -jnp.exp(alog_ref[pl.ds(g, 1), :])
    ii = lax.broadcasted_iota(jnp.int32, (c, c), 0)
    jj = lax.broadcasted_iota(jnp.int32, (c, c), 1)
    incl = ii >= jj
    acs = _select_rows(jnp.where(incl, 1.0, 0.0).astype(BF16), la)
    acs_t = acs.T
    el = lax.broadcasted_iota(jnp.int32, (LANES, gcols), 0)
    ej = lax.broadcasted_iota(jnp.int32, (LANES, gcols), 1)
    expand = jnp.where(el == (ej >> (p.bit_length() - 1)), 1.0, 0.0).astype(BF16)
    dt_e = _select_cols(dt, expand)
    acs_e = _select_cols(acs, expand)
    dskip_e = _select_cols(jnp.broadcast_to(dskip_ref[pl.ds(g, 1), :], (SUBLANES, LANES)), expand)[0:1, :]

    x = x_ref[...]
    bm = b_ref[...].astype(BF16)
    cm = c_ref[...].astype(BF16)
    xdt = (x * dt_e).astype(BF16)
    cb = _dot_nt(cm, bm)
    ht = ht_ref[...]
    y = _dot(cm, ht.astype(BF16)) * jnp.exp(acs_e) + dskip_e * x
    lane = lax.broadcasted_iota(jnp.int32, (c, 2 * p), 1)
    pairs = []
    for j in range(hpg // 2):
        xs = xdt[:, j * 2 * p:(j + 1) * 2 * p]
        ys = []
        for e in range(2):
            hd = 2 * j + e
            decay = jnp.exp(jnp.where(incl, acs[:, hd:hd + 1] - acs_t[hd:hd + 1, :], NEG_BIG))
            ys.append(_dot((cb * decay).astype(BF16), xs))
        pairs.append(jnp.where(lane < p, ys[0], ys[1]))
    y = y + jnp.concatenate(pairs, axis=1)
    y = y * _silu(z_ref[...])
    y = y * lax.rsqrt(jnp.mean(y * y, axis=-1, keepdims=True) + RMS_EPS) * nw_ref[...]
    y_ref[...] = y.astype(y_ref.dtype)

    last_e = acs_e[c - 1:c, :]
    xw = (x * (jnp.exp(last_e - acs_e) * dt_e)).astype(BF16)
    ht_new = ht * jnp.exp(last_e) + _dot_tn(bm, xw)
    ht_ref[...] = ht_new

    @pl.when(ci == nc - 1)
    def _():
        hout_ref[...] = ht_new.T.reshape(hpg, p, C_D_STATE)


def ssd_scan(xc, z_arr, dt, dtb_pad, alog_pad, dskip_pad, norm_w, h0, *, nseq, seq, t_valid, spare_rows=0):
    c = C_CHUNK
    nc = seq // c
    gc = C_GROUP_COLS
    b0 = C_D_INNER // C_D_STATE
    row = lambda s, g, ci: s * nc + ci
    par = pl.BlockSpec((C_GROUPS, LANES), lambda s, g, ci: (0, 0))
    return pl.pallas_call(
        functools.partial(_ssd_kernel, c=c, nc=nc, t_valid=t_valid),
        grid=(nseq, C_GROUPS, nc),
        in_specs=[pl.BlockSpec((c, gc), lambda s, g, ci: (row(s, g, ci), g)),
                  pl.BlockSpec((c, C_D_STATE), lambda s, g, ci: (row(s, g, ci), b0 + g)),
                  pl.BlockSpec((c, C_D_STATE), lambda s, g, ci: (row(s, g, ci), b0 + C_GROUPS + g)),
                  pl.BlockSpec((c, gc), lambda s, g, ci: (row(s, g, ci), g)),
                  pl.BlockSpec((c, LANES), lambda s, g, ci: (row(s, g, ci), g)),
                  par, par, par,
                  pl.BlockSpec((1, gc), lambda s, g, ci: (0, g)),
                  pl.BlockSpec((None, C_HEADS_PER_GROUP, C_HEAD_DIM, C_D_STATE), lambda s, g, ci: (s, g, 0, 0))],
        out_specs=[pl.BlockSpec((c, gc), lambda s, g, ci: (row(s, g, ci), g)),
                   pl.BlockSpec((None, C_HEADS_PER_GROUP, C_HEAD_DIM, C_D_STATE), lambda s, g, ci: (s, g, 0, 0))],
        out_shape=[jax.ShapeDtypeStruct((nseq * seq + spare_rows, C_D_INNER), BF16),
                   jax.ShapeDtypeStruct((nseq, C_HEADS, C_HEAD_DIM, C_D_STATE), F32)],
        scratch_shapes=[pltpu.VMEM((C_D_STATE, gc), F32)],
        compiler_params=_params(3),
        name="ssd_scan",
    )(xc, xc, xc, z_arr, dt, dtb_pad, alog_pad, dskip_pad, norm_w, h0)


def _rope_tables(mp, seq, n_sample, ts):
    pos = jnp.concatenate([jnp.arange(mp) % seq, PAST_LEN + jnp.arange(n_sample) % ts]).astype(F32)
    inv_freq = ROPE_THETA ** (-jnp.arange(A_ROT_HALF, dtype=F32) / A_ROT_HALF)
    ang = pos[:, None] * inv_freq[None, :]
    cos, sin = jnp.cos(ang), jnp.sin(ang)
    rest = A_HEAD_DIM - 2 * A_ROT_HALF
    rows = pos.shape[0]
    cos_full = jnp.concatenate([cos, cos, jnp.ones((rows, rest), F32)], axis=1)
    sin_full = jnp.concatenate([-sin, sin, jnp.zeros((rows, rest), F32)], axis=1)
    return cos_full, sin_full


def _lane_groups(cols, group):
    lead = cols.shape[:-1]
    n = cols.shape[-1]
    c = cols.reshape(*lead, n // group, group)
    c = jnp.pad(c, [(0, 0)] * (c.ndim - 1) + [(0, LANES - group)])
    return c.reshape(*lead, n // group * LANES)


def _pad_seq_rows(x, nseq, ts, seq):
    cols = x.shape[-1]
    x = jnp.pad(x.reshape(nseq, ts, cols), ((0, 0), (0, seq - ts), (0, 0)))
    return x.reshape(nseq * seq, cols)


def _conv_prev(state):
    return jnp.pad(state, ((0, 0), (SUBLANES - state.shape[1], 0), (0, 0)))


def _conv_taps(w):
    return jnp.pad(w, ((0, SUBLANES - w.shape[0]), (0, 0)))


def _last_rows(x, nseq, seq, n, col0, cols):
    return jnp.stack([x[(s + 1) * seq - n:(s + 1) * seq, col0:col0 + cols] for s in range(nseq)])


def kernel(x_prompt, x_sample, cache_a_kv_w128, cache_a_kv_w512, cache_a_kv_w2048, state_b_ssm, state_b_conv, state_c_ssm, state_c_conv, ln_g, ln_b, ffn1_w_gu, ffn1_w_down, ffn2_w_gu, ffn2_w_down, a_w_qkv, a_w_o, b_w_in, b_conv_w, b_a_log, b_dt_bias, b_norm_w, b_w_out, c_w_in, c_conv_w, c_conv_b, c_dt_bias, c_a_log, c_d, c_norm_w, c_w_out):
    a_bufs = (cache_a_kv_w128, cache_a_kv_w512, cache_a_kv_w2048)
    bp, seq, d = x_prompt.shape
    db, ts, _ = x_sample.shape
    mp, ms = bp * seq, db * ts
    assert seq % (A_BLOCK * max(A_DILATIONS)) == 0 and mp % ms == 0
    x = jnp.concatenate([x_prompt.reshape(mp, d), x_sample.reshape(ms, d)], axis=0)
    xb = x.astype(BF16)
    cos_full, sin_full = _rope_tables(mp, seq, ms, ts)
    wb16 = {name: w.astype(BF16) for name, w in dict(
        ffn1_gu=ffn1_w_gu, ffn1_down=ffn1_w_down, ffn2_gu=ffn2_w_gu, ffn2_down=ffn2_w_down,
        a_qkv=a_w_qkv, a_o=a_w_o, b_in=b_w_in, b_out=b_w_out, c_in=c_w_in, c_out=c_w_out).items()}

    def with_sample_rows(a, a_s, rows):
        return lax.dynamic_update_slice(a, a_s.reshape(db, rows, -1)[:, :ts].reshape(ms, -1), (mp, 0))

    outs = {}

    def mixer_a(xb, j):
        qkv = matmul(xb, wb16["a_qkv"], layer=j, n=a_w_qkv.shape[2], tn=1024)
        gw = A_GROUP_COLS
        o_p, l_p, o_s, l_s = [], [], [], []
        for g, dil in enumerate(A_DILATIONS):
            qkv_r = qkv_split(qkv, cos_full, sin_full, g=g, dil=dil, row0=0, batch=bp, seq=seq, tile=A_SPLIT_TILE)
            o, lse = attn_prompt(qkv_r)
            o_p.append(o)
            l_p.append(lse)
            new = qkv_split(qkv, cos_full, sin_full, g=g, dil=1, row0=mp, batch=1, seq=ms, tile=ms)
            q_s, k_s, v_s = (new[0, 0, :, sec * gw:(sec + 1) * gw].reshape(db, ts, A_HEADS, A_HEAD_DIM)
                             for sec in range(A_SECTIONS))
            o, lse = attn_sample(q_s, k_s, v_s, a_bufs[g], layer=j, dil=dil)
            o_s.append(o)
            l_s.append(lse)
            length = seq // dil
            tail = qkv_r[:, :, length - A_BLOCK:, gw:].transpose(0, 2, 1, 3)
            outs.setdefault("a_p%d" % g, []).append(tail.reshape(bp, A_BLOCK * dil, 2, A_HEADS, A_HEAD_DIM))
            outs.setdefault("a_new%d" % g, []).append(jnp.stack([k_s, v_s], axis=2))
        a = attn_merge_prompt(o_p, l_p, spare_rows=ms)
        a_s = attn_merge_rows(o_s, l_s).reshape(ms, gw).astype(BF16)
        return lax.dynamic_update_slice(a, a_s, (mp, 0)), wb16["a_o"], a_w_o.shape[1]

    def mixer_b(xb, j):
        hps = B_HEADS_PER_STEP
        ng = B_V_HEADS // hps
        w_in = b_w_in[j]
        main = B_CONV_DIM + B_VAL_DIM
        pb = matmul(xb, wb16["b_in"], layer=j, n=main, tn=1024)
        w_tail = jnp.concatenate([w_in[:, main:main + B_V_HEADS].reshape(d, ng, hps),
                                  w_in[:, main + B_V_HEADS:].reshape(d, ng, hps)], axis=2)
        ba = matmul(xb, _lane_groups(w_tail.reshape(d, -1), 2 * hps).astype(BF16)[None],
                    layer=0, n=ng * LANES, tn=ng * LANES)
        gate_lanes = lambda v: jnp.pad(_lane_groups(
            jnp.concatenate([jnp.zeros((ng, hps), F32), v.reshape(ng, hps)], axis=1).reshape(-1), 2 * hps
        ).reshape(ng, LANES), ((0, SUBLANES - ng), (0, 0)))
        alog_pad, dtb_pad = gate_lanes(b_a_log[j]), gate_lanes(b_dt_bias[j])
        taps = _conv_taps(b_conv_w[j])
        no_bias = jnp.zeros((1, B_CONV_DIM), F32)
        nw = b_norm_w[j].reshape(1, B_HEAD_DIM)
        qkv_c = conv_silu(pb, jnp.zeros((bp, SUBLANES, B_CONV_DIM), F32), taps, no_bias,
                          nseq=bp, seq=seq, col0=0, cols=B_CONV_DIM, tt=512)
        a_p, s_p = gdn_scan(qkv_c, pb, B_CONV_DIM, ba, alog_pad, dtb_pad, nw,
                            jnp.zeros((bp, B_V_HEADS, B_HEAD_DIM, B_HEAD_DIM), F32),
                            nseq=bp, seq=seq, t_valid=seq, spare_rows=ms)
        pb_s = _pad_seq_rows(pb[mp:], db, ts, B_CHUNK)
        ba_s = _pad_seq_rows(ba[mp:], db, ts, B_CHUNK)
        qkv_cs = conv_silu(pb_s, _conv_prev(state_b_conv[j]), taps, no_bias,
                           nseq=db, seq=B_CHUNK, col0=0, cols=B_CONV_DIM, tt=B_CHUNK)
        a_s, s_s = gdn_scan(qkv_cs, pb_s, B_CONV_DIM, ba_s, alog_pad, dtb_pad, nw,
                            state_b_ssm[j], nseq=db, seq=B_CHUNK, t_valid=ts)
        outs.setdefault("b_ssm_p", []).append(s_p)
        outs.setdefault("b_ssm_s", []).append(s_s)
        outs.setdefault("b_conv_p", []).append(_last_rows(pb, bp, seq, CONV_TAPS - 1, 0, B_CONV_DIM))
        outs.setdefault("b_conv_s", []).append(_last_rows(pb[mp:], db, ts, CONV_TAPS - 1, 0, B_CONV_DIM))
        return with_sample_rows(a_p, a_s, B_CHUNK), wb16["b_out"], 1024

    def mixer_c(xb, j):
        w_in = c_w_in[j]
        main = C_D_INNER + C_CONV_DIM
        pc = matmul(xb, wb16["c_in"], layer=j, n=main, tn=1024)
        dt = matmul(xb, _lane_groups(w_in[:, main:], C_HEADS_PER_GROUP).astype(BF16)[None],
                    layer=0, n=C_GROUPS * LANES, tn=C_GROUPS * LANES)
        head_lanes = lambda v: _lane_groups(v, C_HEADS_PER_GROUP).reshape(C_GROUPS, LANES)
        dtb_pad, alog_pad, dskip_pad = head_lanes(c_dt_bias[j]), head_lanes(c_a_log[j]), head_lanes(c_d[j])
        taps = _conv_taps(c_conv_w[j])
        bias = c_conv_b[j].reshape(1, C_CONV_DIM)
        nw = c_norm_w[j].reshape(1, C_D_INNER)
        xc = conv_silu(pc, jnp.zeros((bp, SUBLANES, C_CONV_DIM), F32), taps, bias,
                       nseq=bp, seq=seq, col0=C_D_INNER, cols=C_CONV_DIM, tt=512)
        a_p, h_p = ssd_scan(xc, pc, dt, dtb_pad, alog_pad, dskip_pad, nw,
                            jnp.zeros((bp, C_HEADS, C_HEAD_DIM, C_D_STATE), F32),
                            nseq=bp, seq=seq, t_valid=seq, spare_rows=ms)
        pc_s = _pad_seq_rows(pc[mp:], db, ts, C_CHUNK)
        dt_s = _pad_seq_rows(dt[mp:], db, ts, C_CHUNK)
        xc_s = conv_silu(pc_s, _conv_prev(state_c_conv[j]), taps, bias,
                         nseq=db, seq=C_CHUNK, col0=C_D_INNER, cols=C_CONV_DIM, tt=C_CHUNK)
        a_s, h_s = ssd_scan(xc_s, pc_s, dt_s, dtb_pad, alog_pad, dskip_pad, nw, state_c_ssm[j],
                            nseq=db, seq=C_CHUNK, t_valid=ts)
        outs.setdefault("c_ssm_p", []).append(h_p)
        outs.setdefault("c_ssm_s", []).append(h_s)
        outs.setdefault("c_conv_p", []).append(_last_rows(pc, bp, seq, CONV_TAPS - 1, C_D_INNER, C_CONV_DIM))
        outs.setdefault("c_conv_s", []).append(_last_rows(pc[mp:], db, ts, CONV_TAPS - 1, C_D_INNER, C_CONV_DIM))
        return with_sample_rows(a_p, a_s, C_CHUNK), wb16["c_out"], 1024

    mixers = (mixer_a, mixer_b, mixer_c)
    for i in range(DEPTH):
        x, xb = ffn_postnorm(x, xb, wb16["ffn1_gu"], wb16["ffn1_down"], ln_g[i, 0], ln_b[i, 0], layer=i)
        a, w_out, tk = mixers[i % 3](xb, i // 3)
        x, xb = matmul_postnorm(a, w_out, x, ln_g[i, 1], ln_b[i, 1], layer=i // 3, tk=tk)
        x, xb = ffn_postnorm(x, xb, wb16["ffn2_gu"], wb16["ffn2_down"], ln_g[i, 2], ln_b[i, 2], layer=i)

    st = lambda name: jnp.stack(outs[name])
    a_s = [jnp.concatenate([a_bufs[g][:, :, ts:], st("a_new%d" % g)], axis=2) for g in range(A_GROUPS)]
    return (x[:mp].reshape(bp, seq, d), x[mp:].reshape(db, ts, d),
            st("a_p0"), a_s[0], st("a_p1"), a_s[1], st("a_p2"), a_s[2],
            st("b_ssm_p"), st("b_ssm_s"), st("b_conv_p"), st("b_conv_s"),
            st("c_ssm_p"), st("c_ssm_s"), st("c_conv_p"), st("c_conv_s"))
```

```python
import functools

import jax
import jax.numpy as jnp
from jax import lax
from jax.experimental import pallas as pl
from jax.experimental.pallas import tpu as pltpu

F32 = jnp.float32
BF16 = jnp.bfloat16

VMEM_LIMIT_BYTES = 56 * 1024 * 1024
LANES = 128
SUBLANES = 8
ROW_TILE = 512
NEG_BIG = -1e30

DEPTH = 4
DN_ALPHA = (2.0 * DEPTH) ** 0.25
LN_EPS = 1e-5
RMS_EPS = 1e-6
PAST_LEN = 16384

A_DILATIONS = (1, 4, 16)
A_GROUPS = 3
A_HEADS = 8
A_HEAD_DIM = 128
A_ROT_HALF = A_HEAD_DIM // 8
A_BLOCK = 128
A_GROUP_COLS = A_HEADS * A_HEAD_DIM
A_SECTIONS = 3
A_SPLIT_TILE = 2048
ROPE_THETA = 500000.0

B_QK_HEADS = 16
B_V_HEADS = 32
B_HEAD_DIM = 128
B_KEY_DIM = B_QK_HEADS * B_HEAD_DIM
B_VAL_DIM = B_V_HEADS * B_HEAD_DIM
B_CONV_DIM = 2 * B_KEY_DIM + B_VAL_DIM
B_CHUNK = 64
B_HEADS_PER_STEP = 16
B_INV_BLOCK = 16
B_INV_PASSES = 1

C_D_INNER = 4096
C_HEADS = 64
C_HEAD_DIM = 64
C_GROUPS = 8
C_HEADS_PER_GROUP = C_HEADS // C_GROUPS
C_GROUP_COLS = C_D_INNER // C_GROUPS
C_D_STATE = 128
C_CONV_DIM = C_D_INNER + 2 * C_GROUPS * C_D_STATE
C_CHUNK = 128

CONV_TAPS = 4


def _params(n_axes):
    return pltpu.CompilerParams(
        dimension_semantics=("arbitrary",) * n_axes,
        vmem_limit_bytes=VMEM_LIMIT_BYTES,
    )


def _dot(a, b):
    return jnp.dot(a, b, preferred_element_type=F32)


def _dot_nt(a, b):
    return lax.dot_general(a, b, (((1,), (1,)), ((), ())), preferred_element_type=F32)


def _dot_tn(a, b):
    return lax.dot_general(a, b, (((0,), (0,)), ((), ())), preferred_element_type=F32)


def _split3(x):
    hi = x.astype(BF16)
    r = x - hi.astype(F32)
    mid = r.astype(BF16)
    lo = (r - mid.astype(F32)).astype(BF16)
    return hi, mid, lo


def _select_rows(sel, x):
    return sum(_dot(sel, part) for part in _split3(x))


def _select_cols(x, sel):
    return sum(_dot(part, sel) for part in _split3(x))


def _mm(a, b, passes):
    a_hi, b_hi = a.astype(BF16), b.astype(BF16)
    out = _dot(a_hi, b_hi)
    if passes == 3:
        a_lo = (a - a_hi.astype(F32)).astype(BF16)
        b_lo = (b - b_hi.astype(F32)).astype(BF16)
        out = out + _dot(a_hi, b_lo) + _dot(a_lo, b_hi)
    return out


def _silu(x):
    return x * jax.nn.sigmoid(x)


def _softplus(x):
    return jnp.maximum(x, 0.0) + jnp.log(1.0 + jnp.exp(-jnp.abs(x)))


def _row_tiles(rows, tm):
    n_full, rem = divmod(rows, tm)
    return n_full + (1 if rem else 0), n_full, rem


def _per_row_tile(i, n_full, rem, tm, body):
    if rem == 0:
        body(tm)
        return

    @pl.when(i < n_full)
    def _():
        body(tm)

    @pl.when(i == n_full)
    def _():
        body(rem)


def _mm_kernel(x_ref, w_ref, o_ref, *, tm, n_full, rem):
    def body(r):
        o_ref[:r, :] = _dot(x_ref[:r, :], w_ref[...]).astype(o_ref.dtype)

    _per_row_tile(pl.program_id(0), n_full, rem, tm, body)


def matmul(x, w, *, layer, n, tn, out_dtype=F32, tm=2 * ROW_TILE):
    rows, k = x.shape
    steps, n_full, rem = _row_tiles(rows, tm)
    return pl.pallas_call(
        functools.partial(_mm_kernel, tm=tm, n_full=n_full, rem=rem),
        grid=(steps, n // tn),
        in_specs=[pl.BlockSpec((tm, k), lambda i, j: (i, 0)),
                  pl.BlockSpec((None, k, tn), lambda i, j: (layer, 0, j))],
        out_specs=pl.BlockSpec((tm, tn), lambda i, j: (i, j)),
        out_shape=jax.ShapeDtypeStruct((rows, n), out_dtype),
        compiler_params=_params(2),
        name="matmul",
    )(x, w)


def _accumulate_then_postnorm(k, nk, r, part, scale, x_ref, g_ref, b_ref, y_ref, yb_ref, acc_ref):
    @pl.when(k == 0)
    def _():
        acc_ref[:r, :] = part

    @pl.when(k > 0)
    def _():
        acc_ref[:r, :] += part

    @pl.when(k == nk - 1)
    def _():
        y = DN_ALPHA * x_ref[:r, :] + scale * acc_ref[:r, :]
        mu = jnp.mean(y, axis=-1, keepdims=True)
        yc = y - mu
        var = jnp.mean(yc * yc, axis=-1, keepdims=True)
        out = yc * lax.rsqrt(var + LN_EPS) * g_ref[...] + b_ref[...]
        y_ref[:r, :] = out
        yb_ref[:r, :] = out.astype(BF16)


def _swiglu_kernel(x_ref, wg_ref, wu_ref, o_ref, *, tm, n_full, rem):
    def body(r):
        x = x_ref[:r, :]
        gate = _dot(x, wg_ref[...].astype(BF16))
        up = _dot(x, wu_ref[...].astype(BF16))
        o_ref[:r, :] = (_silu(gate) * up).astype(o_ref.dtype)

    _per_row_tile(pl.program_id(0), n_full, rem, tm, body)


def swiglu_up(x, w_gu, *, layer, tn=512, tm=2 * ROW_TILE):
    rows, k = x.shape
    f = w_gu.shape[2] // 2
    steps, n_full, rem = _row_tiles(rows, tm)
    nj = f // tn
    return pl.pallas_call(
        functools.partial(_swiglu_kernel, tm=tm, n_full=n_full, rem=rem),
        grid=(steps, nj),
        in_specs=[pl.BlockSpec((tm, k), lambda i, j: (i, 0)),
                  pl.BlockSpec((None, k, tn), lambda i, j: (layer, 0, j)),
                  pl.BlockSpec((None, k, tn), lambda i, j: (layer, 0, j + nj))],
        out_specs=pl.BlockSpec((tm, tn), lambda i, j: (i, j)),
        out_shape=jax.ShapeDtypeStruct((rows, f), BF16),
        compiler_params=_params(2),
        name="swiglu_up",
    )(x, w_gu, w_gu)


def _postnorm_kernel(a_ref, w_ref, x_ref, g_ref, b_ref, y_ref, yb_ref, acc_ref,
                     *, scale, nk, tm, n_full, rem):
    k = pl.program_id(1)

    def body(r):
        part = _dot(a_ref[:r, :], w_ref[...])
        _accumulate_then_postnorm(k, nk, r, part, scale, x_ref, g_ref, b_ref, y_ref, yb_ref, acc_ref)

    _per_row_tile(pl.program_id(0), n_full, rem, tm, body)


def matmul_postnorm(a, w, x, g, b, *, layer, scale, tk, tm=ROW_TILE):
    rows, kdim = a.shape
    d = w.shape[2]
    nk = kdim // tk
    steps, n_full, rem = _row_tiles(rows, tm)
    return pl.pallas_call(
        functools.partial(_postnorm_kernel, scale=scale, nk=nk, tm=tm, n_full=n_full, rem=rem),
        grid=(steps, nk),
        in_specs=[pl.BlockSpec((tm, tk), lambda i, k: (i, k)),
                  pl.BlockSpec((None, tk, d), lambda i, k: (layer, k, 0)),
                  pl.BlockSpec((tm, d), lambda i, k: (i, 0)),
                  pl.BlockSpec((1, d), lambda i, k: (0, 0)),
                  pl.BlockSpec((1, d), lambda i, k: (0, 0))],
        out_specs=[pl.BlockSpec((tm, d), lambda i, k: (i, 0)),
                   pl.BlockSpec((tm, d), lambda i, k: (i, 0))],
        out_shape=[jax.ShapeDtypeStruct((rows, d), F32),
                   jax.ShapeDtypeStruct((rows, d), BF16)],
        scratch_shapes=[pltpu.VMEM((tm, d), F32)],
        compiler_params=_params(2),
        name="matmul_postnorm",
    )(a, w, x, g.reshape(1, d), b.reshape(1, d))


def _strided_rows(r, n, stride):
    return pl.ds(r, n, stride=stride) if stride > 1 else pl.ds(0, n)


def _qkv_split_kernel(x_ref, cos_ref, sin_ref, o_ref, *, dil, n):
    rotated = pl.program_id(2) < (A_SECTIONS - 1) * A_HEADS
    lane = lax.broadcasted_iota(jnp.int32, (n, A_HEAD_DIM), 1)

    def emit(rotate):
        for r in range(dil):
            rows = _strided_rows(r, n, dil)
            x = x_ref[rows, :]
            if rotate:
                partner = jnp.where(lane < A_ROT_HALF,
                                    pltpu.roll(x, A_HEAD_DIM - A_ROT_HALF, axis=1),
                                    pltpu.roll(x, A_ROT_HALF, axis=1))
                x = x * cos_ref[rows, :] + partner * sin_ref[rows, :]
            o_ref[r] = x

    @pl.when(rotated)
    def _():
        emit(True)

    @pl.when(jnp.logical_not(rotated))
    def _():
        emit(False)


def qkv_split(qkv, cos_full, sin_full, *, g, dil, row0, batch, seq, tile):
    n = tile // dil
    tps = seq // tile
    r0 = row0 // tile
    in_col = lambda c: (c // A_HEADS * A_GROUPS + g) * A_HEADS + c % A_HEADS
    return pl.pallas_call(
        functools.partial(_qkv_split_kernel, dil=dil, n=n),
        grid=(batch, tps, A_SECTIONS * A_HEADS),
        in_specs=[pl.BlockSpec((tile, A_HEAD_DIM), lambda b, i, c: (r0 + b * tps + i, in_col(c))),
                  pl.BlockSpec((tile, A_HEAD_DIM), lambda b, i, c: (r0 + b * tps + i, 0)),
                  pl.BlockSpec((tile, A_HEAD_DIM), lambda b, i, c: (r0 + b * tps + i, 0))],
        out_specs=pl.BlockSpec((None, dil, n, A_HEAD_DIM), lambda b, i, c: (b, 0, i, c)),
        out_shape=jax.ShapeDtypeStruct((batch, dil, seq // dil, A_SECTIONS * A_GROUP_COLS), F32),
        compiler_params=_params(3),
        name="qkv_split",
    )(qkv, cos_full, sin_full)


def _rotate_pairs(x, cos, sin, lane):
    partner = jnp.where(lane < A_ROT_HALF,
                        pltpu.roll(x, A_HEAD_DIM - A_ROT_HALF, axis=1),
                        pltpu.roll(x, A_ROT_HALF, axis=1))
    return x * cos + partner * sin


def _qkv_project_kernel(x_ref, w_ref, cos_ref, sin_ref, *refs, tile):
    o_refs, head_ref = refs[:A_GROUPS], refs[A_GROUPS]
    j = pl.program_id(2)
    sec = j // A_GROUPS
    grp = j - sec * A_GROUPS
    res = _dot(x_ref[...], w_ref[...])
    cols = [slice(h * A_HEAD_DIM, (h + 1) * A_HEAD_DIM) for h in range(A_HEADS)]

    @pl.when(sec < A_SECTIONS - 1)
    def _():
        lane = lax.broadcasted_iota(jnp.int32, (tile, A_HEAD_DIM), 1)
        cos, sin = cos_ref[...], sin_ref[...]
        for h in range(A_HEADS):
            head_ref[h] = _rotate_pairs(res[:, cols[h]], cos, sin, lane)

    @pl.when(sec == A_SECTIONS - 1)
    def _():
        for h in range(A_HEADS):
            head_ref[h] = res[:, cols[h]]

    for g, dil in enumerate(A_DILATIONS):
        @pl.when(grp == g)
        def _(g=g, dil=dil):
            for r in range(dil):
                rows = _strided_rows(r, tile // dil, dil)
                for h in range(A_HEADS):
                    o_refs[g][r, :, cols[h]] = head_ref[h, rows, :]


def qkv_project(xb, w_qkv, cos_full, sin_full, *, layer, batch, seq, tile=ROW_TILE):
    d = xb.shape[1]
    tps = seq // tile
    out_specs, out_shapes = [], []
    for g, dil in enumerate(A_DILATIONS):
        out_specs.append(pl.BlockSpec(
            (None, dil, tile // dil, A_GROUP_COLS),
            lambda b, i, j, g=g: (b, 0, i, jnp.clip((j - g) // A_GROUPS, 0, A_SECTIONS - 1))))
        out_shapes.append(jax.ShapeDtypeStruct((batch, dil, seq // dil, A_SECTIONS * A_GROUP_COLS), F32))
    row_spec = lambda width: pl.BlockSpec((tile, width), lambda b, i, j: (b * tps + i, 0))
    return pl.pallas_call(
        functools.partial(_qkv_project_kernel, tile=tile),
        grid=(batch, tps, A_SECTIONS * A_GROUPS),
        in_specs=[row_spec(d),
                  pl.BlockSpec((None, d, A_GROUP_COLS), lambda b, i, j: (layer, 0, j)),
                  row_spec(A_HEAD_DIM), row_spec(A_HEAD_DIM)],
        out_specs=out_specs,
        out_shape=out_shapes,
        scratch_shapes=[pltpu.VMEM((A_HEADS, tile, A_HEAD_DIM), F32)],
        compiler_params=_params(3),
        name="qkv_project",
    )(xb, w_qkv, cos_full, sin_full)


def _attn_prompt_kernel(q_ref, kp_ref, kc_ref, vp_ref, vc_ref, o_ref, lse_ref):
    lb = pl.program_id(2)
    n = A_BLOCK
    qi = lax.broadcasted_iota(jnp.int32, (n, n), 0)
    kj = lax.broadcasted_iota(jnp.int32, (n, n), 1)
    mask_cur = kj <= qi
    mask_prev = jnp.logical_and(kj >= qi, lb > 0)
    scale = A_HEAD_DIM ** -0.5
    heads = range(A_HEADS)
    cols = [slice(h * A_HEAD_DIM, (h + 1) * A_HEAD_DIM) for h in heads]
    q = [q_ref[:, sl].astype(BF16) for sl in cols]
    s_c = [jnp.where(mask_cur, _dot_nt(q[h], kc_ref[:, cols[h]].astype(BF16)) * scale, NEG_BIG) for h in heads]
    s_p = [jnp.where(mask_prev, _dot_nt(q[h], kp_ref[:, cols[h]].astype(BF16)) * scale, NEG_BIG) for h in heads]
    m = [jnp.maximum(jnp.max(s_c[h], axis=-1, keepdims=True), jnp.max(s_p[h], axis=-1, keepdims=True)) for h in heads]
    p_c = [jnp.exp(s_c[h] - m[h]) for h in heads]
    p_p = [jnp.exp(s_p[h] - m[h]) for h in heads]
    l = [jnp.sum(p_c[h], axis=-1, keepdims=True) + jnp.sum(p_p[h], axis=-1, keepdims=True) for h in heads]
    o = [_dot(p_c[h].astype(BF16), vc_ref[:, cols[h]].astype(BF16))
         + _dot(p_p[h].astype(BF16), vp_ref[:, cols[h]].astype(BF16)) for h in heads]
    lse_all = jnp.zeros((n, LANES), F32)
    for h in heads:
        o_ref[:, cols[h]] = o[h] / l[h]
        lse_all = jnp.where(kj == h, m[h] + jnp.log(l[h]), lse_all)
    lse_ref[...] = lse_all


def attn_prompt(qkv_r):
    batch, dil, length, _ = qkv_r.shape
    nb = length // A_BLOCK
    spec = lambda sec, back: pl.BlockSpec(
        (None, None, A_BLOCK, A_GROUP_COLS), lambda b, r, lb: (b, r, jnp.maximum(lb - back, 0), sec))
    return pl.pallas_call(
        _attn_prompt_kernel,
        grid=(batch, dil, nb),
        in_specs=[spec(0, 0), spec(1, 1), spec(1, 0), spec(2, 1), spec(2, 0)],
        out_specs=[pl.BlockSpec((None, None, A_BLOCK, A_GROUP_COLS), lambda b, r, lb: (b, r, lb, 0)),
                   pl.BlockSpec((None, None, A_BLOCK, LANES), lambda b, r, lb: (b, r, lb, 0))],
        out_shape=[jax.ShapeDtypeStruct((batch, dil, length, A_GROUP_COLS), F32),
                   jax.ShapeDtypeStruct((batch, dil, length, LANES), F32)],
        compiler_params=_params(3),
        name="attn_prompt",
    )(qkv_r, qkv_r, qkv_r, qkv_r, qkv_r)


def _attn_sample_kernel(q_ref, kn_ref, vn_ref, *refs, dil, ts):
    n_res = len(refs) // 2 - 1
    cache_refs, (o_ref, lse_ref) = refs[:2 * n_res], refs[2 * n_res:]
    scale = A_HEAD_DIM ** -0.5
    jc = lax.broadcasted_iota(jnp.int32, (A_BLOCK, 1, 1), 0)
    un = lax.broadcasted_iota(jnp.int32, (ts, 1, 1), 0)
    kn = kn_ref[...]
    vn = vn_ref[...]
    for t in range(ts):
        kc_ref, vc_ref = cache_refs[2 * (t % n_res)], cache_refs[2 * (t % n_res) + 1]
        valid_n = (un <= t) if dil == 1 else (un == t)
        q = q_ref[t:t + 1]
        s_c = jnp.sum(kc_ref[...] * q, axis=-1, keepdims=True) * scale
        if dil == 1:
            s_c = jnp.where(jc >= t, s_c, NEG_BIG)
        s_n = jnp.where(valid_n, jnp.sum(kn * q, axis=-1, keepdims=True) * scale, NEG_BIG)
        m = jnp.maximum(jnp.max(s_c, axis=0, keepdims=True), jnp.max(s_n, axis=0, keepdims=True))
        p_c = jnp.exp(s_c - m)
        p_n = jnp.exp(s_n - m)
        l = jnp.sum(p_c, axis=0, keepdims=True) + jnp.sum(p_n, axis=0, keepdims=True)
        o = jnp.sum(p_c * vc_ref[...], axis=0, keepdims=True) + jnp.sum(p_n * vn, axis=0, keepdims=True)
        o_ref[t:t + 1] = o / l
        lse_ref[t:t + 1] = jnp.broadcast_to(m + jnp.log(l), (1, A_HEADS, A_HEAD_DIM))


def attn_sample(q, k_new, v_new, caches, *, layer, dil):
    db, ts = q.shape[:2]
    wb = caches.shape[2]
    assert wb == A_BLOCK * dil and (dil == 1 or ts <= dil)
    cache_v = caches.reshape(caches.shape[0], db, A_BLOCK, dil, 2, A_HEADS, A_HEAD_DIM)
    n_res = 1 if dil == 1 else ts
    tok_spec = pl.BlockSpec((None, ts, A_HEADS, A_HEAD_DIM), lambda b: (b, 0, 0, 0))
    cache_specs = [pl.BlockSpec((None, None, A_BLOCK, None, None, A_HEADS, A_HEAD_DIM),
                                lambda b, r=r, kv=kv: (layer, b, 0, r, kv, 0, 0))
                   for r in range(n_res) for kv in range(2)]
    out = jax.ShapeDtypeStruct((db, ts, A_HEADS, A_HEAD_DIM), F32)
    return pl.pallas_call(
        functools.partial(_attn_sample_kernel, dil=dil, ts=ts),
        grid=(db,),
        in_specs=[tok_spec, tok_spec, tok_spec] + cache_specs,
        out_specs=[tok_spec, tok_spec],
        out_shape=[out, out],
        compiler_params=_params(1),
        name="attn_sample",
    )(q, k_new, v_new, *([cache_v] * (2 * n_res)))


def _merge_groups(head_out, lses, a_ref):
    m = functools.reduce(jnp.maximum, lses)
    es = [jnp.exp(l - m) for l in lses]
    den = sum(es)
    ws = [e / den for e in es]
    for h in range(A_HEADS):
        a = sum(w[:, h:h + 1] * head_out(g, h) for g, w in enumerate(ws))
        a_ref[:, h * A_HEAD_DIM:(h + 1) * A_HEAD_DIM] = a.astype(a_ref.dtype)


def _attn_merge_rows_kernel(*refs):
    outs = [r[...] for r in refs[:A_GROUPS]]
    lses = [r[...] for r in refs[A_GROUPS:2 * A_GROUPS]]
    m = functools.reduce(jnp.maximum, lses)
    es = [jnp.exp(l - m) for l in lses]
    den = sum(es)
    refs[2 * A_GROUPS][...] = sum(e / den * o for e, o in zip(es, outs))


def attn_merge_rows(outs, lses):
    spec = pl.BlockSpec(outs[0].shape, lambda i: (0, 0, 0, 0))
    return pl.pallas_call(
        _attn_merge_rows_kernel,
        grid=(1,),
        in_specs=[spec] * (2 * A_GROUPS),
        out_specs=spec,
        out_shape=jax.ShapeDtypeStruct(outs[0].shape, F32),
        compiler_params=_params(1),
        name="attn_merge_rows",
    )(*outs, *lses)


def _attn_merge_prompt_kernel(*refs, tile):
    o_refs, l_refs = refs[:A_GROUPS], refs[A_GROUPS:2 * A_GROUPS]
    a_ref, o_nat, l_nat = refs[2 * A_GROUPS:]
    lses = []
    for g, dil in enumerate(A_DILATIONS):
        if dil == 1:
            lses.append(l_refs[g][0])
            continue
        n = tile // dil
        for r in range(dil):
            rows = pl.ds(r, n, stride=dil)
            l_nat[g, rows, :] = l_refs[g][r]
            for h in range(A_HEADS):
                o_nat[g * A_HEADS + h, rows, :] = o_refs[g][r, :, h * A_HEAD_DIM:(h + 1) * A_HEAD_DIM]
        lses.append(l_nat[g])

    def head_out(g, h):
        if A_DILATIONS[g] == 1:
            return o_refs[g][0, :, h * A_HEAD_DIM:(h + 1) * A_HEAD_DIM]
        return o_nat[g * A_HEADS + h]

    _merge_groups(head_out, lses, a_ref)


def attn_merge_prompt(outs, lses, *, spare_rows=0, tile=ROW_TILE):
    batch, _, seq, cols = outs[0].shape
    tps = seq // tile
    in_specs = []
    for width in (cols, LANES):
        for dil in A_DILATIONS:
            in_specs.append(pl.BlockSpec((None, dil, tile // dil, width), lambda b, i: (b, 0, i, 0)))
    return pl.pallas_call(
        functools.partial(_attn_merge_prompt_kernel, tile=tile),
        grid=(batch, tps),
        in_specs=in_specs,
        out_specs=pl.BlockSpec((tile, cols), lambda b, i: (b * tps + i, 0)),
        out_shape=jax.ShapeDtypeStruct((batch * seq + spare_rows, cols), BF16),
        scratch_shapes=[pltpu.VMEM((A_GROUPS * A_HEADS, tile, A_HEAD_DIM), F32),
                        pltpu.VMEM((A_GROUPS, tile, LANES), F32)],
        compiler_params=_params(2),
        name="attn_merge_prompt",
    )(*outs, *lses)


def _conv_kernel(u_ref, prev_ref, w_ref, b_ref, o_ref, full_ref, *, tt):
    t = pl.program_id(2)

    @pl.when(t == 0)
    def _():
        full_ref[0:SUBLANES, :] = prev_ref[...]

    u = u_ref[...]
    full_ref[SUBLANES:SUBLANES + tt, :] = u
    acc = u * w_ref[CONV_TAPS - 1:CONV_TAPS, :] + b_ref[...]
    for s in range(1, CONV_TAPS):
        acc = acc + full_ref[SUBLANES - s:SUBLANES - s + tt, :] * w_ref[CONV_TAPS - 1 - s:CONV_TAPS - s, :]
    o_ref[...] = _silu(acc)
    full_ref[0:SUBLANES, :] = u[tt - SUBLANES:tt, :]


def conv_silu(u, prev, w, bias, *, nseq, seq, col0, cols, tt, tc=512):
    nt = seq // tt
    c0 = col0 // tc
    return pl.pallas_call(
        functools.partial(_conv_kernel, tt=tt),
        grid=(nseq, cols // tc, nt),
        in_specs=[pl.BlockSpec((tt, tc), lambda s, c, t: (s * nt + t, c0 + c)),
                  pl.BlockSpec((None, SUBLANES, tc), lambda s, c, t: (s, 0, c)),
                  pl.BlockSpec((SUBLANES, tc), lambda s, c, t: (0, c)),
                  pl.BlockSpec((1, tc), lambda s, c, t: (0, c))],
        out_specs=pl.BlockSpec((tt, tc), lambda s, c, t: (s * nt + t, c)),
        out_shape=jax.ShapeDtypeStruct((nseq * seq, cols), F32),
        scratch_shapes=[pltpu.VMEM((SUBLANES + tt, tc), F32)],
        compiler_params=_params(3),
        name="conv_silu",
    )(u, prev, w, bias)


def _unit_lower_inverses(mats, n, top):
    blk = min(B_INV_BLOCK, top)
    ii = lax.broadcasted_iota(jnp.int32, (n, n), 0)
    jj = lax.broadcasted_iota(jnp.int32, (n, n), 1)
    eye = jnp.where(ii == jj, 1.0, 0.0).astype(F32)
    shift = blk.bit_length() - 1
    same = (ii >> shift) == (jj >> shift)
    ps = [jnp.where(same, -a, 0.0) for a in mats]
    xs = [eye + p for p in ps]
    for _ in range(shift - 1):
        ps = [_mm(p, p, B_INV_PASSES) for p in ps]
        xs = [x + _mm(x, p, B_INV_PASSES) for x, p in zip(xs, ps)]
    size = blk
    while size < top:
        shift += 1
        same_next = (ii >> shift) == (jj >> shift)
        sel = jnp.logical_and(same_next, jnp.logical_not(same))
        ys = [_mm(x, jnp.where(sel, a, 0.0), B_INV_PASSES) for x, a in zip(xs, mats)]
        xs = [x - _mm(y, x, B_INV_PASSES) for x, y in zip(xs, ys)]
        same = same_next
        size *= 2
    return xs


def _gdn_kernel(q_ref, k_ref, v_ref, z_ref, ba_ref, alog_ref, dtb_ref, nw_ref, s0_ref,
                o_ref, s_ref, *, c, hps, t_valid):
    hb = pl.program_id(1)
    ci = pl.program_id(2)
    dk = B_HEAD_DIM
    c2 = 2 * c
    assert c2 == LANES

    @pl.when(ci == 0)
    def _():
        s_ref[...] = s0_ref[...]

    ba = ba_ref[...]
    valid = lax.broadcasted_iota(jnp.int32, (c, LANES), 0) + ci * c < t_valid
    beta_all = jnp.where(valid, jax.nn.sigmoid(ba), 0.0)
    g_all = jnp.where(valid, -jnp.exp(alog_ref[pl.ds(hb, 1), :]) * _softplus(ba + dtb_ref[pl.ds(hb, 1), :]), 0.0)
    ci_, cj_ = lax.broadcasted_iota(jnp.int32, (c, c), 0), lax.broadcasted_iota(jnp.int32, (c, c), 1)
    gc_all = _select_rows(jnp.where(ci_ >= cj_, 1.0, 0.0).astype(BF16), g_all)
    gc_t = jnp.concatenate([gc_all, gc_all], axis=0).T
    ii = lax.broadcasted_iota(jnp.int32, (c2, c2), 0)
    jj = lax.broadcasted_iota(jnp.int32, (c2, c2), 1)
    shift = c.bit_length() - 1
    same = (ii >> shift) == (jj >> shift)
    incl = jnp.logical_and(same, ii >= jj)
    strict = jnp.logical_and(same, ii > jj)
    first = lax.broadcasted_iota(jnp.int32, (1, c2), 1) < c
    top = lax.broadcasted_iota(jnp.int32, (c2, 1), 0) < c
    nw = nw_ref[...]

    def stack_cols(arr, l0, l1):
        return jnp.concatenate([arr[:, l0:l0 + 1], arr[:, l1:l1 + 1]], axis=0)

    def own_half(r):
        return jnp.where(top, r[:, :dk], r[:, dk:])

    pairs = range(hps // 2)
    cols = lambda h: slice(h * dk, (h + 1) * dk)

    def prepare(qh):
        h0, h1 = 2 * qh, 2 * qh + 1
        q = q_ref[:, cols(qh)]
        k = k_ref[:, cols(qh)]
        q = q * lax.rsqrt(jnp.sum(q * q, axis=-1, keepdims=True) + 1e-6) * (dk ** -0.5)
        k = k * lax.rsqrt(jnp.sum(k * k, axis=-1, keepdims=True) + 1e-6)
        q2 = jnp.concatenate([q, q], axis=0)
        k2 = jnp.concatenate([k, k], axis=0)
        k2_b = k2.astype(BF16)
        beta = stack_cols(beta_all, h0, h1)
        gc = stack_cols(gc_all, hps + h0, hps + h1)
        gr = jnp.where(first, gc_t[hps + h0:hps + h0 + 1, :], gc_t[hps + h1:hps + h1 + 1, :])
        gl0 = gc_all[c - 1:c, hps + h0:hps + h0 + 1]
        gl1 = gc_all[c - 1:c, hps + h1:hps + h1 + 1]
        decay = jnp.exp(jnp.where(incl, gc - gr, NEG_BIG))
        eg = jnp.exp(gc)
        v2 = jnp.concatenate([v_ref[:, cols(h0)], v_ref[:, cols(h1)]], axis=0)
        return dict(
            a=jnp.where(strict, _dot_nt(k2_b, k2_b) * decay, 0.0) * beta,
            qk=jnp.where(incl, _dot_nt(q2.astype(BF16), k2_b) * decay, 0.0).astype(BF16),
            rhs=jnp.concatenate([v2 * beta, k2 * (beta * eg)], axis=1),
            q_eg=(q2 * eg).astype(BF16),
            k_dec=(k2 * jnp.exp(jnp.where(top, gl0, gl1) - gc)).astype(BF16),
            dec_cat=jnp.concatenate([jnp.broadcast_to(jnp.exp(gl0), (1, dk)),
                                     jnp.broadcast_to(jnp.exp(gl1), (1, dk))], axis=1))

    st = [prepare(qh) for qh in pairs]
    t_inv = _unit_lower_inverses([s["a"] for s in st], c2, c)
    uw = [_mm(t, s["rhs"], B_INV_PASSES) for t, s in zip(t_inv, st)]
    s_cat = [jnp.concatenate([s_ref[2 * qh], s_ref[2 * qh + 1]], axis=1) for qh in pairs]
    s_cat_b = [s.astype(BF16) for s in s_cat]
    v_new = [r[:, :dk] - own_half(_dot(r[:, dk:].astype(BF16), sb)) for r, sb in zip(uw, s_cat_b)]
    v_new_b = [v.astype(BF16) for v in v_new]
    outs = [own_half(_dot(s["q_eg"], sb)) + _dot(s["qk"], vb) for s, sb, vb in zip(st, s_cat_b, v_new_b)]
    for qh in pairs:
        h0, h1 = 2 * qh, 2 * qh + 1
        v = v_new[qh]
        v_blk = jnp.concatenate([jnp.where(top, v, 0.0), jnp.where(top, 0.0, v)], axis=1).astype(BF16)
        s_new = s_cat[qh] * st[qh]["dec_cat"] + _dot_tn(st[qh]["k_dec"], v_blk)
        s_ref[h0] = s_new[:, :dk]
        s_ref[h1] = s_new[:, dk:]
        o = outs[qh]
        o = o * lax.rsqrt(jnp.mean(o * o, axis=-1, keepdims=True) + RMS_EPS) * nw
        o = (o * _silu(jnp.concatenate([z_ref[:, cols(h0)], z_ref[:, cols(h1)]], axis=0))).astype(o_ref.dtype)
        o_ref[:, cols(h0)] = o[:c]
        o_ref[:, cols(h1)] = o[c:]


def gdn_scan(qkv_c, z_arr, z_col0, ba, alog_pad, dtb_pad, norm_w, s0, *, nseq, seq, t_valid, spare_rows=0):
    c = B_CHUNK
    hps = B_HEADS_PER_STEP
    nc = seq // c
    ng = B_V_HEADS // hps
    qw = hps // 2 * B_HEAD_DIM
    vw = hps * B_HEAD_DIM
    par_rows = alog_pad.shape[0]
    row = lambda s, h, ci: s * nc + ci
    return pl.pallas_call(
        functools.partial(_gdn_kernel, c=c, hps=hps, t_valid=t_valid),
        grid=(nseq, ng, nc),
        in_specs=[pl.BlockSpec((c, qw), lambda s, h, ci: (row(s, h, ci), h)),
                  pl.BlockSpec((c, qw), lambda s, h, ci: (row(s, h, ci), B_KEY_DIM // qw + h)),
                  pl.BlockSpec((c, vw), lambda s, h, ci: (row(s, h, ci), 2 * B_KEY_DIM // vw + h)),
                  pl.BlockSpec((c, vw), lambda s, h, ci: (row(s, h, ci), z_col0 // vw + h)),
                  pl.BlockSpec((c, LANES), lambda s, h, ci: (row(s, h, ci), h)),
                  pl.BlockSpec((par_rows, LANES), lambda s, h, ci: (0, 0)),
                  pl.BlockSpec((par_rows, LANES), lambda s, h, ci: (0, 0)),
                  pl.BlockSpec((1, B_HEAD_DIM), lambda s, h, ci: (0, 0)),
                  pl.BlockSpec((None, hps, B_HEAD_DIM, B_HEAD_DIM), lambda s, h, ci: (s, h, 0, 0))],
        out_specs=[pl.BlockSpec((c, vw), lambda s, h, ci: (row(s, h, ci), h)),
                   pl.BlockSpec((None, hps, B_HEAD_DIM, B_HEAD_DIM), lambda s, h, ci: (s, h, 0, 0))],
        out_shape=[jax.ShapeDtypeStruct((nseq * seq + spare_rows, B_VAL_DIM), BF16),
                   jax.ShapeDtypeStruct((nseq, B_V_HEADS, B_HEAD_DIM, B_HEAD_DIM), F32)],
        compiler_params=_params(3),
        name="gdn_scan",
    )(qkv_c, qkv_c, qkv_c, z_arr, ba, alog_pad, dtb_pad, norm_w, s0)


def _ssd_kernel(x_ref, b_ref, c_ref, z_ref, dt_ref, dtb_ref, alog_ref, dskip_ref, nw_ref, h0_ref,
                y_ref, hout_ref, ht_ref, *, c, nc, t_valid):
    g = pl.program_id(1)
    ci = pl.program_id(2)
    hpg = C_HEADS_PER_GROUP
    p = C_HEAD_DIM
    gcols = hpg * p

    @pl.when(ci == 0)
    def _():
        ht_ref[...] = h0_ref[...].reshape(gcols, C_D_STATE).T

    row = lax.broadcasted_iota(jnp.int32, (c, LANES), 0) + ci * c
    dt = jnp.where(row < t_valid, _softplus(dt_ref[...] + dtb_ref[pl.ds(g, 1), :]), 0.0)
    la = dt * -jnp.exp(alog_ref[pl.ds(g, 1), :])
    ii = lax.broadcasted_iota(jnp.int32, (c, c), 0)
    jj = lax.broadcasted_iota(jnp.int32, (c, c), 1)
    incl = ii >= jj
    acs = _select_rows(jnp.where(incl, 1.0, 0.0).astype(BF16), la)
    acs_t = acs.T
    el = lax.broadcasted_iota(jnp.int32, (LANES, gcols), 0)
    ej = lax.broadcasted_iota(jnp.int32, (LANES, gcols), 1)
    expand = jnp.where(el == (ej >> (p.bit_length() - 1)), 1.0, 0.0).astype(BF16)
    dt_e = _select_cols(dt, expand)
    acs_e = _select_cols(acs, expand)
    dskip_e = _select_cols(jnp.broadcast_to(dskip_ref[pl.ds(g, 1), :], (SUBLANES, LANES)), expand)[0:1, :]

    x = x_ref[...]
    bm = b_ref[...].astype(BF16)
    cm = c_ref[...].astype(BF16)
    xdt = (x * dt_e).astype(BF16)
    cb = _dot_nt(cm, bm)
    ht = ht_ref[...]
    y = _dot(cm, ht.astype(BF16)) * jnp.exp(acs_e) + dskip_e * x
    lane = lax.broadcasted_iota(jnp.int32, (c, 2 * p), 1)
    decay = [jnp.exp(jnp.where(incl, acs[:, hd:hd + 1] - acs_t[hd:hd + 1, :], NEG_BIG)) for hd in range(hpg)]
    mats = [(cb * dec).astype(BF16) for dec in decay]
    ys = [_dot(mats[hd], xdt[:, (hd // 2) * 2 * p:(hd // 2 + 1) * 2 * p]) for hd in range(hpg)]
    y = y + jnp.concatenate([jnp.where(lane < p, ys[2 * j], ys[2 * j + 1]) for j in range(hpg // 2)], axis=1)
    y = y * _silu(z_ref[...])
    y = y * lax.rsqrt(jnp.mean(y * y, axis=-1, keepdims=True) + RMS_EPS) * nw_ref[...]
    y_ref[...] = y.astype(y_ref.dtype)

    last_e = acs_e[c - 1:c, :]
    xw = (x * (jnp.exp(last_e - acs_e) * dt_e)).astype(BF16)
    ht_new = ht * jnp.exp(last_e) + _dot_tn(bm, xw)
    ht_ref[...] = ht_new

    @pl.when(ci == nc - 1)
    def _():
        hout_ref[...] = ht_new.T.reshape(hpg, p, C_D_STATE)


def ssd_scan(xc, z_arr, dt, dtb_pad, alog_pad, dskip_pad, norm_w, h0, *, nseq, seq, t_valid, spare_rows=0):
    c = C_CHUNK
    nc = seq // c
    gc = C_GROUP_COLS
    b0 = C_D_INNER // C_D_STATE
    row = lambda s, g, ci: s * nc + ci
    par = pl.BlockSpec((C_GROUPS, LANES), lambda s, g, ci: (0, 0))
    return pl.pallas_call(
        functools.partial(_ssd_kernel, c=c, nc=nc, t_valid=t_valid),
        grid=(nseq, C_GROUPS, nc),
        in_specs=[pl.BlockSpec((c, gc), lambda s, g, ci: (row(s, g, ci), g)),
                  pl.BlockSpec((c, C_D_STATE), lambda s, g, ci: (row(s, g, ci), b0 + g)),
                  pl.BlockSpec((c, C_D_STATE), lambda s, g, ci: (row(s, g, ci), b0 + C_GROUPS + g)),
                  pl.BlockSpec((c, gc), lambda s, g, ci: (row(s, g, ci), g)),
                  pl.BlockSpec((c, LANES), lambda s, g, ci: (row(s, g, ci), g)),
                  par, par, par,
                  pl.BlockSpec((1, gc), lambda s, g, ci: (0, g)),
                  pl.BlockSpec((None, C_HEADS_PER_GROUP, C_HEAD_DIM, C_D_STATE), lambda s, g, ci: (s, g, 0, 0))],
        out_specs=[pl.BlockSpec((c, gc), lambda s, g, ci: (row(s, g, ci), g)),
                   pl.BlockSpec((None, C_HEADS_PER_GROUP, C_HEAD_DIM, C_D_STATE), lambda s, g, ci: (s, g, 0, 0))],
        out_shape=[jax.ShapeDtypeStruct((nseq * seq + spare_rows, C_D_INNER), BF16),
                   jax.ShapeDtypeStruct((nseq, C_HEADS, C_HEAD_DIM, C_D_STATE), F32)],
        scratch_shapes=[pltpu.VMEM((C_D_STATE, gc), F32)],
        compiler_params=_params(3),
        name="ssd_scan",
    )(xc, xc, xc, z_arr, dt, dtb_pad, alog_pad, dskip_pad, norm_w, h0)


def _rope_tables(mp, seq, n_sample, ts):
    pos = jnp.concatenate([jnp.arange(mp) % seq, PAST_LEN + jnp.arange(n_sample) % ts]).astype(F32)
    inv_freq = ROPE_THETA ** (-jnp.arange(A_ROT_HALF, dtype=F32) / A_ROT_HALF)
    ang = pos[:, None] * inv_freq[None, :]
    cos, sin = jnp.cos(ang), jnp.sin(ang)
    rest = A_HEAD_DIM - 2 * A_ROT_HALF
    rows = pos.shape[0]
    cos_full = jnp.concatenate([cos, cos, jnp.ones((rows, rest), F32)], axis=1)
    sin_full = jnp.concatenate([-sin, sin, jnp.zeros((rows, rest), F32)], axis=1)
    return cos_full, sin_full


def _lane_groups(cols, group):
    lead = cols.shape[:-1]
    n = cols.shape[-1]
    c = cols.reshape(*lead, n // group, group)
    c = jnp.pad(c, [(0, 0)] * (c.ndim - 1) + [(0, LANES - group)])
    return c.reshape(*lead, n // group * LANES)


def _pad_seq_rows(x, nseq, ts, seq):
    cols = x.shape[-1]
    x = jnp.pad(x.reshape(nseq, ts, cols), ((0, 0), (0, seq - ts), (0, 0)))
    return x.reshape(nseq * seq, cols)


def _conv_prev(state):
    return jnp.pad(state, ((0, 0), (SUBLANES - state.shape[1], 0), (0, 0)))


def _conv_taps(w):
    return jnp.pad(w, ((0, SUBLANES - w.shape[0]), (0, 0)))


def _last_rows(x, nseq, seq, n, col0, cols):
    return jnp.stack([x[(s + 1) * seq - n:(s + 1) * seq, col0:col0 + cols] for s in range(nseq)])


def kernel(x_prompt, x_sample, cache_a_kv_w128, cache_a_kv_w512, cache_a_kv_w2048, state_b_ssm, state_b_conv, state_c_ssm, state_c_conv, ln_g, ln_b, ffn1_w_gu, ffn1_w_down, ffn2_w_gu, ffn2_w_down, a_w_qkv, a_w_o, b_w_in, b_conv_w, b_a_log, b_dt_bias, b_norm_w, b_w_out, c_w_in, c_conv_w, c_conv_b, c_dt_bias, c_a_log, c_d, c_norm_w, c_w_out):
    a_bufs = (cache_a_kv_w128, cache_a_kv_w512, cache_a_kv_w2048)
    bp, seq, d = x_prompt.shape
    db, ts, _ = x_sample.shape
    mp, ms = bp * seq, db * ts
    assert seq % (A_BLOCK * max(A_DILATIONS)) == 0 and mp % ms == 0
    x = jnp.concatenate([x_prompt.reshape(mp, d), x_sample.reshape(ms, d)], axis=0)
    xb = x.astype(BF16)
    cos_full, sin_full = _rope_tables(mp, seq, ms, ts)
    wb16 = {name: w.astype(BF16) for name, w in dict(
        ffn1_down=ffn1_w_down, ffn2_down=ffn2_w_down,
        a_qkv=a_w_qkv, a_o=a_w_o, b_in=b_w_in, b_out=b_w_out, c_in=c_w_in, c_out=c_w_out).items()}
    w_gu = dict(ffn1=ffn1_w_gu, ffn2=ffn2_w_gu)

    def with_sample_rows(a, a_s, rows):
        return lax.dynamic_update_slice(a, a_s.reshape(db, rows, -1)[:, :ts].reshape(ms, -1), (mp, 0))

    outs = {}

    def mixer_a(xb, j):
        qkv_groups = qkv_project(xb, wb16["a_qkv"], cos_full, sin_full, layer=j, batch=bp, seq=seq)
        qkv_s = matmul(xb[mp:], wb16["a_qkv"], layer=j, n=a_w_qkv.shape[2], tn=1024, tm=ms)
        gw = A_GROUP_COLS
        o_p, l_p, o_s, l_s = [], [], [], []
        for g, dil in enumerate(A_DILATIONS):
            qkv_r = qkv_groups[g]
            o, lse = attn_prompt(qkv_r)
            o_p.append(o)
            l_p.append(lse)
            new = qkv_split(qkv_s, cos_full[mp:], sin_full[mp:], g=g, dil=1, row0=0, batch=1, seq=ms, tile=ms)
            q_s, k_s, v_s = (new[0, 0, :, sec * gw:(sec + 1) * gw].reshape(db, ts, A_HEADS, A_HEAD_DIM)
                             for sec in range(A_SECTIONS))
            o, lse = attn_sample(q_s, k_s, v_s, a_bufs[g], layer=j, dil=dil)
            o_s.append(o)
            l_s.append(lse)
            length = seq // dil
            tail = qkv_r[:, :, length - A_BLOCK:, gw:].transpose(0, 2, 1, 3)
            outs.setdefault("a_p%d" % g, []).append(tail.reshape(bp, A_BLOCK * dil, 2, A_HEADS, A_HEAD_DIM))
            outs.setdefault("a_new%d" % g, []).append(jnp.stack([k_s, v_s], axis=2))
        a = attn_merge_prompt(o_p, l_p, spare_rows=ms)
        a_s = attn_merge_rows(o_s, l_s).reshape(ms, gw).astype(BF16)
        return lax.dynamic_update_slice(a, a_s, (mp, 0)), wb16["a_o"], a_w_o.shape[1]

    def mixer_b(xb, j):
        hps = B_HEADS_PER_STEP
        ng = B_V_HEADS // hps
        w_in = b_w_in[j]
        main = B_CONV_DIM + B_VAL_DIM
        pb = matmul(xb, wb16["b_in"], layer=j, n=main, tn=1024)
        w_tail = jnp.concatenate([w_in[:, main:main + B_V_HEADS].reshape(d, ng, hps),
                                  w_in[:, main + B_V_HEADS:].reshape(d, ng, hps)], axis=2)
        ba = matmul(xb, _lane_groups(w_tail.reshape(d, -1), 2 * hps).astype(BF16)[None],
                    layer=0, n=ng * LANES, tn=ng * LANES)
        gate_lanes = lambda v: jnp.pad(_lane_groups(
            jnp.concatenate([jnp.zeros((ng, hps), F32), v.reshape(ng, hps)], axis=1).reshape(-1), 2 * hps
        ).reshape(ng, LANES), ((0, SUBLANES - ng), (0, 0)))
        alog_pad, dtb_pad = gate_lanes(b_a_log[j]), gate_lanes(b_dt_bias[j])
        taps = _conv_taps(b_conv_w[j])
        no_bias = jnp.zeros((1, B_CONV_DIM), F32)
        nw = b_norm_w[j].reshape(1, B_HEAD_DIM)
        qkv_c = conv_silu(pb, jnp.zeros((bp, SUBLANES, B_CONV_DIM), F32), taps, no_bias,
                          nseq=bp, seq=seq, col0=0, cols=B_CONV_DIM, tt=512)
        a_p, s_p = gdn_scan(qkv_c, pb, B_CONV_DIM, ba, alog_pad, dtb_pad, nw,
                            jnp.zeros((bp, B_V_HEADS, B_HEAD_DIM, B_HEAD_DIM), F32),
                            nseq=bp, seq=seq, t_valid=seq, spare_rows=ms)
        pb_s = _pad_seq_rows(pb[mp:], db, ts, B_CHUNK)
        ba_s = _pad_seq_rows(ba[mp:], db, ts, B_CHUNK)
        qkv_cs = conv_silu(pb_s, _conv_prev(state_b_conv[j]), taps, no_bias,
                           nseq=db, seq=B_CHUNK, col0=0, cols=B_CONV_DIM, tt=B_CHUNK)
        a_s, s_s = gdn_scan(qkv_cs, pb_s, B_CONV_DIM, ba_s, alog_pad, dtb_pad, nw,
                            state_b_ssm[j], nseq=db, seq=B_CHUNK, t_valid=ts)
        outs.setdefault("b_ssm_p", []).append(s_p)
        outs.setdefault("b_ssm_s", []).append(s_s)
        outs.setdefault("b_conv_p", []).append(_last_rows(pb, bp, seq, CONV_TAPS - 1, 0, B_CONV_DIM))
        outs.setdefault("b_conv_s", []).append(_last_rows(pb[mp:], db, ts, CONV_TAPS - 1, 0, B_CONV_DIM))
        return with_sample_rows(a_p, a_s, B_CHUNK), wb16["b_out"], 1024

    def mixer_c(xb, j):
        w_in = c_w_in[j]
        main = C_D_INNER + C_CONV_DIM
        pc = matmul(xb, wb16["c_in"], layer=j, n=main, tn=1024)
        dt = matmul(xb, _lane_groups(w_in[:, main:], C_HEADS_PER_GROUP).astype(BF16)[None],
                    layer=0, n=C_GROUPS * LANES, tn=C_GROUPS * LANES)
        head_lanes = lambda v: _lane_groups(v, C_HEADS_PER_GROUP).reshape(C_GROUPS, LANES)
        dtb_pad, alog_pad, dskip_pad = head_lanes(c_dt_bias[j]), head_lanes(c_a_log[j]), head_lanes(c_d[j])
        taps = _conv_taps(c_conv_w[j])
        bias = c_conv_b[j].reshape(1, C_CONV_DIM)
        nw = c_norm_w[j].reshape(1, C_D_INNER)
        xc = conv_silu(pc, jnp.zeros((bp, SUBLANES, C_CONV_DIM), F32), taps, bias,
                       nseq=bp, seq=seq, col0=C_D_INNER, cols=C_CONV_DIM, tt=512)
        a_p, h_p = ssd_scan(xc, pc, dt, dtb_pad, alog_pad, dskip_pad, nw,
                            jnp.zeros((bp, C_HEADS, C_HEAD_DIM, C_D_STATE), F32),
                            nseq=bp, seq=seq, t_valid=seq, spare_rows=ms)
        pc_s = _pad_seq_rows(pc[mp:], db, ts, C_CHUNK)
        dt_s = _pad_seq_rows(dt[mp:], db, ts, C_CHUNK)
        xc_s = conv_silu(pc_s, _conv_prev(state_c_conv[j]), taps, bias,
                         nseq=db, seq=C_CHUNK, col0=C_D_INNER, cols=C_CONV_DIM, tt=C_CHUNK)
        a_s, h_s = ssd_scan(xc_s, pc_s, dt_s, dtb_pad, alog_pad, dskip_pad, nw, state_c_ssm[j],
                            nseq=db, seq=C_CHUNK, t_valid=ts)
        outs.setdefault("c_ssm_p", []).append(h_p)
        outs.setdefault("c_ssm_s", []).append(h_s)
        outs.setdefault("c_conv_p", []).append(_last_rows(pc, bp, seq, CONV_TAPS - 1, C_D_INNER, C_CONV_DIM))
        outs.setdefault("c_conv_s", []).append(_last_rows(pc[mp:], db, ts, CONV_TAPS - 1, C_D_INNER, C_CONV_DIM))
        return with_sample_rows(a_p, a_s, C_CHUNK), wb16["c_out"], 1024

    def ffn(x, xb, which, i, g, b):
        h = swiglu_up(xb, w_gu[which], layer=i)
        return matmul_postnorm(h, wb16[which + "_down"], x, g, b, layer=i, scale=0.5, tk=h.shape[1] // 4)

    mixers = (mixer_a, mixer_b, mixer_c)
    for i in range(DEPTH):
        x, xb = ffn(x, xb, "ffn1", i, ln_g[i, 0], ln_b[i, 0])
        a, w_out, tk = mixers[i % 3](xb, i // 3)
        x, xb = matmul_postnorm(a, w_out, x, ln_g[i, 1], ln_b[i, 1], layer=i // 3, scale=1.0, tk=tk)
        x, xb = ffn(x, xb, "ffn2", i, ln_g[i, 2], ln_b[i, 2])

    st = lambda name: jnp.stack(outs[name])
    a_s = [jnp.concatenate([a_bufs[g][:, :, ts:], st("a_new%d" % g)], axis=2) for g in range(A_GROUPS)]
    return (x[:mp].reshape(bp, seq, d), x[mp:].reshape(db, ts, d),
            st("a_p0"), a_s[0], st("a_p1"), a_s[1], st("a_p2"), a_s[2],
            st("b_ssm_p"), st("b_ssm_s"), st("b_conv_p"), st("b_conv_s"),
            st("c_ssm_p"), st("c_ssm_s"), st("c_conv_p"), st("c_conv_s"))
```

```python
import functools

import jax
import jax.numpy as jnp
from jax import lax
from jax.experimental import pallas as pl
from jax.experimental.pallas import tpu as pltpu

F32 = jnp.float32
BF16 = jnp.bfloat16

VMEM_LIMIT_BYTES = 56 * 1024 * 1024
LANES = 128
SUBLANES = 8
ROW_TILE = 512
POSTNORM_ROW_TILE = 256
NEG_BIG = -1e30

DEPTH = 4
DN_ALPHA = (2.0 * DEPTH) ** 0.25
LN_EPS = 1e-5
RMS_EPS = 1e-6
PAST_LEN = 16384

A_DILATIONS = (1, 4, 16)
A_GROUPS = 3
A_HEADS = 8
A_HEAD_DIM = 128
A_ROT_HALF = A_HEAD_DIM // 8
A_BLOCK = 128
A_GROUP_COLS = A_HEADS * A_HEAD_DIM
A_SECTIONS = 3
A_SPLIT_TILE = 2048
ROPE_THETA = 500000.0

B_QK_HEADS = 16
B_V_HEADS = 32
B_HEAD_DIM = 128
B_KEY_DIM = B_QK_HEADS * B_HEAD_DIM
B_VAL_DIM = B_V_HEADS * B_HEAD_DIM
B_CONV_DIM = 2 * B_KEY_DIM + B_VAL_DIM
B_CHUNK = 64
B_HEADS_PER_STEP = 16
B_INV_BLOCK = 16
B_INV_PASSES = 1

C_D_INNER = 4096
C_HEADS = 64
C_HEAD_DIM = 64
C_GROUPS = 8
C_HEADS_PER_GROUP = C_HEADS // C_GROUPS
C_GROUP_COLS = C_D_INNER // C_GROUPS
C_D_STATE = 128
C_CONV_DIM = C_D_INNER + 2 * C_GROUPS * C_D_STATE
C_CHUNK = 128

CONV_TAPS = 4


def _params(n_axes):
    return pltpu.CompilerParams(
        dimension_semantics=("arbitrary",) * n_axes,
        vmem_limit_bytes=VMEM_LIMIT_BYTES,
    )


def _dot(a, b):
    return jnp.dot(a, b, preferred_element_type=F32)


def _dot_nt(a, b):
    return lax.dot_general(a, b, (((1,), (1,)), ((), ())), preferred_element_type=F32)


def _dot_tn(a, b):
    return lax.dot_general(a, b, (((0,), (0,)), ((), ())), preferred_element_type=F32)


def _split3(x):
    hi = x.astype(BF16)
    r = x - hi.astype(F32)
    mid = r.astype(BF16)
    lo = (r - mid.astype(F32)).astype(BF16)
    return hi, mid, lo


def _select_rows(sel, x):
    return sum(_dot(sel, part) for part in _split3(x))


def _select_cols(x, sel):
    return sum(_dot(part, sel) for part in _split3(x))


def _mm(a, b, passes):
    a_hi, b_hi = a.astype(BF16), b.astype(BF16)
    out = _dot(a_hi, b_hi)
    if passes == 3:
        a_lo = (a - a_hi.astype(F32)).astype(BF16)
        b_lo = (b - b_hi.astype(F32)).astype(BF16)
        out = out + _dot(a_hi, b_lo) + _dot(a_lo, b_hi)
    return out


def _silu(x):
    return x * jax.nn.sigmoid(x)


def _softplus(x):
    return jnp.maximum(x, 0.0) + jnp.log(1.0 + jnp.exp(-jnp.abs(x)))


def _row_tiles(rows, tm):
    n_full, rem = divmod(rows, tm)
    return n_full + (1 if rem else 0), n_full, rem


def _per_row_tile(i, n_full, rem, tm, body):
    if rem == 0:
        body(tm)
        return

    @pl.when(i < n_full)
    def _():
        body(tm)

    @pl.when(i == n_full)
    def _():
        body(rem)


def _mm_kernel(x_ref, w_ref, o_ref, *, tm, n_full, rem):
    def body(r):
        o_ref[:r, :] = _dot(x_ref[:r, :], w_ref[...]).astype(o_ref.dtype)

    _per_row_tile(pl.program_id(0), n_full, rem, tm, body)


def matmul(x, w, *, layer, n, tn, out_dtype=F32, tm=2 * ROW_TILE):
    rows, k = x.shape
    steps, n_full, rem = _row_tiles(rows, tm)
    return pl.pallas_call(
        functools.partial(_mm_kernel, tm=tm, n_full=n_full, rem=rem),
        grid=(steps, n // tn),
        in_specs=[pl.BlockSpec((tm, k), lambda i, j: (i, 0)),
                  pl.BlockSpec((None, k, tn), lambda i, j: (layer, 0, j))],
        out_specs=pl.BlockSpec((tm, tn), lambda i, j: (i, j)),
        out_shape=jax.ShapeDtypeStruct((rows, n), out_dtype),
        compiler_params=_params(2),
        name="matmul",
    )(x, w)


def _swiglu_kernel(x_ref, wg_ref, wu_ref, o_ref, *, tm, n_full, rem):
    def body(r):
        x = x_ref[:r, :]
        gate = _dot(x, wg_ref[...].astype(BF16))
        up = _dot(x, wu_ref[...].astype(BF16))
        o_ref[:r, :] = (_silu(gate) * up).astype(o_ref.dtype)

    _per_row_tile(pl.program_id(0), n_full, rem, tm, body)


def swiglu_up(x, w_gu, *, layer, tn=512, tm=2 * ROW_TILE):
    rows, k = x.shape
    f = w_gu.shape[2] // 2
    steps, n_full, rem = _row_tiles(rows, tm)
    nj = f // tn
    return pl.pallas_call(
        functools.partial(_swiglu_kernel, tm=tm, n_full=n_full, rem=rem),
        grid=(steps, nj),
        in_specs=[pl.BlockSpec((tm, k), lambda i, j: (i, 0)),
                  pl.BlockSpec((None, k, tn), lambda i, j: (layer, 0, j)),
                  pl.BlockSpec((None, k, tn), lambda i, j: (layer, 0, j + nj))],
        out_specs=pl.BlockSpec((tm, tn), lambda i, j: (i, j)),
        out_shape=jax.ShapeDtypeStruct((rows, f), BF16),
        compiler_params=_params(2),
        name="swiglu_up",
    )(x, w_gu, w_gu)


def _postnorm_kernel(a_ref, w_ref, x_ref, g_ref, b_ref, y_ref, yb_ref, *, scale, tm, n_full, rem):
    def body(r):
        y = DN_ALPHA * x_ref[:r, :] + scale * _dot(a_ref[:r, :], w_ref[...])
        mu = jnp.mean(y, axis=-1, keepdims=True)
        yc = y - mu
        var = jnp.mean(yc * yc, axis=-1, keepdims=True)
        out = yc * lax.rsqrt(var + LN_EPS) * g_ref[...] + b_ref[...]
        y_ref[:r, :] = out
        yb_ref[:r, :] = out.astype(BF16)

    _per_row_tile(pl.program_id(0), n_full, rem, tm, body)


def matmul_postnorm(a, w, x, g, b, *, layer, scale, tm=POSTNORM_ROW_TILE):
    rows, kdim = a.shape
    d = w.shape[2]
    steps, n_full, rem = _row_tiles(rows, tm)
    return pl.pallas_call(
        functools.partial(_postnorm_kernel, scale=scale, tm=tm, n_full=n_full, rem=rem),
        grid=(steps,),
        in_specs=[pl.BlockSpec((tm, kdim), lambda i: (i, 0)),
                  pl.BlockSpec((None, kdim, d), lambda i: (layer, 0, 0), pipeline_mode=pl.Buffered(1)),
                  pl.BlockSpec((tm, d), lambda i: (i, 0)),
                  pl.BlockSpec((1, d), lambda i: (0, 0)),
                  pl.BlockSpec((1, d), lambda i: (0, 0))],
        out_specs=[pl.BlockSpec((tm, d), lambda i: (i, 0)),
                   pl.BlockSpec((tm, d), lambda i: (i, 0))],
        out_shape=[jax.ShapeDtypeStruct((rows, d), F32),
                   jax.ShapeDtypeStruct((rows, d), BF16)],
        compiler_params=_params(1),
        name="matmul_postnorm",
    )(a, w, x, g.reshape(1, d), b.reshape(1, d))


def _strided_rows(r, n, stride):
    return pl.ds(r, n, stride=stride) if stride > 1 else pl.ds(0, n)


def _qkv_split_kernel(x_ref, cos_ref, sin_ref, o_ref, *, dil, n):
    rotated = pl.program_id(2) < (A_SECTIONS - 1) * A_HEADS
    lane = lax.broadcasted_iota(jnp.int32, (n, A_HEAD_DIM), 1)

    def emit(rotate):
        for r in range(dil):
            rows = _strided_rows(r, n, dil)
            x = x_ref[rows, :]
            if rotate:
                partner = jnp.where(lane < A_ROT_HALF,
                                    pltpu.roll(x, A_HEAD_DIM - A_ROT_HALF, axis=1),
                                    pltpu.roll(x, A_ROT_HALF, axis=1))
                x = x * cos_ref[rows, :] + partner * sin_ref[rows, :]
            o_ref[r] = x

    @pl.when(rotated)
    def _():
        emit(True)

    @pl.when(jnp.logical_not(rotated))
    def _():
        emit(False)


def qkv_split(qkv, cos_full, sin_full, *, g, dil, row0, batch, seq, tile):
    n = tile // dil
    tps = seq // tile
    r0 = row0 // tile
    in_col = lambda c: (c // A_HEADS * A_GROUPS + g) * A_HEADS + c % A_HEADS
    return pl.pallas_call(
        functools.partial(_qkv_split_kernel, dil=dil, n=n),
        grid=(batch, tps, A_SECTIONS * A_HEADS),
        in_specs=[pl.BlockSpec((tile, A_HEAD_DIM), lambda b, i, c: (r0 + b * tps + i, in_col(c))),
                  pl.BlockSpec((tile, A_HEAD_DIM), lambda b, i, c: (r0 + b * tps + i, 0)),
                  pl.BlockSpec((tile, A_HEAD_DIM), lambda b, i, c: (r0 + b * tps + i, 0))],
        out_specs=pl.BlockSpec((None, dil, n, A_HEAD_DIM), lambda b, i, c: (b, 0, i, c)),
        out_shape=jax.ShapeDtypeStruct((batch, dil, seq // dil, A_SECTIONS * A_GROUP_COLS), F32),
        compiler_params=_params(3),
        name="qkv_split",
    )(qkv, cos_full, sin_full)


def _rotate_pairs(x, cos, sin, lane):
    partner = jnp.where(lane < A_ROT_HALF,
                        pltpu.roll(x, A_HEAD_DIM - A_ROT_HALF, axis=1),
                        pltpu.roll(x, A_ROT_HALF, axis=1))
    return x * cos + partner * sin


def _qkv_project_kernel(x_ref, w_ref, cos_ref, sin_ref, *refs, tile):
    o_refs, head_ref = refs[:A_GROUPS], refs[A_GROUPS]
    j = pl.program_id(2)
    sec = j // A_GROUPS
    grp = j - sec * A_GROUPS
    res = _dot(x_ref[...], w_ref[...])
    cols = [slice(h * A_HEAD_DIM, (h + 1) * A_HEAD_DIM) for h in range(A_HEADS)]

    @pl.when(sec < A_SECTIONS - 1)
    def _():
        lane = lax.broadcasted_iota(jnp.int32, (tile, A_HEAD_DIM), 1)
        cos, sin = cos_ref[...], sin_ref[...]
        for h in range(A_HEADS):
            head_ref[h] = _rotate_pairs(res[:, cols[h]], cos, sin, lane)

    @pl.when(sec == A_SECTIONS - 1)
    def _():
        for h in range(A_HEADS):
            head_ref[h] = res[:, cols[h]]

    for g, dil in enumerate(A_DILATIONS):
        @pl.when(grp == g)
        def _(g=g, dil=dil):
            for r in range(dil):
                rows = _strided_rows(r, tile // dil, dil)
                for h in range(A_HEADS):
                    o_refs[g][r, :, cols[h]] = head_ref[h, rows, :]


def qkv_project(xb, w_qkv, cos_full, sin_full, *, layer, batch, seq, tile=ROW_TILE):
    d = xb.shape[1]
    tps = seq // tile
    out_specs, out_shapes = [], []
    for g, dil in enumerate(A_DILATIONS):
        out_specs.append(pl.BlockSpec(
            (None, dil, tile // dil, A_GROUP_COLS),
            lambda b, i, j, g=g: (b, 0, i, jnp.clip((j - g) // A_GROUPS, 0, A_SECTIONS - 1))))
        out_shapes.append(jax.ShapeDtypeStruct((batch, dil, seq // dil, A_SECTIONS * A_GROUP_COLS), F32))
    row_spec = lambda width: pl.BlockSpec((tile, width), lambda b, i, j: (b * tps + i, 0))
    return pl.pallas_call(
        functools.partial(_qkv_project_kernel, tile=tile),
        grid=(batch, tps, A_SECTIONS * A_GROUPS),
        in_specs=[row_spec(d),
                  pl.BlockSpec((None, d, A_GROUP_COLS), lambda b, i, j: (layer, 0, j)),
                  row_spec(A_HEAD_DIM), row_spec(A_HEAD_DIM)],
        out_specs=out_specs,
        out_shape=out_shapes,
        scratch_shapes=[pltpu.VMEM((A_HEADS, tile, A_HEAD_DIM), F32)],
        compiler_params=_params(3),
        name="qkv_project",
    )(xb, w_qkv, cos_full, sin_full)


def _attn_prompt_kernel(q_ref, kp_ref, kc_ref, vp_ref, vc_ref, o_ref, lse_ref):
    lb = pl.program_id(2)
    n = A_BLOCK
    qi = lax.broadcasted_iota(jnp.int32, (n, n), 0)
    kj = lax.broadcasted_iota(jnp.int32, (n, n), 1)
    mask_cur = kj <= qi
    mask_prev = jnp.logical_and(kj >= qi, lb > 0)
    scale = A_HEAD_DIM ** -0.5
    heads = range(A_HEADS)
    cols = [slice(h * A_HEAD_DIM, (h + 1) * A_HEAD_DIM) for h in heads]
    q = [q_ref[:, sl].astype(BF16) for sl in cols]
    s_c = [jnp.where(mask_cur, _dot_nt(q[h], kc_ref[:, cols[h]].astype(BF16)) * scale, NEG_BIG) for h in heads]
    s_p = [jnp.where(mask_prev, _dot_nt(q[h], kp_ref[:, cols[h]].astype(BF16)) * scale, NEG_BIG) for h in heads]
    m = [jnp.maximum(jnp.max(s_c[h], axis=-1, keepdims=True), jnp.max(s_p[h], axis=-1, keepdims=True)) for h in heads]
    p_c = [jnp.exp(s_c[h] - m[h]) for h in heads]
    p_p = [jnp.exp(s_p[h] - m[h]) for h in heads]
    l = [jnp.sum(p_c[h], axis=-1, keepdims=True) + jnp.sum(p_p[h], axis=-1, keepdims=True) for h in heads]
    o = [_dot(p_c[h].astype(BF16), vc_ref[:, cols[h]].astype(BF16))
         + _dot(p_p[h].astype(BF16), vp_ref[:, cols[h]].astype(BF16)) for h in heads]
    lse_all = jnp.zeros((n, LANES), F32)
    for h in heads:
        o_ref[:, cols[h]] = o[h] / l[h]
        lse_all = jnp.where(kj == h, m[h] + jnp.log(l[h]), lse_all)
    lse_ref[...] = lse_all


def attn_prompt(qkv_r):
    batch, dil, length, _ = qkv_r.shape
    nb = length // A_BLOCK
    spec = lambda sec, back: pl.BlockSpec(
        (None, None, A_BLOCK, A_GROUP_COLS), lambda b, r, lb: (b, r, jnp.maximum(lb - back, 0), sec))
    return pl.pallas_call(
        _attn_prompt_kernel,
        grid=(batch, dil, nb),
        in_specs=[spec(0, 0), spec(1, 1), spec(1, 0), spec(2, 1), spec(2, 0)],
        out_specs=[pl.BlockSpec((None, None, A_BLOCK, A_GROUP_COLS), lambda b, r, lb: (b, r, lb, 0)),
                   pl.BlockSpec((None, None, A_BLOCK, LANES), lambda b, r, lb: (b, r, lb, 0))],
        out_shape=[jax.ShapeDtypeStruct((batch, dil, length, A_GROUP_COLS), F32),
                   jax.ShapeDtypeStruct((batch, dil, length, LANES), F32)],
        compiler_params=_params(3),
        name="attn_prompt",
    )(qkv_r, qkv_r, qkv_r, qkv_r, qkv_r)


def _attn_sample_kernel(q_ref, kn_ref, vn_ref, *refs, dil, ts):
    n_res = len(refs) // 2 - 1
    cache_refs, (o_ref, lse_ref) = refs[:2 * n_res], refs[2 * n_res:]
    scale = A_HEAD_DIM ** -0.5
    jc = lax.broadcasted_iota(jnp.int32, (A_BLOCK, 1, 1), 0)
    un = lax.broadcasted_iota(jnp.int32, (ts, 1, 1), 0)
    kn = kn_ref[...]
    vn = vn_ref[...]
    for t in range(ts):
        kc_ref, vc_ref = cache_refs[2 * (t % n_res)], cache_refs[2 * (t % n_res) + 1]
        valid_n = (un <= t) if dil == 1 else (un == t)
        q = q_ref[t:t + 1]
        s_c = jnp.sum(kc_ref[...] * q, axis=-1, keepdims=True) * scale
        if dil == 1:
            s_c = jnp.where(jc >= t, s_c, NEG_BIG)
        s_n = jnp.where(valid_n, jnp.sum(kn * q, axis=-1, keepdims=True) * scale, NEG_BIG)
        m = jnp.maximum(jnp.max(s_c, axis=0, keepdims=True), jnp.max(s_n, axis=0, keepdims=True))
        p_c = jnp.exp(s_c - m)
        p_n = jnp.exp(s_n - m)
        l = jnp.sum(p_c, axis=0, keepdims=True) + jnp.sum(p_n, axis=0, keepdims=True)
        o = jnp.sum(p_c * vc_ref[...], axis=0, keepdims=True) + jnp.sum(p_n * vn, axis=0, keepdims=True)
        o_ref[t:t + 1] = o / l
        lse_ref[t:t + 1] = jnp.broadcast_to(m + jnp.log(l), (1, A_HEADS, A_HEAD_DIM))


def attn_sample(q, k_new, v_new, caches, *, layer, dil):
    db, ts = q.shape[:2]
    wb = caches.shape[2]
    assert wb == A_BLOCK * dil and (dil == 1 or ts <= dil)
    cache_v = caches.reshape(caches.shape[0], db, A_BLOCK, dil, 2, A_HEADS, A_HEAD_DIM)
    n_res = 1 if dil == 1 else ts
    tok_spec = pl.BlockSpec((None, ts, A_HEADS, A_HEAD_DIM), lambda b: (b, 0, 0, 0))
    cache_specs = [pl.BlockSpec((None, None, A_BLOCK, None, None, A_HEADS, A_HEAD_DIM),
                                lambda b, r=r, kv=kv: (layer, b, 0, r, kv, 0, 0))
                   for r in range(n_res) for kv in range(2)]
    out = jax.ShapeDtypeStruct((db, ts, A_HEADS, A_HEAD_DIM), F32)
    return pl.pallas_call(
        functools.partial(_attn_sample_kernel, dil=dil, ts=ts),
        grid=(db,),
        in_specs=[tok_spec, tok_spec, tok_spec] + cache_specs,
        out_specs=[tok_spec, tok_spec],
        out_shape=[out, out],
        compiler_params=_params(1),
        name="attn_sample",
    )(q, k_new, v_new, *([cache_v] * (2 * n_res)))


def _merge_groups(head_out, lses, a_ref):
    m = functools.reduce(jnp.maximum, lses)
    es = [jnp.exp(l - m) for l in lses]
    den = sum(es)
    ws = [e / den for e in es]
    for h in range(A_HEADS):
        a = sum(w[:, h:h + 1] * head_out(g, h) for g, w in enumerate(ws))
        a_ref[:, h * A_HEAD_DIM:(h + 1) * A_HEAD_DIM] = a.astype(a_ref.dtype)


def _attn_merge_rows_kernel(*refs):
    outs = [r[...] for r in refs[:A_GROUPS]]
    lses = [r[...] for r in refs[A_GROUPS:2 * A_GROUPS]]
    m = functools.reduce(jnp.maximum, lses)
    es = [jnp.exp(l - m) for l in lses]
    den = sum(es)
    refs[2 * A_GROUPS][...] = sum(e / den * o for e, o in zip(es, outs))


def attn_merge_rows(outs, lses):
    spec = pl.BlockSpec(outs[0].shape, lambda i: (0, 0, 0, 0))
    return pl.pallas_call(
        _attn_merge_rows_kernel,
        grid=(1,),
        in_specs=[spec] * (2 * A_GROUPS),
        out_specs=spec,
        out_shape=jax.ShapeDtypeStruct(outs[0].shape, F32),
        compiler_params=_params(1),
        name="attn_merge_rows",
    )(*outs, *lses)


def _attn_merge_prompt_kernel(*refs, tile):
    o_refs, l_refs = refs[:A_GROUPS], refs[A_GROUPS:2 * A_GROUPS]
    a_ref, o_nat, l_nat = refs[2 * A_GROUPS:]
    lses = []
    for g, dil in enumerate(A_DILATIONS):
        if dil == 1:
            lses.append(l_refs[g][0])
            continue
        n = tile // dil
        for r in range(dil):
            rows = pl.ds(r, n, stride=dil)
            l_nat[g, rows, :] = l_refs[g][r]
            for h in range(A_HEADS):
                o_nat[g * A_HEADS + h, rows, :] = o_refs[g][r, :, h * A_HEAD_DIM:(h + 1) * A_HEAD_DIM]
        lses.append(l_nat[g])

    def head_out(g, h):
        if A_DILATIONS[g] == 1:
            return o_refs[g][0, :, h * A_HEAD_DIM:(h + 1) * A_HEAD_DIM]
        return o_nat[g * A_HEADS + h]

    _merge_groups(head_out, lses, a_ref)


def attn_merge_prompt(outs, lses, *, spare_rows=0, tile=ROW_TILE):
    batch, _, seq, cols = outs[0].shape
    tps = seq // tile
    in_specs = []
    for width in (cols, LANES):
        for dil in A_DILATIONS:
            in_specs.append(pl.BlockSpec((None, dil, tile // dil, width), lambda b, i: (b, 0, i, 0)))
    return pl.pallas_call(
        functools.partial(_attn_merge_prompt_kernel, tile=tile),
        grid=(batch, tps),
        in_specs=in_specs,
        out_specs=pl.BlockSpec((tile, cols), lambda b, i: (b * tps + i, 0)),
        out_shape=jax.ShapeDtypeStruct((batch * seq + spare_rows, cols), BF16),
        scratch_shapes=[pltpu.VMEM((A_GROUPS * A_HEADS, tile, A_HEAD_DIM), F32),
                        pltpu.VMEM((A_GROUPS, tile, LANES), F32)],
        compiler_params=_params(2),
        name="attn_merge_prompt",
    )(*outs, *lses)


def _conv_kernel(u_ref, prev_ref, w_ref, b_ref, o_ref, full_ref, *, tt):
    t = pl.program_id(2)

    @pl.when(t == 0)
    def _():
        full_ref[0:SUBLANES, :] = prev_ref[...]

    u = u_ref[...]
    full_ref[SUBLANES:SUBLANES + tt, :] = u
    acc = u * w_ref[CONV_TAPS - 1:CONV_TAPS, :] + b_ref[...]
    for s in range(1, CONV_TAPS):
        acc = acc + full_ref[SUBLANES - s:SUBLANES - s + tt, :] * w_ref[CONV_TAPS - 1 - s:CONV_TAPS - s, :]
    o_ref[...] = _silu(acc)
    full_ref[0:SUBLANES, :] = u[tt - SUBLANES:tt, :]


def conv_silu(u, prev, w, bias, *, nseq, seq, col0, cols, tt, tc=512):
    nt = seq // tt
    c0 = col0 // tc
    return pl.pallas_call(
        functools.partial(_conv_kernel, tt=tt),
        grid=(nseq, cols // tc, nt),
        in_specs=[pl.BlockSpec((tt, tc), lambda s, c, t: (s * nt + t, c0 + c)),
                  pl.BlockSpec((None, SUBLANES, tc), lambda s, c, t: (s, 0, c)),
                  pl.BlockSpec((SUBLANES, tc), lambda s, c, t: (0, c)),
                  pl.BlockSpec((1, tc), lambda s, c, t: (0, c))],
        out_specs=pl.BlockSpec((tt, tc), lambda s, c, t: (s * nt + t, c)),
        out_shape=jax.ShapeDtypeStruct((nseq * seq, cols), F32),
        scratch_shapes=[pltpu.VMEM((SUBLANES + tt, tc), F32)],
        compiler_params=_params(3),
        name="conv_silu",
    )(u, prev, w, bias)


def _unit_lower_inverses(mats, n, top):
    blk = min(B_INV_BLOCK, top)
    ii = lax.broadcasted_iota(jnp.int32, (n, n), 0)
    jj = lax.broadcasted_iota(jnp.int32, (n, n), 1)
    eye = jnp.where(ii == jj, 1.0, 0.0).astype(F32)
    shift = blk.bit_length() - 1
    same = (ii >> shift) == (jj >> shift)
    ps = [jnp.where(same, -a, 0.0) for a in mats]
    xs = [eye + p for p in ps]
    for _ in range(shift - 1):
        ps = [_mm(p, p, B_INV_PASSES) for p in ps]
        xs = [x + _mm(x, p, B_INV_PASSES) for x, p in zip(xs, ps)]
    size = blk
    while size < top:
        shift += 1
        same_next = (ii >> shift) == (jj >> shift)
        sel = jnp.logical_and(same_next, jnp.logical_not(same))
        ys = [_mm(x, jnp.where(sel, a, 0.0), B_INV_PASSES) for x, a in zip(xs, mats)]
        xs = [x - _mm(y, x, B_INV_PASSES) for x, y in zip(xs, ys)]
        same = same_next
        size *= 2
    return xs


def _gdn_kernel(q_ref, k_ref, v_ref, z_ref, ba_ref, alog_ref, dtb_ref, nw_ref, s0_ref,
                o_ref, s_ref, *, c, hps, t_valid):
    hb = pl.program_id(1)
    ci = pl.program_id(2)
    dk = B_HEAD_DIM
    c2 = 2 * c
    assert c2 == LANES

    @pl.when(ci == 0)
    def _():
        s_ref[...] = s0_ref[...]

    ba = ba_ref[...]
    valid = lax.broadcasted_iota(jnp.int32, (c, LANES), 0) + ci * c < t_valid
    beta_all = jnp.where(valid, jax.nn.sigmoid(ba), 0.0)
    g_all = jnp.where(valid, -jnp.exp(alog_ref[pl.ds(hb, 1), :]) * _softplus(ba + dtb_ref[pl.ds(hb, 1), :]), 0.0)
    ci_, cj_ = lax.broadcasted_iota(jnp.int32, (c, c), 0), lax.broadcasted_iota(jnp.int32, (c, c), 1)
    gc_all = _select_rows(jnp.where(ci_ >= cj_, 1.0, 0.0).astype(BF16), g_all)
    gc_t = jnp.concatenate([gc_all, gc_all], axis=0).T
    ii = lax.broadcasted_iota(jnp.int32, (c2, c2), 0)
    jj = lax.broadcasted_iota(jnp.int32, (c2, c2), 1)
    shift = c.bit_length() - 1
    same = (ii >> shift) == (jj >> shift)
    incl = jnp.logical_and(same, ii >= jj)
    strict = jnp.logical_and(same, ii > jj)
    first = lax.broadcasted_iota(jnp.int32, (1, c2), 1) < c
    top = lax.broadcasted_iota(jnp.int32, (c2, 1), 0) < c
    nw = nw_ref[...]

    def stack_cols(arr, l0, l1):
        return jnp.concatenate([arr[:, l0:l0 + 1], arr[:, l1:l1 + 1]], axis=0)

    def own_half(r):
        return jnp.where(top, r[:, :dk], r[:, dk:])

    pairs = range(hps // 2)
    cols = lambda h: slice(h * dk, (h + 1) * dk)

    def prepare(qh):
        h0, h1 = 2 * qh, 2 * qh + 1
        q = q_ref[:, cols(qh)]
        k = k_ref[:, cols(qh)]
        q = q * lax.rsqrt(jnp.sum(q * q, axis=-1, keepdims=True) + 1e-6) * (dk ** -0.5)
        k = k * lax.rsqrt(jnp.sum(k * k, axis=-1, keepdims=True) + 1e-6)
        q2 = jnp.concatenate([q, q], axis=0)
        k2 = jnp.concatenate([k, k], axis=0)
        k2_b = k2.astype(BF16)
        beta = stack_cols(beta_all, h0, h1)
        gc = stack_cols(gc_all, hps + h0, hps + h1)
        gr = jnp.where(first, gc_t[hps + h0:hps + h0 + 1, :], gc_t[hps + h1:hps + h1 + 1, :])
        gl0 = gc_all[c - 1:c, hps + h0:hps + h0 + 1]
        gl1 = gc_all[c - 1:c, hps + h1:hps + h1 + 1]
        decay = jnp.exp(jnp.where(incl, gc - gr, NEG_BIG))
        eg = jnp.exp(gc)
        v2 = jnp.concatenate([v_ref[:, cols(h0)], v_ref[:, cols(h1)]], axis=0)
        return dict(
            a=jnp.where(strict, _dot_nt(k2_b, k2_b) * decay, 0.0) * beta,
            qk=jnp.where(incl, _dot_nt(q2.astype(BF16), k2_b) * decay, 0.0).astype(BF16),
            rhs=jnp.concatenate([v2 * beta, k2 * (beta * eg)], axis=1),
            q_eg=(q2 * eg).astype(BF16),
            k_dec=(k2 * jnp.exp(jnp.where(top, gl0, gl1) - gc)).astype(BF16),
            dec_cat=jnp.concatenate([jnp.broadcast_to(jnp.exp(gl0), (1, dk)),
                                     jnp.broadcast_to(jnp.exp(gl1), (1, dk))], axis=1))

    st = [prepare(qh) for qh in pairs]
    t_inv = _unit_lower_inverses([s["a"] for s in st], c2, c)
    uw = [_mm(t, s["rhs"], B_INV_PASSES) for t, s in zip(t_inv, st)]
    s_cat = [jnp.concatenate([s_ref[2 * qh], s_ref[2 * qh + 1]], axis=1) for qh in pairs]
    s_cat_b = [s.astype(BF16) for s in s_cat]
    v_new = [r[:, :dk] - own_half(_dot(r[:, dk:].astype(BF16), sb)) for r, sb in zip(uw, s_cat_b)]
    v_new_b = [v.astype(BF16) for v in v_new]
    outs = [own_half(_dot(s["q_eg"], sb)) + _dot(s["qk"], vb) for s, sb, vb in zip(st, s_cat_b, v_new_b)]
    for qh in pairs:
        h0, h1 = 2 * qh, 2 * qh + 1
        v = v_new[qh]
        v_blk = jnp.concatenate([jnp.where(top, v, 0.0), jnp.where(top, 0.0, v)], axis=1).astype(BF16)
        s_new = s_cat[qh] * st[qh]["dec_cat"] + _dot_tn(st[qh]["k_dec"], v_blk)
        s_ref[h0] = s_new[:, :dk]
        s_ref[h1] = s_new[:, dk:]
        o = outs[qh]
        o = o * lax.rsqrt(jnp.mean(o * o, axis=-1, keepdims=True) + RMS_EPS) * nw
        o = (o * _silu(jnp.concatenate([z_ref[:, cols(h0)], z_ref[:, cols(h1)]], axis=0))).astype(o_ref.dtype)
        o_ref[:, cols(h0)] = o[:c]
        o_ref[:, cols(h1)] = o[c:]


def gdn_scan(qkv_c, z_arr, z_col0, ba, alog_pad, dtb_pad, norm_w, s0, *, nseq, seq, t_valid, spare_rows=0):
    c = B_CHUNK
    hps = B_HEADS_PER_STEP
    nc = seq // c
    ng = B_V_HEADS // hps
    qw = hps // 2 * B_HEAD_DIM
    vw = hps * B_HEAD_DIM
    par_rows = alog_pad.shape[0]
    row = lambda s, h, ci: s * nc + ci
    return pl.pallas_call(
        functools.partial(_gdn_kernel, c=c, hps=hps, t_valid=t_valid),
        grid=(nseq, ng, nc),
        in_specs=[pl.BlockSpec((c, qw), lambda s, h, ci: (row(s, h, ci), h)),
                  pl.BlockSpec((c, qw), lambda s, h, ci: (row(s, h, ci), B_KEY_DIM // qw + h)),
                  pl.BlockSpec((c, vw), lambda s, h, ci: (row(s, h, ci), 2 * B_KEY_DIM // vw + h)),
                  pl.BlockSpec((c, vw), lambda s, h, ci: (row(s, h, ci), z_col0 // vw + h)),
                  pl.BlockSpec((c, LANES), lambda s, h, ci: (row(s, h, ci), h)),
                  pl.BlockSpec((par_rows, LANES), lambda s, h, ci: (0, 0)),
                  pl.BlockSpec((par_rows, LANES), lambda s, h, ci: (0, 0)),
                  pl.BlockSpec((1, B_HEAD_DIM), lambda s, h, ci: (0, 0)),
                  pl.BlockSpec((None, hps, B_HEAD_DIM, B_HEAD_DIM), lambda s, h, ci: (s, h, 0, 0))],
        out_specs=[pl.BlockSpec((c, vw), lambda s, h, ci: (row(s, h, ci), h)),
                   pl.BlockSpec((None, hps, B_HEAD_DIM, B_HEAD_DIM), lambda s, h, ci: (s, h, 0, 0))],
        out_shape=[jax.ShapeDtypeStruct((nseq * seq + spare_rows, B_VAL_DIM), BF16),
                   jax.ShapeDtypeStruct((nseq, B_V_HEADS, B_HEAD_DIM, B_HEAD_DIM), F32)],
        compiler_params=_params(3),
        name="gdn_scan",
    )(qkv_c, qkv_c, qkv_c, z_arr, ba, alog_pad, dtb_pad, norm_w, s0)


def _ssd_kernel(x_ref, b_ref, c_ref, z_ref, dt_ref, dtb_ref, alog_ref, dskip_ref, nw_ref, h0_ref,
                y_ref, hout_ref, ht_ref, *, c, nc, t_valid):
    g = pl.program_id(1)
    ci = pl.program_id(2)
    hpg = C_HEADS_PER_GROUP
    p = C_HEAD_DIM
    gcols = hpg * p

    @pl.when(ci == 0)
    def _():
        ht_ref[...] = h0_ref[...].reshape(gcols, C_D_STATE).T

    row = lax.broadcasted_iota(jnp.int32, (c, LANES), 0) + ci * c
    dt = jnp.where(row < t_valid, _softplus(dt_ref[...] + dtb_ref[pl.ds(g, 1), :]), 0.0)
    la = dt * -jnp.exp(alog_ref[pl.ds(g, 1), :])
    ii = lax.broadcasted_iota(jnp.int32, (c, c), 0)
    jj = lax.broadcasted_iota(jnp.int32, (c, c), 1)
    incl = ii >= jj
    acs = _select_rows(jnp.where(incl, 1.0, 0.0).astype(BF16), la)
    acs_t = acs.T
    el = lax.broadcasted_iota(jnp.int32, (LANES, gcols), 0)
    ej = lax.broadcasted_iota(jnp.int32, (LANES, gcols), 1)
    expand = jnp.where(el == (ej >> (p.bit_length() - 1)), 1.0, 0.0).astype(BF16)
    dt_e = _select_cols(dt, expand)
    acs_e = _select_cols(acs, expand)
    dskip_e = _select_cols(jnp.broadcast_to(dskip_ref[pl.ds(g, 1), :], (SUBLANES, LANES)), expand)[0:1, :]

    x = x_ref[...]
    bm = b_ref[...].astype(BF16)
    cm = c_ref[...].astype(BF16)
    xdt = (x * dt_e).astype(BF16)
    cb = _dot_nt(cm, bm)
    ht = ht_ref[...]
    y = _dot(cm, ht.astype(BF16)) * jnp.exp(acs_e) + dskip_e * x
    lane = lax.broadcasted_iota(jnp.int32, (c, 2 * p), 1)
    decay = [jnp.exp(jnp.where(incl, acs[:, hd:hd + 1] - acs_t[hd:hd + 1, :], NEG_BIG)) for hd in range(hpg)]
    mats = [(cb * dec).astype(BF16) for dec in decay]
    ys = [_dot(mats[hd], xdt[:, (hd // 2) * 2 * p:(hd // 2 + 1) * 2 * p]) for hd in range(hpg)]
    y = y + jnp.concatenate([jnp.where(lane < p, ys[2 * j], ys[2 * j + 1]) for j in range(hpg // 2)], axis=1)
    y = y * _silu(z_ref[...])
    y = y * lax.rsqrt(jnp.mean(y * y, axis=-1, keepdims=True) + RMS_EPS) * nw_ref[...]
    y_ref[...] = y.astype(y_ref.dtype)

    last_e = acs_e[c - 1:c, :]
    xw = (x * (jnp.exp(last_e - acs_e) * dt_e)).astype(BF16)
    ht_new = ht * jnp.exp(last_e) + _dot_tn(bm, xw)
    ht_ref[...] = ht_new

    @pl.when(ci == nc - 1)
    def _():
        hout_ref[...] = ht_new.T.reshape(hpg, p, C_D_STATE)


def ssd_scan(xc, z_arr, dt, dtb_pad, alog_pad, dskip_pad, norm_w, h0, *, nseq, seq, t_valid, spare_rows=0):
    c = C_CHUNK
    nc = seq // c
    gc = C_GROUP_COLS
    b0 = C_D_INNER // C_D_STATE
    row = lambda s, g, ci: s * nc + ci
    par = pl.BlockSpec((C_GROUPS, LANES), lambda s, g, ci: (0, 0))
    return pl.pallas_call(
        functools.partial(_ssd_kernel, c=c, nc=nc, t_valid=t_valid),
        grid=(nseq, C_GROUPS, nc),
        in_specs=[pl.BlockSpec((c, gc), lambda s, g, ci: (row(s, g, ci), g)),
                  pl.BlockSpec((c, C_D_STATE), lambda s, g, ci: (row(s, g, ci), b0 + g)),
                  pl.BlockSpec((c, C_D_STATE), lambda s, g, ci: (row(s, g, ci), b0 + C_GROUPS + g)),
                  pl.BlockSpec((c, gc), lambda s, g, ci: (row(s, g, ci), g)),
                  pl.BlockSpec((c, LANES), lambda s, g, ci: (row(s, g, ci), g)),
                  par, par, par,
                  pl.BlockSpec((1, gc), lambda s, g, ci: (0, g)),
                  pl.BlockSpec((None, C_HEADS_PER_GROUP, C_HEAD_DIM, C_D_STATE), lambda s, g, ci: (s, g, 0, 0))],
        out_specs=[pl.BlockSpec((c, gc), lambda s, g, ci: (row(s, g, ci), g)),
                   pl.BlockSpec((None, C_HEADS_PER_GROUP, C_HEAD_DIM, C_D_STATE), lambda s, g, ci: (s, g, 0, 0))],
        out_shape=[jax.ShapeDtypeStruct((nseq * seq + spare_rows, C_D_INNER), BF16),
                   jax.ShapeDtypeStruct((nseq, C_HEADS, C_HEAD_DIM, C_D_STATE), F32)],
        scratch_shapes=[pltpu.VMEM((C_D_STATE, gc), F32)],
        compiler_params=_params(3),
        name="ssd_scan",
    )(xc, xc, xc, z_arr, dt, dtb_pad, alog_pad, dskip_pad, norm_w, h0)


def _rope_tables(mp, seq, n_sample, ts):
    pos = jnp.concatenate([jnp.arange(mp) % seq, PAST_LEN + jnp.arange(n_sample) % ts]).astype(F32)
    inv_freq = ROPE_THETA ** (-jnp.arange(A_ROT_HALF, dtype=F32) / A_ROT_HALF)
    ang = pos[:, None] * inv_freq[None, :]
    cos, sin = jnp.cos(ang), jnp.sin(ang)
    rest = A_HEAD_DIM - 2 * A_ROT_HALF
    rows = pos.shape[0]
    cos_full = jnp.concatenate([cos, cos, jnp.ones((rows, rest), F32)], axis=1)
    sin_full = jnp.concatenate([-sin, sin, jnp.zeros((rows, rest), F32)], axis=1)
    return cos_full, sin_full


def _lane_groups(cols, group):
    lead = cols.shape[:-1]
    n = cols.shape[-1]
    c = cols.reshape(*lead, n // group, group)
    c = jnp.pad(c, [(0, 0)] * (c.ndim - 1) + [(0, LANES - group)])
    return c.reshape(*lead, n // group * LANES)


def _pad_seq_rows(x, nseq, ts, seq):
    cols = x.shape[-1]
    x = jnp.pad(x.reshape(nseq, ts, cols), ((0, 0), (0, seq - ts), (0, 0)))
    return x.reshape(nseq * seq, cols)


def _conv_prev(state):
    return jnp.pad(state, ((0, 0), (SUBLANES - state.shape[1], 0), (0, 0)))


def _conv_taps(w):
    return jnp.pad(w, ((0, SUBLANES - w.shape[0]), (0, 0)))


def _last_rows(x, nseq, seq, n, col0, cols):
    return jnp.stack([x[(s + 1) * seq - n:(s + 1) * seq, col0:col0 + cols] for s in range(nseq)])


def kernel(x_prompt, x_sample, cache_a_kv_w128, cache_a_kv_w512, cache_a_kv_w2048, state_b_ssm, state_b_conv, state_c_ssm, state_c_conv, ln_g, ln_b, ffn1_w_gu, ffn1_w_down, ffn2_w_gu, ffn2_w_down, a_w_qkv, a_w_o, b_w_in, b_conv_w, b_a_log, b_dt_bias, b_norm_w, b_w_out, c_w_in, c_conv_w, c_conv_b, c_dt_bias, c_a_log, c_d, c_norm_w, c_w_out):
    a_bufs = (cache_a_kv_w128, cache_a_kv_w512, cache_a_kv_w2048)
    bp, seq, d = x_prompt.shape
    db, ts, _ = x_sample.shape
    mp, ms = bp * seq, db * ts
    assert seq % (A_BLOCK * max(A_DILATIONS)) == 0 and mp % ms == 0
    x = jnp.concatenate([x_prompt.reshape(mp, d), x_sample.reshape(ms, d)], axis=0)
    xb = x.astype(BF16)
    cos_full, sin_full = _rope_tables(mp, seq, ms, ts)
    wb16 = {name: w.astype(BF16) for name, w in dict(
        ffn1_down=ffn1_w_down, ffn2_down=ffn2_w_down, a_qkv=a_w_qkv, a_o=a_w_o,
        b_in=b_w_in[:, :, :B_CONV_DIM + B_VAL_DIM], b_out=b_w_out,
        c_in=c_w_in[:, :, :C_D_INNER + C_CONV_DIM], c_out=c_w_out).items()}
    w_gu = dict(ffn1=ffn1_w_gu, ffn2=ffn2_w_gu)

    def with_sample_rows(a, a_s, rows):
        return lax.dynamic_update_slice(a, a_s.reshape(db, rows, -1)[:, :ts].reshape(ms, -1), (mp, 0))

    outs = {}

    def mixer_a(xb, j):
        qkv_groups = qkv_project(xb, wb16["a_qkv"], cos_full, sin_full, layer=j, batch=bp, seq=seq)
        qkv_s = matmul(xb[mp:], wb16["a_qkv"], layer=j, n=a_w_qkv.shape[2], tn=1024, tm=ms)
        gw = A_GROUP_COLS
        o_p, l_p, o_s, l_s = [], [], [], []
        for g, dil in enumerate(A_DILATIONS):
            qkv_r = qkv_groups[g]
            o, lse = attn_prompt(qkv_r)
            o_p.append(o)
            l_p.append(lse)
            new = qkv_split(qkv_s, cos_full[mp:], sin_full[mp:], g=g, dil=1, row0=0, batch=1, seq=ms, tile=ms)
            q_s, k_s, v_s = (new[0, 0, :, sec * gw:(sec + 1) * gw].reshape(db, ts, A_HEADS, A_HEAD_DIM)
                             for sec in range(A_SECTIONS))
            o, lse = attn_sample(q_s, k_s, v_s, a_bufs[g], layer=j, dil=dil)
            o_s.append(o)
            l_s.append(lse)
            length = seq // dil
            tail = qkv_r[:, :, length - A_BLOCK:, gw:].transpose(0, 2, 1, 3)
            outs.setdefault("a_p%d" % g, []).append(tail.reshape(bp, A_BLOCK * dil, 2, A_HEADS, A_HEAD_DIM))
            outs.setdefault("a_new%d" % g, []).append(jnp.stack([k_s, v_s], axis=2))
        a = attn_merge_prompt(o_p, l_p, spare_rows=ms)
        a_s = attn_merge_rows(o_s, l_s).reshape(ms, gw).astype(BF16)
        return lax.dynamic_update_slice(a, a_s, (mp, 0)), wb16["a_o"]

    def mixer_b(xb, j):
        hps = B_HEADS_PER_STEP
        ng = B_V_HEADS // hps
        w_in = b_w_in[j]
        main = B_CONV_DIM + B_VAL_DIM
        pb = matmul(xb, wb16["b_in"], layer=j, n=main, tn=1024)
        w_tail = jnp.concatenate([w_in[:, main:main + B_V_HEADS].reshape(d, ng, hps),
                                  w_in[:, main + B_V_HEADS:].reshape(d, ng, hps)], axis=2)
        ba = matmul(xb, _lane_groups(w_tail.reshape(d, -1), 2 * hps).astype(BF16)[None],
                    layer=0, n=ng * LANES, tn=ng * LANES)
        gate_lanes = lambda v: jnp.pad(_lane_groups(
            jnp.concatenate([jnp.zeros((ng, hps), F32), v.reshape(ng, hps)], axis=1).reshape(-1), 2 * hps
        ).reshape(ng, LANES), ((0, SUBLANES - ng), (0, 0)))
        alog_pad, dtb_pad = gate_lanes(b_a_log[j]), gate_lanes(b_dt_bias[j])
        taps = _conv_taps(b_conv_w[j])
        no_bias = jnp.zeros((1, B_CONV_DIM), F32)
        nw = b_norm_w[j].reshape(1, B_HEAD_DIM)
        qkv_c = conv_silu(pb, jnp.zeros((bp, SUBLANES, B_CONV_DIM), F32), taps, no_bias,
                          nseq=bp, seq=seq, col0=0, cols=B_CONV_DIM, tt=512)
        a_p, s_p = gdn_scan(qkv_c, pb, B_CONV_DIM, ba, alog_pad, dtb_pad, nw,
                            jnp.zeros((bp, B_V_HEADS, B_HEAD_DIM, B_HEAD_DIM), F32),
                            nseq=bp, seq=seq, t_valid=seq, spare_rows=ms)
        pb_s = _pad_seq_rows(pb[mp:], db, ts, B_CHUNK)
        ba_s = _pad_seq_rows(ba[mp:], db, ts, B_CHUNK)
        qkv_cs = conv_silu(pb_s, _conv_prev(state_b_conv[j]), taps, no_bias,
                           nseq=db, seq=B_CHUNK, col0=0, cols=B_CONV_DIM, tt=B_CHUNK)
        a_s, s_s = gdn_scan(qkv_cs, pb_s, B_CONV_DIM, ba_s, alog_pad, dtb_pad, nw,
                            state_b_ssm[j], nseq=db, seq=B_CHUNK, t_valid=ts)
        outs.setdefault("b_ssm_p", []).append(s_p)
        outs.setdefault("b_ssm_s", []).append(s_s)
        outs.setdefault("b_conv_p", []).append(_last_rows(pb, bp, seq, CONV_TAPS - 1, 0, B_CONV_DIM))
        outs.setdefault("b_conv_s", []).append(_last_rows(pb[mp:], db, ts, CONV_TAPS - 1, 0, B_CONV_DIM))
        return with_sample_rows(a_p, a_s, B_CHUNK), wb16["b_out"]

    def mixer_c(xb, j):
        w_in = c_w_in[j]
        main = C_D_INNER + C_CONV_DIM
        pc = matmul(xb, wb16["c_in"], layer=j, n=main, tn=1024)
        dt = matmul(xb, _lane_groups(w_in[:, main:], C_HEADS_PER_GROUP).astype(BF16)[None],
                    layer=0, n=C_GROUPS * LANES, tn=C_GROUPS * LANES)
        head_lanes = lambda v: _lane_groups(v, C_HEADS_PER_GROUP).reshape(C_GROUPS, LANES)
        dtb_pad, alog_pad, dskip_pad = head_lanes(c_dt_bias[j]), head_lanes(c_a_log[j]), head_lanes(c_d[j])
        taps = _conv_taps(c_conv_w[j])
        bias = c_conv_b[j].reshape(1, C_CONV_DIM)
        nw = c_norm_w[j].reshape(1, C_D_INNER)
        xc = conv_silu(pc, jnp.zeros((bp, SUBLANES, C_CONV_DIM), F32), taps, bias,
                       nseq=bp, seq=seq, col0=C_D_INNER, cols=C_CONV_DIM, tt=512)
        a_p, h_p = ssd_scan(xc, pc, dt, dtb_pad, alog_pad, dskip_pad, nw,
                            jnp.zeros((bp, C_HEADS, C_HEAD_DIM, C_D_STATE), F32),
                            nseq=bp, seq=seq, t_valid=seq, spare_rows=ms)
        pc_s = _pad_seq_rows(pc[mp:], db, ts, C_CHUNK)
        dt_s = _pad_seq_rows(dt[mp:], db, ts, C_CHUNK)
        xc_s = conv_silu(pc_s, _conv_prev(state_c_conv[j]), taps, bias,
                         nseq=db, seq=C_CHUNK, col0=C_D_INNER, cols=C_CONV_DIM, tt=C_CHUNK)
        a_s, h_s = ssd_scan(xc_s, pc_s, dt_s, dtb_pad, alog_pad, dskip_pad, nw, state_c_ssm[j],
                            nseq=db, seq=C_CHUNK, t_valid=ts)
        outs.setdefault("c_ssm_p", []).append(h_p)
        outs.setdefault("c_ssm_s", []).append(h_s)
        outs.setdefault("c_conv_p", []).append(_last_rows(pc, bp, seq, CONV_TAPS - 1, C_D_INNER, C_CONV_DIM))
        outs.setdefault("c_conv_s", []).append(_last_rows(pc[mp:], db, ts, CONV_TAPS - 1, C_D_INNER, C_CONV_DIM))
        return with_sample_rows(a_p, a_s, C_CHUNK), wb16["c_out"]

    def ffn(x, xb, which, i, g, b):
        h = swiglu_up(xb, w_gu[which], layer=i)
        return matmul_postnorm(h, wb16[which + "_down"], x, g, b, layer=i, scale=0.5)

    mixers = (mixer_a, mixer_b, mixer_c)
    for i in range(DEPTH):
        x, xb = ffn(x, xb, "ffn1", i, ln_g[i, 0], ln_b[i, 0])
        a, w_out = mixers[i % 3](xb, i // 3)
        x, xb = matmul_postnorm(a, w_out, x, ln_g[i, 1], ln_b[i, 1], layer=i // 3, scale=1.0)
        x, xb = ffn(x, xb, "ffn2", i, ln_g[i, 2], ln_b[i, 2])

    st = lambda name: jnp.stack(outs[name])
    def shifted(buf, new):
        moved = lax.pad(buf, jnp.zeros((), buf.dtype), [(0, 0, 0), (0, 0, 0), (-ts, ts, 0)] + [(0, 0, 0)] * 3)
        return lax.dynamic_update_slice(moved, new, (0, 0, buf.shape[2] - ts, 0, 0, 0))

    a_s = [shifted(a_bufs[g], st("a_new%d" % g)) for g in range(A_GROUPS)]
    return (x[:mp].reshape(bp, seq, d), x[mp:].reshape(db, ts, d),
            st("a_p0"), a_s[0], st("a_p1"), a_s[1], st("a_p2"), a_s[2],
            st("b_ssm_p"), st("b_ssm_s"), st("b_conv_p"), st("b_conv_s"),
            st("c_ssm_p"), st("c_ssm_s"), st("c_conv_p"), st("c_conv_s"))
```

```python
import functools

import jax
import jax.numpy as jnp
from jax import lax
from jax.experimental import pallas as pl
from jax.experimental.pallas import tpu as pltpu

F32 = jnp.float32
BF16 = jnp.bfloat16

VMEM_LIMIT_BYTES = 56 * 1024 * 1024
LANES = 128
SUBLANES = 8
ROW_TILE = 512
POSTNORM_ROW_TILE = 256
NEG_BIG = -1e30

DEPTH = 4
DN_ALPHA = (2.0 * DEPTH) ** 0.25
LN_EPS = 1e-5
RMS_EPS = 1e-6
PAST_LEN = 16384

A_DILATIONS = (1, 4, 16)
A_GROUPS = 3
A_HEADS = 8
A_HEAD_DIM = 128
A_ROT_HALF = A_HEAD_DIM // 8
A_BLOCK = 128
A_GROUP_COLS = A_HEADS * A_HEAD_DIM
A_SECTIONS = 3
A_SPLIT_TILE = 2048
ROPE_THETA = 500000.0

B_QK_HEADS = 16
B_V_HEADS = 32
B_HEAD_DIM = 128
B_KEY_DIM = B_QK_HEADS * B_HEAD_DIM
B_VAL_DIM = B_V_HEADS * B_HEAD_DIM
B_CONV_DIM = 2 * B_KEY_DIM + B_VAL_DIM
B_CHUNK = 64
B_HEADS_PER_STEP = 16
B_INV_BLOCK = 16
B_INV_PASSES = 1

C_D_INNER = 4096
C_HEADS = 64
C_HEAD_DIM = 64
C_GROUPS = 8
C_HEADS_PER_GROUP = C_HEADS // C_GROUPS
C_GROUP_COLS = C_D_INNER // C_GROUPS
C_D_STATE = 128
C_CONV_DIM = C_D_INNER + 2 * C_GROUPS * C_D_STATE
C_CHUNK = 128

CONV_TAPS = 4


def _params(n_axes):
    return pltpu.CompilerParams(
        dimension_semantics=("arbitrary",) * n_axes,
        vmem_limit_bytes=VMEM_LIMIT_BYTES,
    )


def _dot(a, b):
    return jnp.dot(a, b, preferred_element_type=F32)


def _dot_nt(a, b):
    return lax.dot_general(a, b, (((1,), (1,)), ((), ())), preferred_element_type=F32)


def _dot_tn(a, b):
    return lax.dot_general(a, b, (((0,), (0,)), ((), ())), preferred_element_type=F32)


def _split3(x):
    hi = x.astype(BF16)
    r = x - hi.astype(F32)
    mid = r.astype(BF16)
    lo = (r - mid.astype(F32)).astype(BF16)
    return hi, mid, lo


def _select_rows(sel, x):
    return sum(_dot(sel, part) for part in _split3(x))


def _select_cols(x, sel):
    return sum(_dot(part, sel) for part in _split3(x))


def _mm(a, b, passes):
    a_hi, b_hi = a.astype(BF16), b.astype(BF16)
    out = _dot(a_hi, b_hi)
    if passes == 3:
        a_lo = (a - a_hi.astype(F32)).astype(BF16)
        b_lo = (b - b_hi.astype(F32)).astype(BF16)
        out = out + _dot(a_hi, b_lo) + _dot(a_lo, b_hi)
    return out


def _silu(x):
    return x * jax.nn.sigmoid(x)


def _softplus(x):
    return jnp.maximum(x, 0.0) + jnp.log(1.0 + jnp.exp(-jnp.abs(x)))


def _row_tiles(rows, tm):
    n_full, rem = divmod(rows, tm)
    return n_full + (1 if rem else 0), n_full, rem


def _per_row_tile(i, n_full, rem, tm, body):
    if rem == 0:
        body(tm)
        return

    @pl.when(i < n_full)
    def _():
        body(tm)

    @pl.when(i == n_full)
    def _():
        body(rem)


def _mm_kernel(x_ref, w_ref, o_ref, *, tm, n_full, rem):
    def body(r):
        o_ref[:r, :] = _dot(x_ref[:r, :], w_ref[...]).astype(o_ref.dtype)

    _per_row_tile(pl.program_id(0), n_full, rem, tm, body)


def matmul(x, w, *, layer, n, tn, out_dtype=F32, tm=2 * ROW_TILE):
    rows, k = x.shape
    steps, n_full, rem = _row_tiles(rows, tm)
    return pl.pallas_call(
        functools.partial(_mm_kernel, tm=tm, n_full=n_full, rem=rem),
        grid=(steps, n // tn),
        in_specs=[pl.BlockSpec((tm, k), lambda i, j: (i, 0)),
                  pl.BlockSpec((None, k, tn), lambda i, j: (layer, 0, j))],
        out_specs=pl.BlockSpec((tm, tn), lambda i, j: (i, j)),
        out_shape=jax.ShapeDtypeStruct((rows, n), out_dtype),
        compiler_params=_params(2),
        name="matmul",
    )(x, w)


def _swiglu_kernel(x_ref, wg_ref, wu_ref, o_ref, *, tm, n_full, rem):
    def body(r):
        x = x_ref[:r, :]
        gate = _dot(x, wg_ref[...].astype(BF16))
        up = _dot(x, wu_ref[...].astype(BF16))
        o_ref[:r, :] = (_silu(gate) * up).astype(o_ref.dtype)

    _per_row_tile(pl.program_id(0), n_full, rem, tm, body)


def swiglu_up(x, w_gu, *, layer, tn=512, tm=2 * ROW_TILE):
    rows, k = x.shape
    f = w_gu.shape[2] // 2
    steps, n_full, rem = _row_tiles(rows, tm)
    nj = f // tn
    return pl.pallas_call(
        functools.partial(_swiglu_kernel, tm=tm, n_full=n_full, rem=rem),
        grid=(steps, nj),
        in_specs=[pl.BlockSpec((tm, k), lambda i, j: (i, 0)),
                  pl.BlockSpec((None, k, tn), lambda i, j: (layer, 0, j)),
                  pl.BlockSpec((None, k, tn), lambda i, j: (layer, 0, j + nj))],
        out_specs=pl.BlockSpec((tm, tn), lambda i, j: (i, j)),
        out_shape=jax.ShapeDtypeStruct((rows, f), BF16),
        compiler_params=_params(2),
        name="swiglu_up",
    )(x, w_gu, w_gu)


def _postnorm_kernel(a_ref, w_ref, x_ref, g_ref, b_ref, y_ref, yb_ref, *, scale, tm, n_full, rem):
    def body(r):
        y = DN_ALPHA * x_ref[:r, :] + scale * _dot(a_ref[:r, :], w_ref[...])
        mu = jnp.mean(y, axis=-1, keepdims=True)
        yc = y - mu
        var = jnp.mean(yc * yc, axis=-1, keepdims=True)
        out = yc * lax.rsqrt(var + LN_EPS) * g_ref[...] + b_ref[...]
        y_ref[:r, :] = out
        yb_ref[:r, :] = out.astype(BF16)

    _per_row_tile(pl.program_id(0), n_full, rem, tm, body)


def matmul_postnorm(a, w, x, g, b, *, layer, scale, tm=POSTNORM_ROW_TILE):
    rows, kdim = a.shape
    d = w.shape[2]
    steps, n_full, rem = _row_tiles(rows, tm)
    return pl.pallas_call(
        functools.partial(_postnorm_kernel, scale=scale, tm=tm, n_full=n_full, rem=rem),
        grid=(steps,),
        in_specs=[pl.BlockSpec((tm, kdim), lambda i: (i, 0)),
                  pl.BlockSpec((None, kdim, d), lambda i: (layer, 0, 0), pipeline_mode=pl.Buffered(1)),
                  pl.BlockSpec((tm, d), lambda i: (i, 0)),
                  pl.BlockSpec((1, d), lambda i: (0, 0)),
                  pl.BlockSpec((1, d), lambda i: (0, 0))],
        out_specs=[pl.BlockSpec((tm, d), lambda i: (i, 0)),
                   pl.BlockSpec((tm, d), lambda i: (i, 0))],
        out_shape=[jax.ShapeDtypeStruct((rows, d), F32),
                   jax.ShapeDtypeStruct((rows, d), BF16)],
        compiler_params=_params(1),
        name="matmul_postnorm",
    )(a, w, x, g.reshape(1, d), b.reshape(1, d))


def _strided_rows(r, n, stride):
    return pl.ds(r, n, stride=stride) if stride > 1 else pl.ds(0, n)


def _qkv_split_kernel(x_ref, cos_ref, sin_ref, o_ref, *, dil, n):
    rotated = pl.program_id(2) < (A_SECTIONS - 1) * A_HEADS
    lane = lax.broadcasted_iota(jnp.int32, (n, A_HEAD_DIM), 1)

    def emit(rotate):
        for r in range(dil):
            rows = _strided_rows(r, n, dil)
            x = x_ref[rows, :]
            if rotate:
                partner = jnp.where(lane < A_ROT_HALF,
                                    pltpu.roll(x, A_HEAD_DIM - A_ROT_HALF, axis=1),
                                    pltpu.roll(x, A_ROT_HALF, axis=1))
                x = x * cos_ref[rows, :] + partner * sin_ref[rows, :]
            o_ref[r] = x

    @pl.when(rotated)
    def _():
        emit(True)

    @pl.when(jnp.logical_not(rotated))
    def _():
        emit(False)


def qkv_split(qkv, cos_full, sin_full, *, g, dil, row0, batch, seq, tile):
    n = tile // dil
    tps = seq // tile
    r0 = row0 // tile
    in_col = lambda c: (c // A_HEADS * A_GROUPS + g) * A_HEADS + c % A_HEADS
    return pl.pallas_call(
        functools.partial(_qkv_split_kernel, dil=dil, n=n),
        grid=(batch, tps, A_SECTIONS * A_HEADS),
        in_specs=[pl.BlockSpec((tile, A_HEAD_DIM), lambda b, i, c: (r0 + b * tps + i, in_col(c))),
                  pl.BlockSpec((tile, A_HEAD_DIM), lambda b, i, c: (r0 + b * tps + i, 0)),
                  pl.BlockSpec((tile, A_HEAD_DIM), lambda b, i, c: (r0 + b * tps + i, 0))],
        out_specs=pl.BlockSpec((None, dil, n, A_HEAD_DIM), lambda b, i, c: (b, 0, i, c)),
        out_shape=jax.ShapeDtypeStruct((batch, dil, seq // dil, A_SECTIONS * A_GROUP_COLS), F32),
        compiler_params=_params(3),
        name="qkv_split",
    )(qkv, cos_full, sin_full)


def _rotate_pairs(x, cos, sin, lane):
    partner = jnp.where(lane < A_ROT_HALF,
                        pltpu.roll(x, A_HEAD_DIM - A_ROT_HALF, axis=1),
                        pltpu.roll(x, A_ROT_HALF, axis=1))
    return x * cos + partner * sin


def _qkv_project_kernel(x_ref, w_ref, cos_ref, sin_ref, *refs, tile):
    o_refs, head_ref = refs[:A_GROUPS], refs[A_GROUPS]
    j = pl.program_id(2)
    sec = j // A_GROUPS
    grp = j - sec * A_GROUPS
    res = _dot(x_ref[...], w_ref[...])
    cols = [slice(h * A_HEAD_DIM, (h + 1) * A_HEAD_DIM) for h in range(A_HEADS)]

    @pl.when(sec < A_SECTIONS - 1)
    def _():
        lane = lax.broadcasted_iota(jnp.int32, (tile, A_HEAD_DIM), 1)
        cos, sin = cos_ref[...], sin_ref[...]
        for h in range(A_HEADS):
            head_ref[h] = _rotate_pairs(res[:, cols[h]], cos, sin, lane)

    @pl.when(sec == A_SECTIONS - 1)
    def _():
        for h in range(A_HEADS):
            head_ref[h] = res[:, cols[h]]

    for g, dil in enumerate(A_DILATIONS):
        @pl.when(grp == g)
        def _(g=g, dil=dil):
            for r in range(dil):
                rows = _strided_rows(r, tile // dil, dil)
                for h in range(A_HEADS):
                    o_refs[g][r, :, cols[h]] = head_ref[h, rows, :]


def qkv_project(xb, w_qkv, cos_full, sin_full, *, layer, batch, seq, tile=ROW_TILE):
    d = xb.shape[1]
    tps = seq // tile
    out_specs, out_shapes = [], []
    for g, dil in enumerate(A_DILATIONS):
        out_specs.append(pl.BlockSpec(
            (None, dil, tile // dil, A_GROUP_COLS),
            lambda b, i, j, g=g: (b, 0, i, jnp.clip((j - g) // A_GROUPS, 0, A_SECTIONS - 1))))
        out_shapes.append(jax.ShapeDtypeStruct((batch, dil, seq // dil, A_SECTIONS * A_GROUP_COLS), F32))
    row_spec = lambda width: pl.BlockSpec((tile, width), lambda b, i, j: (b * tps + i, 0))
    return pl.pallas_call(
        functools.partial(_qkv_project_kernel, tile=tile),
        grid=(batch, tps, A_SECTIONS * A_GROUPS),
        in_specs=[row_spec(d),
                  pl.BlockSpec((None, d, A_GROUP_COLS), lambda b, i, j: (layer, 0, j)),
                  row_spec(A_HEAD_DIM), row_spec(A_HEAD_DIM)],
        out_specs=out_specs,
        out_shape=out_shapes,
        scratch_shapes=[pltpu.VMEM((A_HEADS, tile, A_HEAD_DIM), F32)],
        compiler_params=_params(3),
        name="qkv_project",
    )(xb, w_qkv, cos_full, sin_full)


def _attn_prompt_kernel(q_ref, kp_ref, kc_ref, vp_ref, vc_ref, o_ref, lse_ref):
    lb = pl.program_id(2)
    n = A_BLOCK
    qi = lax.broadcasted_iota(jnp.int32, (n, n), 0)
    kj = lax.broadcasted_iota(jnp.int32, (n, n), 1)
    mask_cur = kj <= qi
    mask_prev = jnp.logical_and(kj >= qi, lb > 0)
    scale = A_HEAD_DIM ** -0.5
    heads = range(A_HEADS)
    cols = [slice(h * A_HEAD_DIM, (h + 1) * A_HEAD_DIM) for h in heads]
    q = [q_ref[:, sl].astype(BF16) for sl in cols]
    s_c = [jnp.where(mask_cur, _dot_nt(q[h], kc_ref[:, cols[h]].astype(BF16)) * scale, NEG_BIG) for h in heads]
    s_p = [jnp.where(mask_prev, _dot_nt(q[h], kp_ref[:, cols[h]].astype(BF16)) * scale, NEG_BIG) for h in heads]
    m = [jnp.maximum(jnp.max(s_c[h], axis=-1, keepdims=True), jnp.max(s_p[h], axis=-1, keepdims=True)) for h in heads]
    p_c = [jnp.exp(s_c[h] - m[h]) for h in heads]
    p_p = [jnp.exp(s_p[h] - m[h]) for h in heads]
    l = [jnp.sum(p_c[h], axis=-1, keepdims=True) + jnp.sum(p_p[h], axis=-1, keepdims=True) for h in heads]
    o = [_dot(p_c[h].astype(BF16), vc_ref[:, cols[h]].astype(BF16))
         + _dot(p_p[h].astype(BF16), vp_ref[:, cols[h]].astype(BF16)) for h in heads]
    lse_all = jnp.zeros((n, LANES), F32)
    for h in heads:
        o_ref[:, cols[h]] = o[h] / l[h]
        lse_all = jnp.where(kj == h, m[h] + jnp.log(l[h]), lse_all)
    lse_ref[...] = lse_all


def attn_prompt(qkv_r):
    batch, dil, length, _ = qkv_r.shape
    nb = length // A_BLOCK
    spec = lambda sec, back: pl.BlockSpec(
        (None, None, A_BLOCK, A_GROUP_COLS), lambda b, r, lb: (b, r, jnp.maximum(lb - back, 0), sec))
    return pl.pallas_call(
        _attn_prompt_kernel,
        grid=(batch, dil, nb),
        in_specs=[spec(0, 0), spec(1, 1), spec(1, 0), spec(2, 1), spec(2, 0)],
        out_specs=[pl.BlockSpec((None, None, A_BLOCK, A_GROUP_COLS), lambda b, r, lb: (b, r, lb, 0)),
                   pl.BlockSpec((None, None, A_BLOCK, LANES), lambda b, r, lb: (b, r, lb, 0))],
        out_shape=[jax.ShapeDtypeStruct((batch, dil, length, A_GROUP_COLS), F32),
                   jax.ShapeDtypeStruct((batch, dil, length, LANES), F32)],
        compiler_params=_params(3),
        name="attn_prompt",
    )(qkv_r, qkv_r, qkv_r, qkv_r, qkv_r)


def _attn_sample_kernel(q_ref, kn_ref, vn_ref, *refs, dil, ts):
    n_res = len(refs) // 2 - 1
    cache_refs, (o_ref, lse_ref) = refs[:2 * n_res], refs[2 * n_res:]
    scale = A_HEAD_DIM ** -0.5
    jc = lax.broadcasted_iota(jnp.int32, (A_BLOCK, 1, 1), 0)
    un = lax.broadcasted_iota(jnp.int32, (ts, 1, 1), 0)
    kn = kn_ref[...]
    vn = vn_ref[...]
    for t in range(ts):
        kc_ref, vc_ref = cache_refs[2 * (t % n_res)], cache_refs[2 * (t % n_res) + 1]
        valid_n = (un <= t) if dil == 1 else (un == t)
        q = q_ref[t:t + 1]
        s_c = jnp.sum(kc_ref[...] * q, axis=-1, keepdims=True) * scale
        if dil == 1:
            s_c = jnp.where(jc >= t, s_c, NEG_BIG)
        s_n = jnp.where(valid_n, jnp.sum(kn * q, axis=-1, keepdims=True) * scale, NEG_BIG)
        m = jnp.maximum(jnp.max(s_c, axis=0, keepdims=True), jnp.max(s_n, axis=0, keepdims=True))
        p_c = jnp.exp(s_c - m)
        p_n = jnp.exp(s_n - m)
        l = jnp.sum(p_c, axis=0, keepdims=True) + jnp.sum(p_n, axis=0, keepdims=True)
        o = jnp.sum(p_c * vc_ref[...], axis=0, keepdims=True) + jnp.sum(p_n * vn, axis=0, keepdims=True)
        o_ref[t:t + 1] = o / l
        lse_ref[t:t + 1] = jnp.broadcast_to(m + jnp.log(l), (1, A_HEADS, A_HEAD_DIM))


def attn_sample(q, k_new, v_new, caches, *, layer, dil):
    db, ts = q.shape[:2]
    wb = caches.shape[2]
    assert wb == A_BLOCK * dil and (dil == 1 or ts <= dil)
    cache_v = caches.reshape(caches.shape[0], db, A_BLOCK, dil, 2, A_HEADS, A_HEAD_DIM)
    n_res = 1 if dil == 1 else ts
    tok_spec = pl.BlockSpec((None, ts, A_HEADS, A_HEAD_DIM), lambda b: (b, 0, 0, 0))
    cache_specs = [pl.BlockSpec((None, None, A_BLOCK, None, None, A_HEADS, A_HEAD_DIM),
                                lambda b, r=r, kv=kv: (layer, b, 0, r, kv, 0, 0))
                   for r in range(n_res) for kv in range(2)]
    out = jax.ShapeDtypeStruct((db, ts, A_HEADS, A_HEAD_DIM), F32)
    return pl.pallas_call(
        functools.partial(_attn_sample_kernel, dil=dil, ts=ts),
        grid=(db,),
        in_specs=[tok_spec, tok_spec, tok_spec] + cache_specs,
        out_specs=[tok_spec, tok_spec],
        out_shape=[out, out],
        compiler_params=_params(1),
        name="attn_sample",
    )(q, k_new, v_new, *([cache_v] * (2 * n_res)))


def _merge_groups(head_out, lses, a_ref):
    m = functools.reduce(jnp.maximum, lses)
    es = [jnp.exp(l - m) for l in lses]
    den = sum(es)
    ws = [e / den for e in es]
    for h in range(A_HEADS):
        a = sum(w[:, h:h + 1] * head_out(g, h) for g, w in enumerate(ws))
        a_ref[:, h * A_HEAD_DIM:(h + 1) * A_HEAD_DIM] = a.astype(a_ref.dtype)


def _attn_merge_rows_kernel(*refs):
    outs = [r[...] for r in refs[:A_GROUPS]]
    lses = [r[...] for r in refs[A_GROUPS:2 * A_GROUPS]]
    m = functools.reduce(jnp.maximum, lses)
    es = [jnp.exp(l - m) for l in lses]
    den = sum(es)
    refs[2 * A_GROUPS][...] = sum(e / den * o for e, o in zip(es, outs))


def attn_merge_rows(outs, lses):
    spec = pl.BlockSpec(outs[0].shape, lambda i: (0, 0, 0, 0))
    return pl.pallas_call(
        _attn_merge_rows_kernel,
        grid=(1,),
        in_specs=[spec] * (2 * A_GROUPS),
        out_specs=spec,
        out_shape=jax.ShapeDtypeStruct(outs[0].shape, F32),
        compiler_params=_params(1),
        name="attn_merge_rows",
    )(*outs, *lses)


def _attn_merge_prompt_kernel(*refs, tile):
    o_refs, l_refs = refs[:A_GROUPS], refs[A_GROUPS:2 * A_GROUPS]
    a_ref, o_nat, l_nat = refs[2 * A_GROUPS + 1:]
    lses = []
    for g, dil in enumerate(A_DILATIONS):
        if dil == 1:
            lses.append(l_refs[g][0])
            continue
        n = tile // dil
        for r in range(dil):
            rows = pl.ds(r, n, stride=dil)
            l_nat[g, rows, :] = l_refs[g][r]
            for h in range(A_HEADS):
                o_nat[g * A_HEADS + h, rows, :] = o_refs[g][r, :, h * A_HEAD_DIM:(h + 1) * A_HEAD_DIM]
        lses.append(l_nat[g])

    def head_out(g, h):
        if A_DILATIONS[g] == 1:
            return o_refs[g][0, :, h * A_HEAD_DIM:(h + 1) * A_HEAD_DIM]
        return o_nat[g * A_HEADS + h]

    _merge_groups(head_out, lses, a_ref)


def attn_merge_prompt(outs, lses, out_init, *, tile=ROW_TILE):
    batch, _, seq, cols = outs[0].shape
    tps = seq // tile
    in_specs = []
    for width in (cols, LANES):
        for dil in A_DILATIONS:
            in_specs.append(pl.BlockSpec((None, dil, tile // dil, width), lambda b, i: (b, 0, i, 0)))
    in_specs.append(pl.BlockSpec(memory_space=pl.ANY))
    return pl.pallas_call(
        functools.partial(_attn_merge_prompt_kernel, tile=tile),
        grid=(batch, tps),
        in_specs=in_specs,
        out_specs=pl.BlockSpec((tile, cols), lambda b, i: (b * tps + i, 0)),
        out_shape=jax.ShapeDtypeStruct(out_init.shape, BF16),
        scratch_shapes=[pltpu.VMEM((A_GROUPS * A_HEADS, tile, A_HEAD_DIM), F32),
                        pltpu.VMEM((A_GROUPS, tile, LANES), F32)],
        input_output_aliases={2 * A_GROUPS: 0},
        compiler_params=_params(2),
        name="attn_merge_prompt",
    )(*outs, *lses, out_init)


def _conv_silu_chunk(first, u_ref, prev_ref, w_ref, bias, full_ref, out_ref, col0):
    c, width = u_ref.shape
    cols = slice(col0, col0 + width)

    @pl.when(first)
    def _():
        full_ref[0:SUBLANES, cols] = prev_ref[...]

    u = u_ref[...]
    full_ref[SUBLANES:SUBLANES + c, cols] = u
    acc = u * w_ref[CONV_TAPS - 1:CONV_TAPS, :]
    if bias is not None:
        acc = acc + bias
    for s in range(1, CONV_TAPS):
        acc = acc + full_ref[SUBLANES - s:SUBLANES - s + c, cols] * w_ref[CONV_TAPS - 1 - s:CONV_TAPS - s, :]
    out_ref[:, cols] = _silu(acc)
    full_ref[0:SUBLANES, cols] = u[c - SUBLANES:c, :]


def _unit_lower_inverses(mats, n, top):
    blk = min(B_INV_BLOCK, top)
    ii = lax.broadcasted_iota(jnp.int32, (n, n), 0)
    jj = lax.broadcasted_iota(jnp.int32, (n, n), 1)
    eye = jnp.where(ii == jj, 1.0, 0.0).astype(F32)
    shift = blk.bit_length() - 1
    same = (ii >> shift) == (jj >> shift)
    ps = [jnp.where(same, -a, 0.0) for a in mats]
    xs = [eye + p for p in ps]
    for _ in range(shift - 1):
        ps = [_mm(p, p, B_INV_PASSES) for p in ps]
        xs = [x + _mm(x, p, B_INV_PASSES) for x, p in zip(xs, ps)]
    size = blk
    while size < top:
        shift += 1
        same_next = (ii >> shift) == (jj >> shift)
        sel = jnp.logical_and(same_next, jnp.logical_not(same))
        ys = [_mm(x, jnp.where(sel, a, 0.0), B_INV_PASSES) for x, a in zip(xs, mats)]
        xs = [x - _mm(y, x, B_INV_PASSES) for x, y in zip(xs, ys)]
        same = same_next
        size *= 2
    return xs


def _gdn_kernel(qr_ref, kr_ref, vr_ref, qp_ref, kp_ref, vp_ref, qw_ref, kw_ref, vw_ref,
                z_ref, ba_ref, alog_ref, dtb_ref, nw_ref, s0_ref, _init_ref,
                o_ref, s_ref, full_ref, conv_ref, *, c, hps, t_valid):
    hb = pl.program_id(1)
    ci = pl.program_id(2)
    dk = B_HEAD_DIM
    c2 = 2 * c
    assert c2 == LANES

    @pl.when(ci == 0)
    def _():
        s_ref[...] = s0_ref[...]

    qw = hps // 2 * dk
    for raw, prev, taps, col0 in ((qr_ref, qp_ref, qw_ref, 0), (kr_ref, kp_ref, kw_ref, qw),
                                  (vr_ref, vp_ref, vw_ref, 2 * qw)):
        _conv_silu_chunk(ci == 0, raw, prev, taps, None, full_ref, conv_ref, col0)
    q_ref = conv_ref.at[:, 0:qw]
    k_ref = conv_ref.at[:, qw:2 * qw]
    v_ref = conv_ref.at[:, 2 * qw:]

    ba = ba_ref[...]
    valid = lax.broadcasted_iota(jnp.int32, (c, LANES), 0) + ci * c < t_valid
    beta_all = jnp.where(valid, jax.nn.sigmoid(ba), 0.0)
    g_all = jnp.where(valid, -jnp.exp(alog_ref[pl.ds(hb, 1), :]) * _softplus(ba + dtb_ref[pl.ds(hb, 1), :]), 0.0)
    ci_, cj_ = lax.broadcasted_iota(jnp.int32, (c, c), 0), lax.broadcasted_iota(jnp.int32, (c, c), 1)
    gc_all = _select_rows(jnp.where(ci_ >= cj_, 1.0, 0.0).astype(BF16), g_all)
    gc_t = jnp.concatenate([gc_all, gc_all], axis=0).T
    ii = lax.broadcasted_iota(jnp.int32, (c2, c2), 0)
    jj = lax.broadcasted_iota(jnp.int32, (c2, c2), 1)
    shift = c.bit_length() - 1
    same = (ii >> shift) == (jj >> shift)
    incl = jnp.logical_and(same, ii >= jj)
    strict = jnp.logical_and(same, ii > jj)
    first = lax.broadcasted_iota(jnp.int32, (1, c2), 1) < c
    top = lax.broadcasted_iota(jnp.int32, (c2, 1), 0) < c
    nw = nw_ref[...]

    def stack_cols(arr, l0, l1):
        return jnp.concatenate([arr[:, l0:l0 + 1], arr[:, l1:l1 + 1]], axis=0)

    def own_half(r):
        return jnp.where(top, r[:, :dk], r[:, dk:])

    pairs = range(hps // 2)
    cols = lambda h: slice(h * dk, (h + 1) * dk)

    def prepare(qh):
        h0, h1 = 2 * qh, 2 * qh + 1
        q = q_ref[:, cols(qh)]
        k = k_ref[:, cols(qh)]
        q = q * lax.rsqrt(jnp.sum(q * q, axis=-1, keepdims=True) + 1e-6) * (dk ** -0.5)
        k = k * lax.rsqrt(jnp.sum(k * k, axis=-1, keepdims=True) + 1e-6)
        q2 = jnp.concatenate([q, q], axis=0)
        k2 = jnp.concatenate([k, k], axis=0)
        k2_b = k2.astype(BF16)
        beta = stack_cols(beta_all, h0, h1)
        gc = stack_cols(gc_all, hps + h0, hps + h1)
        gr = jnp.where(first, gc_t[hps + h0:hps + h0 + 1, :], gc_t[hps + h1:hps + h1 + 1, :])
        gl0 = gc_all[c - 1:c, hps + h0:hps + h0 + 1]
        gl1 = gc_all[c - 1:c, hps + h1:hps + h1 + 1]
        decay = jnp.exp(jnp.where(incl, gc - gr, NEG_BIG))
        eg = jnp.exp(gc)
        v2 = jnp.concatenate([v_ref[:, cols(h0)], v_ref[:, cols(h1)]], axis=0)
        return dict(
            a=jnp.where(strict, _dot_nt(k2_b, k2_b) * decay, 0.0) * beta,
            qk=jnp.where(incl, _dot_nt(q2.astype(BF16), k2_b) * decay, 0.0).astype(BF16),
            rhs=jnp.concatenate([v2 * beta, k2 * (beta * eg)], axis=1),
            q_eg=(q2 * eg).astype(BF16),
            k_dec=(k2 * jnp.exp(jnp.where(top, gl0, gl1) - gc)).astype(BF16),
            dec_cat=jnp.concatenate([jnp.broadcast_to(jnp.exp(gl0), (1, dk)),
                                     jnp.broadcast_to(jnp.exp(gl1), (1, dk))], axis=1))

    st = [prepare(qh) for qh in pairs]
    t_inv = _unit_lower_inverses([s["a"] for s in st], c2, c)
    uw = [_mm(t, s["rhs"], B_INV_PASSES) for t, s in zip(t_inv, st)]
    s_cat = [jnp.concatenate([s_ref[2 * qh], s_ref[2 * qh + 1]], axis=1) for qh in pairs]
    s_cat_b = [s.astype(BF16) for s in s_cat]
    v_new = [r[:, :dk] - own_half(_dot(r[:, dk:].astype(BF16), sb)) for r, sb in zip(uw, s_cat_b)]
    v_new_b = [v.astype(BF16) for v in v_new]
    outs = [own_half(_dot(s["q_eg"], sb)) + _dot(s["qk"], vb) for s, sb, vb in zip(st, s_cat_b, v_new_b)]
    for qh in pairs:
        h0, h1 = 2 * qh, 2 * qh + 1
        v = v_new[qh]
        v_blk = jnp.concatenate([jnp.where(top, v, 0.0), jnp.where(top, 0.0, v)], axis=1).astype(BF16)
        s_new = s_cat[qh] * st[qh]["dec_cat"] + _dot_tn(st[qh]["k_dec"], v_blk)
        s_ref[h0] = s_new[:, :dk]
        s_ref[h1] = s_new[:, dk:]
        o = outs[qh]
        o = o * lax.rsqrt(jnp.mean(o * o, axis=-1, keepdims=True) + RMS_EPS) * nw
        o = (o * _silu(jnp.concatenate([z_ref[:, cols(h0)], z_ref[:, cols(h1)]], axis=0))).astype(o_ref.dtype)
        o_ref[:, cols(h0)] = o[:c]
        o_ref[:, cols(h1)] = o[c:]


def gdn_scan(proj, conv_prev, conv_taps, ba, alog_pad, dtb_pad, norm_w, s0, out_init, *, nseq, seq, t_valid):
    c = B_CHUNK
    hps = B_HEADS_PER_STEP
    nc = seq // c
    ng = B_V_HEADS // hps
    qw = hps // 2 * B_HEAD_DIM
    vw = hps * B_HEAD_DIM
    par_rows = alog_pad.shape[0]
    row = lambda s, h, ci: s * nc + ci
    k0, v0, z0 = B_KEY_DIM // qw, 2 * B_KEY_DIM // vw, B_CONV_DIM // vw
    chunk = lambda width, col0: pl.BlockSpec((c, width), lambda s, h, ci: (row(s, h, ci), col0 + h))
    state = lambda width, col0: pl.BlockSpec((None, SUBLANES, width), lambda s, h, ci: (s, 0, col0 + h))
    taps = lambda width, col0: pl.BlockSpec((SUBLANES, width), lambda s, h, ci: (0, col0 + h))
    params = pl.BlockSpec((par_rows, LANES), lambda s, h, ci: (0, 0))
    heads = pl.BlockSpec((None, hps, B_HEAD_DIM, B_HEAD_DIM), lambda s, h, ci: (s, h, 0, 0))
    return pl.pallas_call(
        functools.partial(_gdn_kernel, c=c, hps=hps, t_valid=t_valid),
        grid=(nseq, ng, nc),
        in_specs=[chunk(qw, 0), chunk(qw, k0), chunk(vw, v0),
                  state(qw, 0), state(qw, k0), state(vw, v0),
                  taps(qw, 0), taps(qw, k0), taps(vw, v0),
                  chunk(vw, z0), chunk(LANES, 0), params, params,
                  pl.BlockSpec((1, B_HEAD_DIM), lambda s, h, ci: (0, 0)), heads,
                  pl.BlockSpec(memory_space=pl.ANY)],
        out_specs=[chunk(vw, 0), heads],
        out_shape=[jax.ShapeDtypeStruct(out_init.shape, BF16),
                   jax.ShapeDtypeStruct((nseq, B_V_HEADS, B_HEAD_DIM, B_HEAD_DIM), F32)],
        scratch_shapes=[pltpu.VMEM((SUBLANES + c, 2 * qw + vw), F32),
                        pltpu.VMEM((c, 2 * qw + vw), F32)],
        input_output_aliases={15: 0},
        compiler_params=_params(3),
        name="gdn_scan",
    )(proj, proj, proj, conv_prev, conv_prev, conv_prev, conv_taps, conv_taps, conv_taps,
      proj, ba, alog_pad, dtb_pad, norm_w, s0, out_init)


def _ssd_kernel(xr_ref, br_ref, cr_ref, xp_ref, bp_ref, cp_ref, xw_ref, bw_ref, cw_ref,
                xbias_ref, bbias_ref, cbias_ref, z_ref, dt_ref, dtb_ref, alog_ref, dskip_ref, nw_ref,
                h0_ref, _init_ref, y_ref, hout_ref, ht_ref, full_ref, conv_ref, *, c, nc, t_valid):
    g = pl.program_id(1)
    ci = pl.program_id(2)
    hpg = C_HEADS_PER_GROUP
    p = C_HEAD_DIM
    gcols = hpg * p

    @pl.when(ci == 0)
    def _():
        ht_ref[...] = h0_ref[...].reshape(gcols, C_D_STATE).T

    for raw, prev, taps, bias, col0 in ((xr_ref, xp_ref, xw_ref, xbias_ref, 0),
                                        (br_ref, bp_ref, bw_ref, bbias_ref, gcols),
                                        (cr_ref, cp_ref, cw_ref, cbias_ref, gcols + C_D_STATE)):
        _conv_silu_chunk(ci == 0, raw, prev, taps, bias[...], full_ref, conv_ref, col0)
    x_ref = conv_ref.at[:, 0:gcols]
    b_ref = conv_ref.at[:, gcols:gcols + C_D_STATE]
    c_ref = conv_ref.at[:, gcols + C_D_STATE:]

    row = lax.broadcasted_iota(jnp.int32, (c, LANES), 0) + ci * c
    dt = jnp.where(row < t_valid, _softplus(dt_ref[...] + dtb_ref[pl.ds(g, 1), :]), 0.0)
    la = dt * -jnp.exp(alog_ref[pl.ds(g, 1), :])
    ii = lax.broadcasted_iota(jnp.int32, (c, c), 0)
    jj = lax.broadcasted_iota(jnp.int32, (c, c), 1)
    incl = ii >= jj
    acs = _select_rows(jnp.where(incl, 1.0, 0.0).astype(BF16), la)
    acs_t = acs.T
    el = lax.broadcasted_iota(jnp.int32, (LANES, gcols), 0)
    ej = lax.broadcasted_iota(jnp.int32, (LANES, gcols), 1)
    expand = jnp.where(el == (ej >> (p.bit_length() - 1)), 1.0, 0.0).astype(BF16)
    dt_e = _select_cols(dt, expand)
    acs_e = _select_cols(acs, expand)
    dskip_e = _select_cols(jnp.broadcast_to(dskip_ref[pl.ds(g, 1), :], (SUBLANES, LANES)), expand)[0:1, :]

    x = x_ref[...]
    bm = b_ref[...].astype(BF16)
    cm = c_ref[...].astype(BF16)
    xdt = (x * dt_e).astype(BF16)
    cb = _dot_nt(cm, bm)
    ht = ht_ref[...]
    y = _dot(cm, ht.astype(BF16)) * jnp.exp(acs_e) + dskip_e * x
    lane = lax.broadcasted_iota(jnp.int32, (c, 2 * p), 1)
    decay = [jnp.exp(jnp.where(incl, acs[:, hd:hd + 1] - acs_t[hd:hd + 1, :], NEG_BIG)) for hd in range(hpg)]
    mats = [(cb * dec).astype(BF16) for dec in decay]
    ys = [_dot(mats[hd], xdt[:, (hd // 2) * 2 * p:(hd // 2 + 1) * 2 * p]) for hd in range(hpg)]
    y = y + jnp.concatenate([jnp.where(lane < p, ys[2 * j], ys[2 * j + 1]) for j in range(hpg // 2)], axis=1)
    y = y * _silu(z_ref[...])
    y = y * lax.rsqrt(jnp.mean(y * y, axis=-1, keepdims=True) + RMS_EPS) * nw_ref[...]
    y_ref[...] = y.astype(y_ref.dtype)

    last_e = acs_e[c - 1:c, :]
    xw = (x * (jnp.exp(last_e - acs_e) * dt_e)).astype(BF16)
    ht_new = ht * jnp.exp(last_e) + _dot_tn(bm, xw)
    ht_ref[...] = ht_new

    @pl.when(ci == nc - 1)
    def _():
        hout_ref[...] = ht_new.T.reshape(hpg, p, C_D_STATE)


def ssd_scan(proj, conv_prev, conv_taps, conv_bias, dt, dtb_pad, alog_pad, dskip_pad, norm_w, h0, out_init,
             *, nseq, seq, t_valid):
    c = C_CHUNK
    nc = seq // c
    gc = C_GROUP_COLS
    n = C_D_STATE
    row = lambda s, g, ci: s * nc + ci
    px, pb_, pc_ = C_D_INNER // gc, 2 * C_D_INNER // n, 2 * C_D_INNER // n + C_GROUPS
    cb_, cc_ = C_D_INNER // n, C_D_INNER // n + C_GROUPS
    chunk = lambda width, col0: pl.BlockSpec((c, width), lambda s, g, ci: (row(s, g, ci), col0 + g))
    state = lambda width, col0: pl.BlockSpec((None, SUBLANES, width), lambda s, g, ci: (s, 0, col0 + g))
    taps = lambda width, col0: pl.BlockSpec((SUBLANES, width), lambda s, g, ci: (0, col0 + g))
    bias = lambda width, col0: pl.BlockSpec((1, width), lambda s, g, ci: (0, col0 + g))
    par = pl.BlockSpec((C_GROUPS, LANES), lambda s, g, ci: (0, 0))
    heads = pl.BlockSpec((None, C_HEADS_PER_GROUP, C_HEAD_DIM, n), lambda s, g, ci: (s, g, 0, 0))
    width = gc + 2 * n
    return pl.pallas_call(
        functools.partial(_ssd_kernel, c=c, nc=nc, t_valid=t_valid),
        grid=(nseq, C_GROUPS, nc),
        in_specs=[chunk(gc, px), chunk(n, pb_), chunk(n, pc_),
                  state(gc, 0), state(n, cb_), state(n, cc_),
                  taps(gc, 0), taps(n, cb_), taps(n, cc_),
                  bias(gc, 0), bias(n, cb_), bias(n, cc_),
                  chunk(gc, 0), chunk(LANES, 0), par, par, par, bias(gc, 0), heads,
                  pl.BlockSpec(memory_space=pl.ANY)],
        out_specs=[chunk(gc, 0), heads],
        out_shape=[jax.ShapeDtypeStruct(out_init.shape, BF16),
                   jax.ShapeDtypeStruct((nseq, C_HEADS, C_HEAD_DIM, n), F32)],
        scratch_shapes=[pltpu.VMEM((n, gc), F32),
                        pltpu.VMEM((SUBLANES + c, width), F32),
                        pltpu.VMEM((c, width), F32)],
        input_output_aliases={19: 0},
        compiler_params=_params(3),
        name="ssd_scan",
    )(proj, proj, proj, conv_prev, conv_prev, conv_prev, conv_taps, conv_taps, conv_taps,
      conv_bias, conv_bias, conv_bias, proj, dt, dtb_pad, alog_pad, dskip_pad, norm_w, h0, out_init)


def _rope_tables(mp, seq, n_sample, ts):
    pos = jnp.concatenate([jnp.arange(mp) % seq, PAST_LEN + jnp.arange(n_sample) % ts]).astype(F32)
    inv_freq = ROPE_THETA ** (-jnp.arange(A_ROT_HALF, dtype=F32) / A_ROT_HALF)
    ang = pos[:, None] * inv_freq[None, :]
    cos, sin = jnp.cos(ang), jnp.sin(ang)
    rest = A_HEAD_DIM - 2 * A_ROT_HALF
    rows = pos.shape[0]
    cos_full = jnp.concatenate([cos, cos, jnp.ones((rows, rest), F32)], axis=1)
    sin_full = jnp.concatenate([-sin, sin, jnp.zeros((rows, rest), F32)], axis=1)
    return cos_full, sin_full


def _lane_groups(cols, group):
    lead = cols.shape[:-1]
    n = cols.shape[-1]
    c = cols.reshape(*lead, n // group, group)
    c = jnp.pad(c, [(0, 0)] * (c.ndim - 1) + [(0, LANES - group)])
    return c.reshape(*lead, n // group * LANES)


def _pad_seq_rows(x, nseq, ts, seq):
    cols = x.shape[-1]
    x = jnp.pad(x.reshape(nseq, ts, cols), ((0, 0), (0, seq - ts), (0, 0)))
    return x.reshape(nseq * seq, cols)


def _conv_prev(state):
    return jnp.pad(state, ((0, 0), (SUBLANES - state.shape[1], 0), (0, 0)))


def _conv_taps(w):
    return jnp.pad(w, ((0, SUBLANES - w.shape[0]), (0, 0)))


def _last_rows(x, nseq, seq, n, col0, cols):
    return jnp.stack([x[(s + 1) * seq - n:(s + 1) * seq, col0:col0 + cols] for s in range(nseq)])


def kernel(x_prompt, x_sample, cache_a_kv_w128, cache_a_kv_w512, cache_a_kv_w2048, state_b_ssm, state_b_conv, state_c_ssm, state_c_conv, ln_g, ln_b, ffn1_w_gu, ffn1_w_down, ffn2_w_gu, ffn2_w_down, a_w_qkv, a_w_o, b_w_in, b_conv_w, b_a_log, b_dt_bias, b_norm_w, b_w_out, c_w_in, c_conv_w, c_conv_b, c_dt_bias, c_a_log, c_d, c_norm_w, c_w_out):
    a_bufs = (cache_a_kv_w128, cache_a_kv_w512, cache_a_kv_w2048)
    bp, seq, d = x_prompt.shape
    db, ts, _ = x_sample.shape
    mp, ms = bp * seq, db * ts
    assert seq % (A_BLOCK * max(A_DILATIONS)) == 0 and mp % ms == 0
    x = jnp.concatenate([x_prompt.reshape(mp, d), x_sample.reshape(ms, d)], axis=0)
    xb = x.astype(BF16)
    cos_full, sin_full = _rope_tables(mp, seq, ms, ts)
    wb16 = {name: w.astype(BF16) for name, w in dict(
        ffn1_down=ffn1_w_down, ffn2_down=ffn2_w_down, a_qkv=a_w_qkv, a_o=a_w_o,
        b_in=b_w_in[:, :, :B_CONV_DIM + B_VAL_DIM], b_out=b_w_out,
        c_in=c_w_in[:, :, :C_D_INNER + C_CONV_DIM], c_out=c_w_out).items()}
    w_gu = dict(ffn1=ffn1_w_gu, ffn2=ffn2_w_gu)

    def rows_buffer(a_s):
        return lax.dynamic_update_slice(jnp.zeros((mp + ms, a_s.shape[1]), BF16), a_s, (mp, 0))

    def valid_rows(a_s, rows):
        return a_s.reshape(db, rows, -1)[:, :ts].reshape(ms, -1)

    outs = {}

    def mixer_a(xb, j):
        qkv_groups = qkv_project(xb, wb16["a_qkv"], cos_full, sin_full, layer=j, batch=bp, seq=seq)
        qkv_s = matmul(xb[mp:], wb16["a_qkv"], layer=j, n=a_w_qkv.shape[2], tn=1024, tm=ms)
        gw = A_GROUP_COLS
        o_p, l_p, o_s, l_s = [], [], [], []
        for g, dil in enumerate(A_DILATIONS):
            qkv_r = qkv_groups[g]
            o, lse = attn_prompt(qkv_r)
            o_p.append(o)
            l_p.append(lse)
            new = qkv_split(qkv_s, cos_full[mp:], sin_full[mp:], g=g, dil=1, row0=0, batch=1, seq=ms, tile=ms)
            q_s, k_s, v_s = (new[0, 0, :, sec * gw:(sec + 1) * gw].reshape(db, ts, A_HEADS, A_HEAD_DIM)
                             for sec in range(A_SECTIONS))
            o, lse = attn_sample(q_s, k_s, v_s, a_bufs[g], layer=j, dil=dil)
            o_s.append(o)
            l_s.append(lse)
            length = seq // dil
            tail = qkv_r[:, :, length - A_BLOCK:, gw:].transpose(0, 2, 1, 3)
            outs.setdefault("a_p%d" % g, []).append(tail.reshape(bp, A_BLOCK * dil, 2, A_HEADS, A_HEAD_DIM))
            outs.setdefault("a_new%d" % g, []).append(jnp.stack([k_s, v_s], axis=2))
        a_s = attn_merge_rows(o_s, l_s).reshape(ms, gw).astype(BF16)
        return attn_merge_prompt(o_p, l_p, rows_buffer(a_s)), wb16["a_o"]

    def mixer_b(xb, j):
        hps = B_HEADS_PER_STEP
        ng = B_V_HEADS // hps
        w_in = b_w_in[j]
        main = B_CONV_DIM + B_VAL_DIM
        pb = matmul(xb, wb16["b_in"], layer=j, n=main, tn=1024)
        w_tail = jnp.concatenate([w_in[:, main:main + B_V_HEADS].reshape(d, ng, hps),
                                  w_in[:, main + B_V_HEADS:].reshape(d, ng, hps)], axis=2)
        ba = matmul(xb, _lane_groups(w_tail.reshape(d, -1), 2 * hps).astype(BF16)[None],
                    layer=0, n=ng * LANES, tn=ng * LANES)
        gate_lanes = lambda v: jnp.pad(_lane_groups(
            jnp.concatenate([jnp.zeros((ng, hps), F32), v.reshape(ng, hps)], axis=1).reshape(-1), 2 * hps
        ).reshape(ng, LANES), ((0, SUBLANES - ng), (0, 0)))
        alog_pad, dtb_pad = gate_lanes(b_a_log[j]), gate_lanes(b_dt_bias[j])
        taps = _conv_taps(b_conv_w[j])
        nw = b_norm_w[j].reshape(1, B_HEAD_DIM)
        pb_s = _pad_seq_rows(pb[mp:], db, ts, B_CHUNK)
        ba_s = _pad_seq_rows(ba[mp:], db, ts, B_CHUNK)
        a_s, s_s = gdn_scan(pb_s, _conv_prev(state_b_conv[j]), taps, ba_s, alog_pad, dtb_pad, nw, state_b_ssm[j],
                            jnp.zeros((db * B_CHUNK, B_VAL_DIM), BF16), nseq=db, seq=B_CHUNK, t_valid=ts)
        a, s_p = gdn_scan(pb, jnp.zeros((bp, SUBLANES, B_CONV_DIM), F32), taps, ba, alog_pad, dtb_pad, nw,
                          jnp.zeros((bp, B_V_HEADS, B_HEAD_DIM, B_HEAD_DIM), F32),
                          rows_buffer(valid_rows(a_s, B_CHUNK)), nseq=bp, seq=seq, t_valid=seq)
        outs.setdefault("b_ssm_p", []).append(s_p)
        outs.setdefault("b_ssm_s", []).append(s_s)
        outs.setdefault("b_conv_p", []).append(_last_rows(pb, bp, seq, CONV_TAPS - 1, 0, B_CONV_DIM))
        outs.setdefault("b_conv_s", []).append(_last_rows(pb[mp:], db, ts, CONV_TAPS - 1, 0, B_CONV_DIM))
        return a, wb16["b_out"]

    def mixer_c(xb, j):
        w_in = c_w_in[j]
        main = C_D_INNER + C_CONV_DIM
        pc = matmul(xb, wb16["c_in"], layer=j, n=main, tn=1024)
        dt = matmul(xb, _lane_groups(w_in[:, main:], C_HEADS_PER_GROUP).astype(BF16)[None],
                    layer=0, n=C_GROUPS * LANES, tn=C_GROUPS * LANES)
        head_lanes = lambda v: _lane_groups(v, C_HEADS_PER_GROUP).reshape(C_GROUPS, LANES)
        dtb_pad, alog_pad, dskip_pad = head_lanes(c_dt_bias[j]), head_lanes(c_a_log[j]), head_lanes(c_d[j])
        taps = _conv_taps(c_conv_w[j])
        bias = c_conv_b[j].reshape(1, C_CONV_DIM)
        nw = c_norm_w[j].reshape(1, C_D_INNER)
        pc_s = _pad_seq_rows(pc[mp:], db, ts, C_CHUNK)
        dt_s = _pad_seq_rows(dt[mp:], db, ts, C_CHUNK)
        a_s, h_s = ssd_scan(pc_s, _conv_prev(state_c_conv[j]), taps, bias, dt_s, dtb_pad, alog_pad, dskip_pad, nw,
                            state_c_ssm[j], jnp.zeros((db * C_CHUNK, C_D_INNER), BF16),
                            nseq=db, seq=C_CHUNK, t_valid=ts)
        a, h_p = ssd_scan(pc, jnp.zeros((bp, SUBLANES, C_CONV_DIM), F32), taps, bias, dt, dtb_pad, alog_pad,
                          dskip_pad, nw, jnp.zeros((bp, C_HEADS, C_HEAD_DIM, C_D_STATE), F32),
                          rows_buffer(valid_rows(a_s, C_CHUNK)), nseq=bp, seq=seq, t_valid=seq)
        outs.setdefault("c_ssm_p", []).append(h_p)
        outs.setdefault("c_ssm_s", []).append(h_s)
        outs.setdefault("c_conv_p", []).append(_last_rows(pc, bp, seq, CONV_TAPS - 1, C_D_INNER, C_CONV_DIM))
        outs.setdefault("c_conv_s", []).append(_last_rows(pc[mp:], db, ts, CONV_TAPS - 1, C_D_INNER, C_CONV_DIM))
        return a, wb16["c_out"]

    def ffn(x, xb, which, i, g, b):
        h = swiglu_up(xb, w_gu[which], layer=i)
        return matmul_postnorm(h, wb16[which + "_down"], x, g, b, layer=i, scale=0.5)

    mixers = (mixer_a, mixer_b, mixer_c)
    for i in range(DEPTH):
        x, xb = ffn(x, xb, "ffn1", i, ln_g[i, 0], ln_b[i, 0])
        a, w_out = mixers[i % 3](xb, i // 3)
        x, xb = matmul_postnorm(a, w_out, x, ln_g[i, 1], ln_b[i, 1], layer=i // 3, scale=1.0)
        x, xb = ffn(x, xb, "ffn2", i, ln_g[i, 2], ln_b[i, 2])

    st = lambda name: jnp.stack(outs[name])
    def shifted(buf, new):
        moved = lax.pad(buf, jnp.zeros((), buf.dtype), [(0, 0, 0), (0, 0, 0), (-ts, ts, 0)] + [(0, 0, 0)] * 3)
        return lax.dynamic_update_slice(moved, new, (0, 0, buf.shape[2] - ts, 0, 0, 0))

    a_s = [shifted(a_bufs[g], st("a_new%d" % g)) for g in range(A_GROUPS)]
    return (x[:mp].reshape(bp, seq, d), x[mp:].reshape(db, ts, d),
            st("a_p0"), a_s[0], st("a_p1"), a_s[1], st("a_p2"), a_s[2],
            st("b_ssm_p"), st("b_ssm_s"), st("b_conv_p"), st("b_conv_s"),
            st("c_ssm_p"), st("c_ssm_s"), st("c_conv_p"), st("c_conv_s"))
```

```python
import functools

import jax
import jax.numpy as jnp
from jax import lax
from jax.experimental import pallas as pl
from jax.experimental.pallas import tpu as pltpu

F32 = jnp.float32
BF16 = jnp.bfloat16

VMEM_LIMIT_BYTES = 56 * 1024 * 1024
LANES = 128
SUBLANES = 8
ROW_TILE = 512
POSTNORM_ROW_TILE = 256
NEG_BIG = -1e30

DEPTH = 4
DN_ALPHA = (2.0 * DEPTH) ** 0.25
LN_EPS = 1e-5
RMS_EPS = 1e-6
PAST_LEN = 16384

A_DILATIONS = (1, 4, 16)
A_GROUPS = 3
A_HEADS = 8
A_HEAD_DIM = 128
A_ROT_HALF = A_HEAD_DIM // 8
A_BLOCK = 128
A_GROUP_COLS = A_HEADS * A_HEAD_DIM
A_SECTIONS = 3
A_SPLIT_TILE = 2048
A_PROJECT_PARTS = 2
ROPE_THETA = 500000.0

B_QK_HEADS = 16
B_V_HEADS = 32
B_HEAD_DIM = 128
B_KEY_DIM = B_QK_HEADS * B_HEAD_DIM
B_VAL_DIM = B_V_HEADS * B_HEAD_DIM
B_CONV_DIM = 2 * B_KEY_DIM + B_VAL_DIM
B_CHUNK = 64
B_HEADS_PER_STEP = 16
B_INV_BLOCK = 16
B_INV_PASSES = 1

C_D_INNER = 4096
C_HEADS = 64
C_HEAD_DIM = 64
C_GROUPS = 8
C_HEADS_PER_GROUP = C_HEADS // C_GROUPS
C_GROUP_COLS = C_D_INNER // C_GROUPS
C_D_STATE = 128
C_CONV_DIM = C_D_INNER + 2 * C_GROUPS * C_D_STATE
C_CHUNK = 128

CONV_TAPS = 4


def _params(n_axes):
    return pltpu.CompilerParams(
        dimension_semantics=("arbitrary",) * n_axes,
        vmem_limit_bytes=VMEM_LIMIT_BYTES,
    )


def _dot(a, b):
    return jnp.dot(a, b, preferred_element_type=F32)


def _dot_nt(a, b):
    return lax.dot_general(a, b, (((1,), (1,)), ((), ())), preferred_element_type=F32)


def _dot_tn(a, b):
    return lax.dot_general(a, b, (((0,), (0,)), ((), ())), preferred_element_type=F32)


def _split3(x):
    hi = x.astype(BF16)
    r = x - hi.astype(F32)
    mid = r.astype(BF16)
    lo = (r - mid.astype(F32)).astype(BF16)
    return hi, mid, lo


def _select_rows(sel, x):
    return sum(_dot(sel, part) for part in _split3(x))


def _select_cols(x, sel):
    return sum(_dot(part, sel) for part in _split3(x))


def _mm(a, b, passes):
    a_hi, b_hi = a.astype(BF16), b.astype(BF16)
    out = _dot(a_hi, b_hi)
    if passes == 3:
        a_lo = (a - a_hi.astype(F32)).astype(BF16)
        b_lo = (b - b_hi.astype(F32)).astype(BF16)
        out = out + _dot(a_hi, b_lo) + _dot(a_lo, b_hi)
    return out


def _silu(x):
    return x * jax.nn.sigmoid(x)


def _softplus(x):
    return jnp.maximum(x, 0.0) + jnp.log(1.0 + jnp.exp(-jnp.abs(x)))


def _row_tiles(rows, tm):
    n_full, rem = divmod(rows, tm)
    return n_full + (1 if rem else 0), n_full, rem


def _per_row_tile(i, n_full, rem, tm, body):
    if rem == 0:
        body(tm)
        return

    @pl.when(i < n_full)
    def _():
        body(tm)

    @pl.when(i == n_full)
    def _():
        body(rem)


def _mm_kernel(x_ref, w_ref, o_ref, *, tm, n_full, rem):
    def body(r):
        o_ref[:r, :] = _dot(x_ref[:r, :], w_ref[...]).astype(o_ref.dtype)

    _per_row_tile(pl.program_id(0), n_full, rem, tm, body)


def matmul(x, w, *, layer, n, tn, out_dtype=F32, tm=2 * ROW_TILE):
    rows, k = x.shape
    steps, n_full, rem = _row_tiles(rows, tm)
    return pl.pallas_call(
        functools.partial(_mm_kernel, tm=tm, n_full=n_full, rem=rem),
        grid=(steps, n // tn),
        in_specs=[pl.BlockSpec((tm, k), lambda i, j: (i, 0)),
                  pl.BlockSpec((None, k, tn), lambda i, j: (layer, 0, j))],
        out_specs=pl.BlockSpec((tm, tn), lambda i, j: (i, j)),
        out_shape=jax.ShapeDtypeStruct((rows, n), out_dtype),
        compiler_params=_params(2),
        name="matmul",
    )(x, w)


def _swiglu_kernel(x_ref, wg_ref, wu_ref, o_ref, *, tm, n_full, rem):
    def body(r):
        x = x_ref[:r, :]
        gate = _dot(x, wg_ref[...].astype(BF16))
        up = _dot(x, wu_ref[...].astype(BF16))
        o_ref[:r, :] = (_silu(gate) * up).astype(o_ref.dtype)

    _per_row_tile(pl.program_id(0), n_full, rem, tm, body)


def swiglu_up(x, w_gu, *, layer, tn=512, tm=2 * ROW_TILE):
    rows, k = x.shape
    f = w_gu.shape[2] // 2
    steps, n_full, rem = _row_tiles(rows, tm)
    nj = f // tn
    return pl.pallas_call(
        functools.partial(_swiglu_kernel, tm=tm, n_full=n_full, rem=rem),
        grid=(steps, nj),
        in_specs=[pl.BlockSpec((tm, k), lambda i, j: (i, 0)),
                  pl.BlockSpec((None, k, tn), lambda i, j: (layer, 0, j)),
                  pl.BlockSpec((None, k, tn), lambda i, j: (layer, 0, j + nj))],
        out_specs=pl.BlockSpec((tm, tn), lambda i, j: (i, j)),
        out_shape=jax.ShapeDtypeStruct((rows, f), BF16),
        compiler_params=_params(2),
        name="swiglu_up",
    )(x, w_gu, w_gu)


def _postnorm_kernel(a_ref, *refs, scale, tm, n_full, rem, split):
    tail_ref = refs[0] if split else a_ref
    w_ref, x_ref, g_ref, b_ref, y_ref, yb_ref = refs[1:] if split else refs

    def body(r):
        a = a_ref[...] if r == tm else tail_ref[:r, :]
        y = DN_ALPHA * x_ref[:r, :] + scale * _dot(a, w_ref[...])
        mu = jnp.mean(y, axis=-1, keepdims=True)
        yc = y - mu
        var = jnp.mean(yc * yc, axis=-1, keepdims=True)
        out = yc * lax.rsqrt(var + LN_EPS) * g_ref[...] + b_ref[...]
        y_ref[:r, :] = out
        yb_ref[:r, :] = out.astype(BF16)

    _per_row_tile(pl.program_id(0), n_full, rem, tm, body)


def matmul_postnorm(a, w, x, g, b, *, layer, scale, a_tail=None, tm=POSTNORM_ROW_TILE):
    rows, d = x.shape
    kdim = a.shape[1]
    steps, n_full, rem = _row_tiles(rows, tm)
    split = a_tail is not None
    assert not split or (a.shape[0] == n_full * tm and a_tail.shape[0] == rem)
    a_specs = [pl.BlockSpec((tm, kdim), lambda i: (jnp.minimum(i, n_full - 1) if split else i, 0))]
    if split:
        a_specs.append(pl.BlockSpec((rem, kdim), lambda i: (0, 0)))
    return pl.pallas_call(
        functools.partial(_postnorm_kernel, scale=scale, tm=tm, n_full=n_full, rem=rem, split=split),
        grid=(steps,),
        in_specs=a_specs + [
            pl.BlockSpec((None, kdim, d), lambda i: (layer, 0, 0), pipeline_mode=pl.Buffered(1)),
            pl.BlockSpec((tm, d), lambda i: (i, 0)),
            pl.BlockSpec((1, d), lambda i: (0, 0)),
            pl.BlockSpec((1, d), lambda i: (0, 0))],
        out_specs=[pl.BlockSpec((tm, d), lambda i: (i, 0)),
                   pl.BlockSpec((tm, d), lambda i: (i, 0))],
        out_shape=[jax.ShapeDtypeStruct((rows, d), F32),
                   jax.ShapeDtypeStruct((rows, d), BF16)],
        compiler_params=_params(1),
        name="matmul_postnorm",
    )(a, *([a_tail] if split else []), w, x, g.reshape(1, d), b.reshape(1, d))


def _strided_rows(r, n, stride):
    return pl.ds(r, n, stride=stride) if stride > 1 else pl.ds(0, n)


def _qkv_split_kernel(x_ref, cos_ref, sin_ref, o_ref, *, dil, n):
    rotated = pl.program_id(2) < (A_SECTIONS - 1) * A_HEADS
    lane = lax.broadcasted_iota(jnp.int32, (n, A_HEAD_DIM), 1)

    def emit(rotate):
        for r in range(dil):
            rows = _strided_rows(r, n, dil)
            x = x_ref[rows, :]
            if rotate:
                partner = jnp.where(lane < A_ROT_HALF,
                                    pltpu.roll(x, A_HEAD_DIM - A_ROT_HALF, axis=1),
                                    pltpu.roll(x, A_ROT_HALF, axis=1))
                x = x * cos_ref[rows, :] + partner * sin_ref[rows, :]
            o_ref[r] = x

    @pl.when(rotated)
    def _():
        emit(True)

    @pl.when(jnp.logical_not(rotated))
    def _():
        emit(False)


def qkv_split(qkv, cos_full, sin_full, *, g, dil, row0, batch, seq, tile):
    n = tile // dil
    tps = seq // tile
    r0 = row0 // tile
    in_col = lambda c: (c // A_HEADS * A_GROUPS + g) * A_HEADS + c % A_HEADS
    return pl.pallas_call(
        functools.partial(_qkv_split_kernel, dil=dil, n=n),
        grid=(batch, tps, A_SECTIONS * A_HEADS),
        in_specs=[pl.BlockSpec((tile, A_HEAD_DIM), lambda b, i, c: (r0 + b * tps + i, in_col(c))),
                  pl.BlockSpec((tile, A_HEAD_DIM), lambda b, i, c: (r0 + b * tps + i, 0)),
                  pl.BlockSpec((tile, A_HEAD_DIM), lambda b, i, c: (r0 + b * tps + i, 0))],
        out_specs=pl.BlockSpec((None, dil, n, A_HEAD_DIM), lambda b, i, c: (b, 0, i, c)),
        out_shape=jax.ShapeDtypeStruct((batch, dil, seq // dil, A_SECTIONS * A_GROUP_COLS), F32),
        compiler_params=_params(3),
        name="qkv_split",
    )(qkv, cos_full, sin_full)


def _rotate_pairs(x, cos, sin, lane):
    partner = jnp.where(lane < A_ROT_HALF,
                        pltpu.roll(x, A_HEAD_DIM - A_ROT_HALF, axis=1),
                        pltpu.roll(x, A_ROT_HALF, axis=1))
    return x * cos + partner * sin


def _qkv_project_kernel(x_ref, w_ref, cos_ref, sin_ref, o_ref, head_ref, *, dil, tile):
    part = tile // A_PROJECT_PARTS
    n = part // dil
    lane = lax.broadcasted_iota(jnp.int32, (part, A_HEAD_DIM), 1)
    for p in range(A_PROJECT_PARTS):
        rows = slice(p * part, (p + 1) * part)
        res = _dot(x_ref[rows, :], w_ref[...])
        cos, sin = cos_ref[rows, :], sin_ref[rows, :]
        for h in range(A_HEADS):
            cols = slice(h * A_HEAD_DIM, (h + 1) * A_HEAD_DIM)
            rot = _rotate_pairs(res[:, cols], cos, sin, lane)
            if dil == 1:
                o_ref[0, rows, cols] = rot
                continue
            head_ref[p, h] = rot
            for r in range(dil):
                o_ref[r, p * n:(p + 1) * n, cols] = head_ref[p, h, pl.ds(r, n, stride=dil), :]


def qkv_project(xb, w_qkv, cos2, sin2, *, layer, g, batch, seq, tile=ROW_TILE):
    d = xb.shape[1]
    dil = A_DILATIONS[g]
    tps = seq // tile
    table = pl.BlockSpec((None, tile, A_HEAD_DIM),
                         lambda b, i, s: (jnp.where(s == A_SECTIONS - 1, 1, 0), b * tps + i, 0))
    return pl.pallas_call(
        functools.partial(_qkv_project_kernel, dil=dil, tile=tile),
        grid=(batch, tps, A_SECTIONS),
        in_specs=[pl.BlockSpec((tile, d), lambda b, i, s: (b * tps + i, 0)),
                  pl.BlockSpec((None, d, A_GROUP_COLS), lambda b, i, s: (layer, 0, s * A_GROUPS + g)),
                  table, table],
        out_specs=pl.BlockSpec((None, dil, tile // dil, A_GROUP_COLS), lambda b, i, s: (b, 0, i, s)),
        out_shape=jax.ShapeDtypeStruct((batch, dil, seq // dil, A_SECTIONS * A_GROUP_COLS), F32),
        scratch_shapes=[pltpu.VMEM((A_PROJECT_PARTS, A_HEADS, tile // A_PROJECT_PARTS, A_HEAD_DIM), F32)],
        compiler_params=_params(3),
        name="qkv_project",
    )(xb, w_qkv, cos2, sin2)


def _attn_prompt_kernel(q_ref, kp_ref, kc_ref, vp_ref, vc_ref, o_ref, lse_ref):
    lb = pl.program_id(2)
    n = A_BLOCK
    qi = lax.broadcasted_iota(jnp.int32, (n, n), 0)
    kj = lax.broadcasted_iota(jnp.int32, (n, n), 1)
    mask_cur = kj <= qi
    mask_prev = jnp.logical_and(kj >= qi, lb > 0)
    scale = A_HEAD_DIM ** -0.5
    heads = range(A_HEADS)
    cols = [slice(h * A_HEAD_DIM, (h + 1) * A_HEAD_DIM) for h in heads]
    q = [q_ref[:, sl].astype(BF16) for sl in cols]
    s_c = [jnp.where(mask_cur, _dot_nt(q[h], kc_ref[:, cols[h]].astype(BF16)) * scale, NEG_BIG) for h in heads]
    s_p = [jnp.where(mask_prev, _dot_nt(q[h], kp_ref[:, cols[h]].astype(BF16)) * scale, NEG_BIG) for h in heads]
    m = [jnp.maximum(jnp.max(s_c[h], axis=-1, keepdims=True), jnp.max(s_p[h], axis=-1, keepdims=True)) for h in heads]
    p_c = [jnp.exp(s_c[h] - m[h]) for h in heads]
    p_p = [jnp.exp(s_p[h] - m[h]) for h in heads]
    l = [jnp.sum(p_c[h], axis=-1, keepdims=True) + jnp.sum(p_p[h], axis=-1, keepdims=True) for h in heads]
    o = [_dot(p_c[h].astype(BF16), vc_ref[:, cols[h]].astype(BF16))
         + _dot(p_p[h].astype(BF16), vp_ref[:, cols[h]].astype(BF16)) for h in heads]
    lse_all = jnp.zeros((n, LANES), F32)
    for h in heads:
        o_ref[:, cols[h]] = o[h] / l[h]
        lse_all = jnp.where(kj == h, m[h] + jnp.log(l[h]), lse_all)
    lse_ref[...] = lse_all


def attn_prompt(qkv_r):
    batch, dil, length, _ = qkv_r.shape
    nb = length // A_BLOCK
    spec = lambda sec, back: pl.BlockSpec(
        (None, None, A_BLOCK, A_GROUP_COLS), lambda b, r, lb: (b, r, jnp.maximum(lb - back, 0), sec))
    return pl.pallas_call(
        _attn_prompt_kernel,
        grid=(batch, dil, nb),
        in_specs=[spec(0, 0), spec(1, 1), spec(1, 0), spec(2, 1), spec(2, 0)],
        out_specs=[pl.BlockSpec((None, None, A_BLOCK, A_GROUP_COLS), lambda b, r, lb: (b, r, lb, 0)),
                   pl.BlockSpec((None, None, A_BLOCK, LANES), lambda b, r, lb: (b, r, lb, 0))],
        out_shape=[jax.ShapeDtypeStruct((batch, dil, length, A_GROUP_COLS), F32),
                   jax.ShapeDtypeStruct((batch, dil, length, LANES), F32)],
        compiler_params=_params(3),
        name="attn_prompt",
    )(qkv_r, qkv_r, qkv_r, qkv_r, qkv_r)


def _attn_sample_kernel(q_ref, kn_ref, vn_ref, *refs, dil, ts):
    n_res = len(refs) // 2 - 1
    cache_refs, (o_ref, lse_ref) = refs[:2 * n_res], refs[2 * n_res:]
    scale = A_HEAD_DIM ** -0.5
    jc = lax.broadcasted_iota(jnp.int32, (A_BLOCK, 1, 1), 0)
    un = lax.broadcasted_iota(jnp.int32, (ts, 1, 1), 0)
    kn = kn_ref[...]
    vn = vn_ref[...]
    for t in range(ts):
        kc_ref, vc_ref = cache_refs[2 * (t % n_res)], cache_refs[2 * (t % n_res) + 1]
        valid_n = (un <= t) if dil == 1 else (un == t)
        q = q_ref[t:t + 1]
        s_c = jnp.sum(kc_ref[...] * q, axis=-1, keepdims=True) * scale
        if dil == 1:
            s_c = jnp.where(jc >= t, s_c, NEG_BIG)
        s_n = jnp.where(valid_n, jnp.sum(kn * q, axis=-1, keepdims=True) * scale, NEG_BIG)
        m = jnp.maximum(jnp.max(s_c, axis=0, keepdims=True), jnp.max(s_n, axis=0, keepdims=True))
        p_c = jnp.exp(s_c - m)
        p_n = jnp.exp(s_n - m)
        l = jnp.sum(p_c, axis=0, keepdims=True) + jnp.sum(p_n, axis=0, keepdims=True)
        o = jnp.sum(p_c * vc_ref[...], axis=0, keepdims=True) + jnp.sum(p_n * vn, axis=0, keepdims=True)
        o_ref[t:t + 1] = o / l
        lse_ref[t:t + 1] = jnp.broadcast_to(m + jnp.log(l), (1, A_HEADS, A_HEAD_DIM))


def attn_sample(q, k_new, v_new, caches, *, layer, dil):
    db, ts = q.shape[:2]
    wb = caches.shape[2]
    assert wb == A_BLOCK * dil and (dil == 1 or ts <= dil)
    cache_v = caches.reshape(caches.shape[0], db, A_BLOCK, dil, 2, A_HEADS, A_HEAD_DIM)
    n_res = 1 if dil == 1 else ts
    tok_spec = pl.BlockSpec((None, ts, A_HEADS, A_HEAD_DIM), lambda b: (b, 0, 0, 0))
    cache_specs = [pl.BlockSpec((None, None, A_BLOCK, None, None, A_HEADS, A_HEAD_DIM),
                                lambda b, r=r, kv=kv: (layer, b, 0, r, kv, 0, 0))
                   for r in range(n_res) for kv in range(2)]
    out = jax.ShapeDtypeStruct((db, ts, A_HEADS, A_HEAD_DIM), F32)
    return pl.pallas_call(
        functools.partial(_attn_sample_kernel, dil=dil, ts=ts),
        grid=(db,),
        in_specs=[tok_spec, tok_spec, tok_spec] + cache_specs,
        out_specs=[tok_spec, tok_spec],
        out_shape=[out, out],
        compiler_params=_params(1),
        name="attn_sample",
    )(q, k_new, v_new, *([cache_v] * (2 * n_res)))


def _merge_groups(head_out, lses, a_ref):
    m = functools.reduce(jnp.maximum, lses)
    es = [jnp.exp(l - m) for l in lses]
    den = sum(es)
    ws = [e / den for e in es]
    for h in range(A_HEADS):
        a = sum(w[:, h:h + 1] * head_out(g, h) for g, w in enumerate(ws))
        a_ref[:, h * A_HEAD_DIM:(h + 1) * A_HEAD_DIM] = a.astype(a_ref.dtype)


def _attn_merge_rows_kernel(*refs):
    outs = [r[...] for r in refs[:A_GROUPS]]
    lses = [r[...] for r in refs[A_GROUPS:2 * A_GROUPS]]
    m = functools.reduce(jnp.maximum, lses)
    es = [jnp.exp(l - m) for l in lses]
    den = sum(es)
    refs[2 * A_GROUPS][...] = sum(e / den * o for e, o in zip(es, outs))


def attn_merge_rows(outs, lses):
    spec = pl.BlockSpec(outs[0].shape, lambda i: (0, 0, 0, 0))
    return pl.pallas_call(
        _attn_merge_rows_kernel,
        grid=(1,),
        in_specs=[spec] * (2 * A_GROUPS),
        out_specs=spec,
        out_shape=jax.ShapeDtypeStruct(outs[0].shape, F32),
        compiler_params=_params(1),
        name="attn_merge_rows",
    )(*outs, *lses)


def _attn_merge_prompt_kernel(*refs, tile):
    o_refs, l_refs = refs[:A_GROUPS], refs[A_GROUPS:2 * A_GROUPS]
    a_ref, o_nat, l_nat = refs[2 * A_GROUPS:]
    lses = []
    for g, dil in enumerate(A_DILATIONS):
        if dil == 1:
            lses.append(l_refs[g][0])
            continue
        n = tile // dil
        for r in range(dil):
            rows = pl.ds(r, n, stride=dil)
            l_nat[g, rows, :] = l_refs[g][r]
            for h in range(A_HEADS):
                o_nat[g * A_HEADS + h, rows, :] = o_refs[g][r, :, h * A_HEAD_DIM:(h + 1) * A_HEAD_DIM]
        lses.append(l_nat[g])

    def head_out(g, h):
        if A_DILATIONS[g] == 1:
            return o_refs[g][0, :, h * A_HEAD_DIM:(h + 1) * A_HEAD_DIM]
        return o_nat[g * A_HEADS + h]

    _merge_groups(head_out, lses, a_ref)


def attn_merge_prompt(outs, lses, *, tile=ROW_TILE):
    batch, _, seq, cols = outs[0].shape
    tps = seq // tile
    in_specs = []
    for width in (cols, LANES):
        for dil in A_DILATIONS:
            in_specs.append(pl.BlockSpec((None, dil, tile // dil, width), lambda b, i: (b, 0, i, 0)))
    return pl.pallas_call(
        functools.partial(_attn_merge_prompt_kernel, tile=tile),
        grid=(batch, tps),
        in_specs=in_specs,
        out_specs=pl.BlockSpec((tile, cols), lambda b, i: (b * tps + i, 0)),
        out_shape=jax.ShapeDtypeStruct((batch * seq, cols), BF16),
        scratch_shapes=[pltpu.VMEM((A_GROUPS * A_HEADS, tile, A_HEAD_DIM), F32),
                        pltpu.VMEM((A_GROUPS, tile, LANES), F32)],
        compiler_params=_params(2),
        name="attn_merge_prompt",
    )(*outs, *lses)


def _conv_silu_chunk(first, u_ref, prev_ref, w_ref, bias, full_ref, out_ref, col0):
    c, width = u_ref.shape
    cols = slice(col0, col0 + width)

    @pl.when(first)
    def _():
        full_ref[0:SUBLANES, cols] = prev_ref[...]

    u = u_ref[...]
    full_ref[SUBLANES:SUBLANES + c, cols] = u
    acc = u * w_ref[CONV_TAPS - 1:CONV_TAPS, :]
    if bias is not None:
        acc = acc + bias
    for s in range(1, CONV_TAPS):
        acc = acc + full_ref[SUBLANES - s:SUBLANES - s + c, cols] * w_ref[CONV_TAPS - 1 - s:CONV_TAPS - s, :]
    out_ref[:, cols] = _silu(acc)
    full_ref[0:SUBLANES, cols] = u[c - SUBLANES:c, :]


def _unit_lower_inverses(mats, n, top):
    blk = min(B_INV_BLOCK, top)
    ii = lax.broadcasted_iota(jnp.int32, (n, n), 0)
    jj = lax.broadcasted_iota(jnp.int32, (n, n), 1)
    eye = jnp.where(ii == jj, 1.0, 0.0).astype(F32)
    shift = blk.bit_length() - 1
    same = (ii >> shift) == (jj >> shift)
    ps = [jnp.where(same, -a, 0.0) for a in mats]
    xs = [eye + p for p in ps]
    for _ in range(shift - 1):
        ps = [_mm(p, p, B_INV_PASSES) for p in ps]
        xs = [x + _mm(x, p, B_INV_PASSES) for x, p in zip(xs, ps)]
    size = blk
    while size < top:
        shift += 1
        same_next = (ii >> shift) == (jj >> shift)
        sel = jnp.logical_and(same_next, jnp.logical_not(same))
        ys = [_mm(x, jnp.where(sel, a, 0.0), B_INV_PASSES) for x, a in zip(xs, mats)]
        xs = [x - _mm(y, x, B_INV_PASSES) for x, y in zip(xs, ys)]
        same = same_next
        size *= 2
    return xs


def _gdn_kernel(qr_ref, kr_ref, vr_ref, qp_ref, kp_ref, vp_ref, qw_ref, kw_ref, vw_ref,
                z_ref, ba_ref, alog_ref, dtb_ref, nw_ref, s0_ref,
                o_ref, s_ref, full_ref, conv_ref, *, c, hps, t_valid):
    hb = pl.program_id(1)
    ci = pl.program_id(2)
    dk = B_HEAD_DIM
    c2 = 2 * c
    assert c2 == LANES

    @pl.when(ci == 0)
    def _():
        s_ref[...] = s0_ref[...]

    qw = hps // 2 * dk
    for raw, prev, taps, col0 in ((qr_ref, qp_ref, qw_ref, 0), (kr_ref, kp_ref, kw_ref, qw),
                                  (vr_ref, vp_ref, vw_ref, 2 * qw)):
        _conv_silu_chunk(ci == 0, raw, prev, taps, None, full_ref, conv_ref, col0)
    q_ref = conv_ref.at[:, 0:qw]
    k_ref = conv_ref.at[:, qw:2 * qw]
    v_ref = conv_ref.at[:, 2 * qw:]

    ba = ba_ref[...]
    valid = lax.broadcasted_iota(jnp.int32, (c, LANES), 0) + ci * c < t_valid
    beta_all = jnp.where(valid, jax.nn.sigmoid(ba), 0.0)
    g_all = jnp.where(valid, -jnp.exp(alog_ref[pl.ds(hb, 1), :]) * _softplus(ba + dtb_ref[pl.ds(hb, 1), :]), 0.0)
    ci_, cj_ = lax.broadcasted_iota(jnp.int32, (c, c), 0), lax.broadcasted_iota(jnp.int32, (c, c), 1)
    gc_all = _select_rows(jnp.where(ci_ >= cj_, 1.0, 0.0).astype(BF16), g_all)
    gc_t = jnp.concatenate([gc_all, gc_all], axis=0).T
    ii = lax.broadcasted_iota(jnp.int32, (c2, c2), 0)
    jj = lax.broadcasted_iota(jnp.int32, (c2, c2), 1)
    shift = c.bit_length() - 1
    same = (ii >> shift) == (jj >> shift)
    incl = jnp.logical_and(same, ii >= jj)
    strict = jnp.logical_and(same, ii > jj)
    first = lax.broadcasted_iota(jnp.int32, (1, c2), 1) < c
    top = lax.broadcasted_iota(jnp.int32, (c2, 1), 0) < c
    nw = nw_ref[...]

    def stack_cols(arr, l0, l1):
        return jnp.concatenate([arr[:, l0:l0 + 1], arr[:, l1:l1 + 1]], axis=0)

    def own_half(r):
        return jnp.where(top, r[:, :dk], r[:, dk:])

    pairs = range(hps // 2)
    cols = lambda h: slice(h * dk, (h + 1) * dk)

    def prepare(qh):
        h0, h1 = 2 * qh, 2 * qh + 1
        q = q_ref[:, cols(qh)]
        k = k_ref[:, cols(qh)]
        q = q * lax.rsqrt(jnp.sum(q * q, axis=-1, keepdims=True) + 1e-6) * (dk ** -0.5)
        k = k * lax.rsqrt(jnp.sum(k * k, axis=-1, keepdims=True) + 1e-6)
        q2 = jnp.concatenate([q, q], axis=0)
        k2 = jnp.concatenate([k, k], axis=0)
        k2_b = k2.astype(BF16)
        beta = stack_cols(beta_all, h0, h1)
        gc = stack_cols(gc_all, hps + h0, hps + h1)
        gr = jnp.where(first, gc_t[hps + h0:hps + h0 + 1, :], gc_t[hps + h1:hps + h1 + 1, :])
        gl0 = gc_all[c - 1:c, hps + h0:hps + h0 + 1]
        gl1 = gc_all[c - 1:c, hps + h1:hps + h1 + 1]
        decay = jnp.exp(jnp.where(incl, gc - gr, NEG_BIG))
        eg = jnp.exp(gc)
        v2 = jnp.concatenate([v_ref[:, cols(h0)], v_ref[:, cols(h1)]], axis=0)
        return dict(
            a=jnp.where(strict, _dot_nt(k2_b, k2_b) * decay, 0.0) * beta,
            qk=jnp.where(incl, _dot_nt(q2.astype(BF16), k2_b) * decay, 0.0).astype(BF16),
            rhs=jnp.concatenate([v2 * beta, k2 * (beta * eg)], axis=1),
            q_eg=(q2 * eg).astype(BF16),
            k_dec=(k2 * jnp.exp(jnp.where(top, gl0, gl1) - gc)).astype(BF16),
            dec_cat=jnp.concatenate([jnp.broadcast_to(jnp.exp(gl0), (1, dk)),
                                     jnp.broadcast_to(jnp.exp(gl1), (1, dk))], axis=1))

    st = [prepare(qh) for qh in pairs]
    t_inv = _unit_lower_inverses([s["a"] for s in st], c2, c)
    uw = [_mm(t, s["rhs"], B_INV_PASSES) for t, s in zip(t_inv, st)]
    s_cat = [jnp.concatenate([s_ref[2 * qh], s_ref[2 * qh + 1]], axis=1) for qh in pairs]
    s_cat_b = [s.astype(BF16) for s in s_cat]
    v_new = [r[:, :dk] - own_half(_dot(r[:, dk:].astype(BF16), sb)) for r, sb in zip(uw, s_cat_b)]
    v_new_b = [v.astype(BF16) for v in v_new]
    outs = [own_half(_dot(s["q_eg"], sb)) + _dot(s["qk"], vb) for s, sb, vb in zip(st, s_cat_b, v_new_b)]
    for qh in pairs:
        h0, h1 = 2 * qh, 2 * qh + 1
        v = v_new[qh]
        v_blk = jnp.concatenate([jnp.where(top, v, 0.0), jnp.where(top, 0.0, v)], axis=1).astype(BF16)
        s_new = s_cat[qh] * st[qh]["dec_cat"] + _dot_tn(st[qh]["k_dec"], v_blk)
        s_ref[h0] = s_new[:, :dk]
        s_ref[h1] = s_new[:, dk:]
        o = outs[qh]
        o = o * lax.rsqrt(jnp.mean(o * o, axis=-1, keepdims=True) + RMS_EPS) * nw
        o = (o * _silu(jnp.concatenate([z_ref[:, cols(h0)], z_ref[:, cols(h1)]], axis=0))).astype(o_ref.dtype)
        o_ref[:, cols(h0)] = o[:c]
        o_ref[:, cols(h1)] = o[c:]


def gdn_scan(proj, conv_prev, conv_taps, ba, alog_pad, dtb_pad, norm_w, s0, *, nseq, seq, t_valid):
    c = B_CHUNK
    hps = B_HEADS_PER_STEP
    nc = seq // c
    ng = B_V_HEADS // hps
    qw = hps // 2 * B_HEAD_DIM
    vw = hps * B_HEAD_DIM
    par_rows = alog_pad.shape[0]
    row = lambda s, h, ci: s * nc + ci
    k0, v0, z0 = B_KEY_DIM // qw, 2 * B_KEY_DIM // vw, B_CONV_DIM // vw
    chunk = lambda width, col0: pl.BlockSpec((c, width), lambda s, h, ci: (row(s, h, ci), col0 + h))
    state = lambda width, col0: pl.BlockSpec((None, SUBLANES, width), lambda s, h, ci: (s, 0, col0 + h))
    taps = lambda width, col0: pl.BlockSpec((SUBLANES, width), lambda s, h, ci: (0, col0 + h))
    params = pl.BlockSpec((par_rows, LANES), lambda s, h, ci: (0, 0))
    heads = pl.BlockSpec((None, hps, B_HEAD_DIM, B_HEAD_DIM), lambda s, h, ci: (s, h, 0, 0))
    return pl.pallas_call(
        functools.partial(_gdn_kernel, c=c, hps=hps, t_valid=t_valid),
        grid=(nseq, ng, nc),
        in_specs=[chunk(qw, 0), chunk(qw, k0), chunk(vw, v0),
                  state(qw, 0), state(qw, k0), state(vw, v0),
                  taps(qw, 0), taps(qw, k0), taps(vw, v0),
                  chunk(vw, z0), chunk(LANES, 0), params, params,
                  pl.BlockSpec((1, B_HEAD_DIM), lambda s, h, ci: (0, 0)), heads],
        out_specs=[chunk(vw, 0), heads],
        out_shape=[jax.ShapeDtypeStruct((nseq * seq, B_VAL_DIM), BF16),
                   jax.ShapeDtypeStruct((nseq, B_V_HEADS, B_HEAD_DIM, B_HEAD_DIM), F32)],
        scratch_shapes=[pltpu.VMEM((SUBLANES + c, 2 * qw + vw), F32),
                        pltpu.VMEM((c, 2 * qw + vw), F32)],
        compiler_params=_params(3),
        name="gdn_scan",
    )(proj, proj, proj, conv_prev, conv_prev, conv_prev, conv_taps, conv_taps, conv_taps,
      proj, ba, alog_pad, dtb_pad, norm_w, s0)


def _ssd_kernel(xr_ref, br_ref, cr_ref, xp_ref, bp_ref, cp_ref, xw_ref, bw_ref, cw_ref,
                xbias_ref, bbias_ref, cbias_ref, z_ref, dt_ref, dtb_ref, alog_ref, dskip_ref, nw_ref,
                h0_ref, y_ref, hout_ref, ht_ref, full_ref, conv_ref, *, c, nc, t_valid):
    g = pl.program_id(1)
    ci = pl.program_id(2)
    hpg = C_HEADS_PER_GROUP
    p = C_HEAD_DIM
    gcols = hpg * p

    @pl.when(ci == 0)
    def _():
        ht_ref[...] = h0_ref[...].reshape(gcols, C_D_STATE).T

    for raw, prev, taps, bias, col0 in ((xr_ref, xp_ref, xw_ref, xbias_ref, 0),
                                        (br_ref, bp_ref, bw_ref, bbias_ref, gcols),
                                        (cr_ref, cp_ref, cw_ref, cbias_ref, gcols + C_D_STATE)):
        _conv_silu_chunk(ci == 0, raw, prev, taps, bias[...], full_ref, conv_ref, col0)
    x_ref = conv_ref.at[:, 0:gcols]
    b_ref = conv_ref.at[:, gcols:gcols + C_D_STATE]
    c_ref = conv_ref.at[:, gcols + C_D_STATE:]

    row = lax.broadcasted_iota(jnp.int32, (c, LANES), 0) + ci * c
    dt = jnp.where(row < t_valid, _softplus(dt_ref[...] + dtb_ref[pl.ds(g, 1), :]), 0.0)
    la = dt * -jnp.exp(alog_ref[pl.ds(g, 1), :])
    ii = lax.broadcasted_iota(jnp.int32, (c, c), 0)
    jj = lax.broadcasted_iota(jnp.int32, (c, c), 1)
    incl = ii >= jj
    acs = _select_rows(jnp.where(incl, 1.0, 0.0).astype(BF16), la)
    acs_t = acs.T
    el = lax.broadcasted_iota(jnp.int32, (LANES, gcols), 0)
    ej = lax.broadcasted_iota(jnp.int32, (LANES, gcols), 1)
    expand = jnp.where(el == (ej >> (p.bit_length() - 1)), 1.0, 0.0).astype(BF16)
    dt_e = _select_cols(dt, expand)
    acs_e = _select_cols(acs, expand)
    dskip_e = _select_cols(jnp.broadcast_to(dskip_ref[pl.ds(g, 1), :], (SUBLANES, LANES)), expand)[0:1, :]

    x = x_ref[...]
    bm = b_ref[...].astype(BF16)
    cm = c_ref[...].astype(BF16)
    xdt = (x * dt_e).astype(BF16)
    cb = _dot_nt(cm, bm)
    ht = ht_ref[...]
    y = _dot(cm, ht.astype(BF16)) * jnp.exp(acs_e) + dskip_e * x
    lane = lax.broadcasted_iota(jnp.int32, (c, 2 * p), 1)
    decay = [jnp.exp(jnp.where(incl, acs[:, hd:hd + 1] - acs_t[hd:hd + 1, :], NEG_BIG)) for hd in range(hpg)]
    mats = [(cb * dec).astype(BF16) for dec in decay]
    ys = [_dot(mats[hd], xdt[:, (hd // 2) * 2 * p:(hd // 2 + 1) * 2 * p]) for hd in range(hpg)]
    y = y + jnp.concatenate([jnp.where(lane < p, ys[2 * j], ys[2 * j + 1]) for j in range(hpg // 2)], axis=1)
    y = y * _silu(z_ref[...])
    y = y * lax.rsqrt(jnp.mean(y * y, axis=-1, keepdims=True) + RMS_EPS) * nw_ref[...]
    y_ref[...] = y.astype(y_ref.dtype)

    last_e = acs_e[c - 1:c, :]
    xw = (x * (jnp.exp(last_e - acs_e) * dt_e)).astype(BF16)
    ht_new = ht * jnp.exp(last_e) + _dot_tn(bm, xw)
    ht_ref[...] = ht_new

    @pl.when(ci == nc - 1)
    def _():
        hout_ref[...] = ht_new.T.reshape(hpg, p, C_D_STATE)


def ssd_scan(proj, conv_prev, conv_taps, conv_bias, dt, dtb_pad, alog_pad, dskip_pad, norm_w, h0,
             *, nseq, seq, t_valid):
    c = C_CHUNK
    nc = seq // c
    gc = C_GROUP_COLS
    n = C_D_STATE
    row = lambda s, g, ci: s * nc + ci
    px, pb_, pc_ = C_D_INNER // gc, 2 * C_D_INNER // n, 2 * C_D_INNER // n + C_GROUPS
    cb_, cc_ = C_D_INNER // n, C_D_INNER // n + C_GROUPS
    chunk = lambda width, col0: pl.BlockSpec((c, width), lambda s, g, ci: (row(s, g, ci), col0 + g))
    state = lambda width, col0: pl.BlockSpec((None, SUBLANES, width), lambda s, g, ci: (s, 0, col0 + g))
    taps = lambda width, col0: pl.BlockSpec((SUBLANES, width), lambda s, g, ci: (0, col0 + g))
    bias = lambda width, col0: pl.BlockSpec((1, width), lambda s, g, ci: (0, col0 + g))
    par = pl.BlockSpec((C_GROUPS, LANES), lambda s, g, ci: (0, 0))
    heads = pl.BlockSpec((None, C_HEADS_PER_GROUP, C_HEAD_DIM, n), lambda s, g, ci: (s, g, 0, 0))
    width = gc + 2 * n
    return pl.pallas_call(
        functools.partial(_ssd_kernel, c=c, nc=nc, t_valid=t_valid),
        grid=(nseq, C_GROUPS, nc),
        in_specs=[chunk(gc, px), chunk(n, pb_), chunk(n, pc_),
                  state(gc, 0), state(n, cb_), state(n, cc_),
                  taps(gc, 0), taps(n, cb_), taps(n, cc_),
                  bias(gc, 0), bias(n, cb_), bias(n, cc_),
                  chunk(gc, 0), chunk(LANES, 0), par, par, par, bias(gc, 0), heads],
        out_specs=[chunk(gc, 0), heads],
        out_shape=[jax.ShapeDtypeStruct((nseq * seq, C_D_INNER), BF16),
                   jax.ShapeDtypeStruct((nseq, C_HEADS, C_HEAD_DIM, n), F32)],
        scratch_shapes=[pltpu.VMEM((n, gc), F32),
                        pltpu.VMEM((SUBLANES + c, width), F32),
                        pltpu.VMEM((c, width), F32)],
        compiler_params=_params(3),
        name="ssd_scan",
    )(proj, proj, proj, conv_prev, conv_prev, conv_prev, conv_taps, conv_taps, conv_taps,
      conv_bias, conv_bias, conv_bias, proj, dt, dtb_pad, alog_pad, dskip_pad, norm_w, h0)


def _rope_tables(mp, seq, n_sample, ts):
    pos = jnp.concatenate([jnp.arange(mp) % seq, PAST_LEN + jnp.arange(n_sample) % ts]).astype(F32)
    inv_freq = ROPE_THETA ** (-jnp.arange(A_ROT_HALF, dtype=F32) / A_ROT_HALF)
    ang = pos[:, None] * inv_freq[None, :]
    cos, sin = jnp.cos(ang), jnp.sin(ang)
    rest = A_HEAD_DIM - 2 * A_ROT_HALF
    rows = pos.shape[0]
    cos_full = jnp.concatenate([cos, cos, jnp.ones((rows, rest), F32)], axis=1)
    sin_full = jnp.concatenate([-sin, sin, jnp.zeros((rows, rest), F32)], axis=1)
    return cos_full, sin_full


def _lane_groups(cols, group):
    lead = cols.shape[:-1]
    n = cols.shape[-1]
    c = cols.reshape(*lead, n // group, group)
    c = jnp.pad(c, [(0, 0)] * (c.ndim - 1) + [(0, LANES - group)])
    return c.reshape(*lead, n // group * LANES)


def _pad_seq_rows(x, nseq, ts, seq):
    cols = x.shape[-1]
    x = jnp.pad(x.reshape(nseq, ts, cols), ((0, 0), (0, seq - ts), (0, 0)))
    return x.reshape(nseq * seq, cols)


def _conv_prev(state):
    return jnp.pad(state, ((0, 0), (SUBLANES - state.shape[1], 0), (0, 0)))


def _conv_taps(w):
    return jnp.pad(w, ((0, SUBLANES - w.shape[0]), (0, 0)))


def _last_rows(x, nseq, seq, n, col0, cols):
    return jnp.stack([x[(s + 1) * seq - n:(s + 1) * seq, col0:col0 + cols] for s in range(nseq)])


def kernel(x_prompt, x_sample, cache_a_kv_w128, cache_a_kv_w512, cache_a_kv_w2048, state_b_ssm, state_b_conv, state_c_ssm, state_c_conv, ln_g, ln_b, ffn1_w_gu, ffn1_w_down, ffn2_w_gu, ffn2_w_down, a_w_qkv, a_w_o, b_w_in, b_conv_w, b_a_log, b_dt_bias, b_norm_w, b_w_out, c_w_in, c_conv_w, c_conv_b, c_dt_bias, c_a_log, c_d, c_norm_w, c_w_out):
    a_bufs = (cache_a_kv_w128, cache_a_kv_w512, cache_a_kv_w2048)
    bp, seq, d = x_prompt.shape
    db, ts, _ = x_sample.shape
    mp, ms = bp * seq, db * ts
    assert seq % (A_BLOCK * max(A_DILATIONS)) == 0 and mp % ms == 0
    x = jnp.concatenate([x_prompt.reshape(mp, d), x_sample.reshape(ms, d)], axis=0)
    xb = x.astype(BF16)
    cos_full, sin_full = _rope_tables(mp, seq, ms, ts)
    cos2 = jnp.stack([cos_full, jnp.ones_like(cos_full)])
    sin2 = jnp.stack([sin_full, jnp.zeros_like(sin_full)])
    wb16 = {name: w.astype(BF16) for name, w in dict(
        ffn1_down=ffn1_w_down, ffn2_down=ffn2_w_down, a_qkv=a_w_qkv, a_o=a_w_o,
        b_in=b_w_in[:, :, :B_CONV_DIM + B_VAL_DIM], b_out=b_w_out,
        c_in=c_w_in[:, :, :C_D_INNER + C_CONV_DIM], c_out=c_w_out).items()}
    w_gu = dict(ffn1=ffn1_w_gu, ffn2=ffn2_w_gu)

    def valid_rows(a_s, rows):
        return a_s.reshape(db, rows, -1)[:, :ts].reshape(ms, -1)

    outs = {}

    def mixer_a(xb, j):
        qkv_s = matmul(xb[mp:], wb16["a_qkv"], layer=j, n=a_w_qkv.shape[2], tn=1024, tm=ms)
        gw = A_GROUP_COLS
        o_p, l_p, o_s, l_s = [], [], [], []
        for g, dil in enumerate(A_DILATIONS):
            qkv_r = qkv_project(xb, wb16["a_qkv"], cos2, sin2, layer=j, g=g, batch=bp, seq=seq)
            o, lse = attn_prompt(qkv_r)
            o_p.append(o)
            l_p.append(lse)
            new = qkv_split(qkv_s, cos_full[mp:], sin_full[mp:], g=g, dil=1, row0=0, batch=1, seq=ms, tile=ms)
            q_s, k_s, v_s = (new[0, 0, :, sec * gw:(sec + 1) * gw].reshape(db, ts, A_HEADS, A_HEAD_DIM)
                             for sec in range(A_SECTIONS))
            o, lse = attn_sample(q_s, k_s, v_s, a_bufs[g], layer=j, dil=dil)
            o_s.append(o)
            l_s.append(lse)
            length = seq // dil
            tail = qkv_r[:, :, length - A_BLOCK:, gw:].transpose(0, 2, 1, 3)
            outs.setdefault("a_p%d" % g, []).append(tail.reshape(bp, A_BLOCK * dil, 2, A_HEADS, A_HEAD_DIM))
            outs.setdefault("a_new%d" % g, []).append(jnp.stack([k_s, v_s], axis=2))
        a_s = attn_merge_rows(o_s, l_s).reshape(ms, gw).astype(BF16)
        return attn_merge_prompt(o_p, l_p), a_s, wb16["a_o"]

    def mixer_b(xb, j):
        hps = B_HEADS_PER_STEP
        ng = B_V_HEADS // hps
        w_in = b_w_in[j]
        main = B_CONV_DIM + B_VAL_DIM
        pb = matmul(xb, wb16["b_in"], layer=j, n=main, tn=1024)
        w_tail = jnp.concatenate([w_in[:, main:main + B_V_HEADS].reshape(d, ng, hps),
                                  w_in[:, main + B_V_HEADS:].reshape(d, ng, hps)], axis=2)
        ba = matmul(xb, _lane_groups(w_tail.reshape(d, -1), 2 * hps).astype(BF16)[None],
                    layer=0, n=ng * LANES, tn=ng * LANES)
        gate_lanes = lambda v: jnp.pad(_lane_groups(
            jnp.concatenate([jnp.zeros((ng, hps), F32), v.reshape(ng, hps)], axis=1).reshape(-1), 2 * hps
        ).reshape(ng, LANES), ((0, SUBLANES - ng), (0, 0)))
        alog_pad, dtb_pad = gate_lanes(b_a_log[j]), gate_lanes(b_dt_bias[j])
        taps = _conv_taps(b_conv_w[j])
        nw = b_norm_w[j].reshape(1, B_HEAD_DIM)
        a_p, s_p = gdn_scan(pb, jnp.zeros((bp, SUBLANES, B_CONV_DIM), F32), taps, ba, alog_pad, dtb_pad, nw,
                            jnp.zeros((bp, B_V_HEADS, B_HEAD_DIM, B_HEAD_DIM), F32),
                            nseq=bp, seq=seq, t_valid=seq)
        pb_s = _pad_seq_rows(pb[mp:], db, ts, B_CHUNK)
        ba_s = _pad_seq_rows(ba[mp:], db, ts, B_CHUNK)
        a_s, s_s = gdn_scan(pb_s, _conv_prev(state_b_conv[j]), taps, ba_s, alog_pad, dtb_pad, nw, state_b_ssm[j],
                            nseq=db, seq=B_CHUNK, t_valid=ts)
        outs.setdefault("b_ssm_p", []).append(s_p)
        outs.setdefault("b_ssm_s", []).append(s_s)
        outs.setdefault("b_conv_p", []).append(_last_rows(pb, bp, seq, CONV_TAPS - 1, 0, B_CONV_DIM))
        outs.setdefault("b_conv_s", []).append(_last_rows(pb[mp:], db, ts, CONV_TAPS - 1, 0, B_CONV_DIM))
        return a_p, valid_rows(a_s, B_CHUNK), wb16["b_out"]

    def mixer_c(xb, j):
        w_in = c_w_in[j]
        main = C_D_INNER + C_CONV_DIM
        pc = matmul(xb, wb16["c_in"], layer=j, n=main, tn=1024)
        dt = matmul(xb, _lane_groups(w_in[:, main:], C_HEADS_PER_GROUP).astype(BF16)[None],
                    layer=0, n=C_GROUPS * LANES, tn=C_GROUPS * LANES)
        head_lanes = lambda v: _lane_groups(v, C_HEADS_PER_GROUP).reshape(C_GROUPS, LANES)
        dtb_pad, alog_pad, dskip_pad = head_lanes(c_dt_bias[j]), head_lanes(c_a_log[j]), head_lanes(c_d[j])
        taps = _conv_taps(c_conv_w[j])
        bias = c_conv_b[j].reshape(1, C_CONV_DIM)
        nw = c_norm_w[j].reshape(1, C_D_INNER)
        a_p, h_p = ssd_scan(pc, jnp.zeros((bp, SUBLANES, C_CONV_DIM), F32), taps, bias, dt, dtb_pad, alog_pad,
                            dskip_pad, nw, jnp.zeros((bp, C_HEADS, C_HEAD_DIM, C_D_STATE), F32),
                            nseq=bp, seq=seq, t_valid=seq)
        pc_s = _pad_seq_rows(pc[mp:], db, ts, C_CHUNK)
        dt_s = _pad_seq_rows(dt[mp:], db, ts, C_CHUNK)
        a_s, h_s = ssd_scan(pc_s, _conv_prev(state_c_conv[j]), taps, bias, dt_s, dtb_pad, alog_pad, dskip_pad, nw,
                            state_c_ssm[j], nseq=db, seq=C_CHUNK, t_valid=ts)
        outs.setdefault("c_ssm_p", []).append(h_p)
        outs.setdefault("c_ssm_s", []).append(h_s)
        outs.setdefault("c_conv_p", []).append(_last_rows(pc, bp, seq, CONV_TAPS - 1, C_D_INNER, C_CONV_DIM))
        outs.setdefault("c_conv_s", []).append(_last_rows(pc[mp:], db, ts, CONV_TAPS - 1, C_D_INNER, C_CONV_DIM))
        return a_p, valid_rows(a_s, C_CHUNK), wb16["c_out"]

    def ffn(x, xb, which, i, g, b):
        h = swiglu_up(xb, w_gu[which], layer=i)
        return matmul_postnorm(h, wb16[which + "_down"], x, g, b, layer=i, scale=0.5)

    mixers = (mixer_a, mixer_b, mixer_c)
    for i in range(DEPTH):
        x, xb = ffn(x, xb, "ffn1", i, ln_g[i, 0], ln_b[i, 0])
        a_p, a_s, w_out = mixers[i % 3](xb, i // 3)
        x, xb = matmul_postnorm(a_p, w_out, x, ln_g[i, 1], ln_b[i, 1], layer=i // 3, scale=1.0, a_tail=a_s)
        x, xb = ffn(x, xb, "ffn2", i, ln_g[i, 2], ln_b[i, 2])

    st = lambda name: jnp.stack(outs[name])
    def shifted(buf, new):
        moved = lax.pad(buf, jnp.zeros((), buf.dtype), [(0, 0, 0), (0, 0, 0), (-ts, ts, 0)] + [(0, 0, 0)] * 3)
        return lax.dynamic_update_slice(moved, new, (0, 0, buf.shape[2] - ts, 0, 0, 0))

    a_s = [shifted(a_bufs[g], st("a_new%d" % g)) for g in range(A_GROUPS)]
    return (x[:mp].reshape(bp, seq, d), x[mp:].reshape(db, ts, d),
            st("a_p0"), a_s[0], st("a_p1"), a_s[1], st("a_p2"), a_s[2],
            st("b_ssm_p"), st("b_ssm_s"), st("b_conv_p"), st("b_conv_s"),
            st("c_ssm_p"), st("c_ssm_s"), st("c_conv_p"), st("c_conv_s"))
```

```python
import functools

import jax
import jax.numpy as jnp
from jax import lax
from jax.experimental import pallas as pl
from jax.experimental.pallas import tpu as pltpu

F32 = jnp.float32
BF16 = jnp.bfloat16

VMEM_LIMIT_BYTES = 56 * 1024 * 1024
LANES = 128
SUBLANES = 8
ROW_TILE = 512
POSTNORM_ROW_TILE = 256
NEG_BIG = -1e30

DEPTH = 4
DN_ALPHA = (2.0 * DEPTH) ** 0.25
LN_EPS = 1e-5
RMS_EPS = 1e-6
PAST_LEN = 16384

A_DILATIONS = (1, 4, 16)
A_GROUPS = 3
A_HEADS = 8
A_HEAD_DIM = 128
A_ROT_HALF = A_HEAD_DIM // 8
A_BLOCK = 128
A_GROUP_COLS = A_HEADS * A_HEAD_DIM
A_SECTIONS = 3
A_SPLIT_TILE = 2048
A_PROJECT_PARTS = 2
ROPE_THETA = 500000.0

B_QK_HEADS = 16
B_V_HEADS = 32
B_HEAD_DIM = 128
B_KEY_DIM = B_QK_HEADS * B_HEAD_DIM
B_VAL_DIM = B_V_HEADS * B_HEAD_DIM
B_CONV_DIM = 2 * B_KEY_DIM + B_VAL_DIM
B_CHUNK = 64
B_HEADS_PER_STEP = 16
B_INV_BLOCK = 16
B_INV_PASSES = 1

C_D_INNER = 4096
C_HEADS = 64
C_HEAD_DIM = 64
C_GROUPS = 8
C_HEADS_PER_GROUP = C_HEADS // C_GROUPS
C_GROUP_COLS = C_D_INNER // C_GROUPS
C_D_STATE = 128
C_CONV_DIM = C_D_INNER + 2 * C_GROUPS * C_D_STATE
C_CHUNK = 128

CONV_TAPS = 4


def _params(n_axes):
    return pltpu.CompilerParams(
        dimension_semantics=("arbitrary",) * n_axes,
        vmem_limit_bytes=VMEM_LIMIT_BYTES,
    )


def _dot(a, b):
    return jnp.dot(a, b, preferred_element_type=F32)


def _dot_nt(a, b):
    return lax.dot_general(a, b, (((1,), (1,)), ((), ())), preferred_element_type=F32)


def _dot_tn(a, b):
    return lax.dot_general(a, b, (((0,), (0,)), ((), ())), preferred_element_type=F32)


def _split3(x):
    hi = x.astype(BF16)
    r = x - hi.astype(F32)
    mid = r.astype(BF16)
    lo = (r - mid.astype(F32)).astype(BF16)
    return hi, mid, lo


def _select_rows(sel, x):
    return sum(_dot(sel, part) for part in _split3(x))


def _select_cols(x, sel):
    return sum(_dot(part, sel) for part in _split3(x))


def _mm(a, b, passes):
    a_hi, b_hi = a.astype(BF16), b.astype(BF16)
    out = _dot(a_hi, b_hi)
    if passes == 3:
        a_lo = (a - a_hi.astype(F32)).astype(BF16)
        b_lo = (b - b_hi.astype(F32)).astype(BF16)
        out = out + _dot(a_hi, b_lo) + _dot(a_lo, b_hi)
    return out


def _silu(x):
    return x * jax.nn.sigmoid(x)


def _softplus(x):
    return jnp.maximum(x, 0.0) + jnp.log(1.0 + jnp.exp(-jnp.abs(x)))


def _row_tiles(rows, tm):
    n_full, rem = divmod(rows, tm)
    return n_full + (1 if rem else 0), n_full, rem


def _per_row_tile(i, n_full, rem, tm, body):
    if rem == 0:
        body(tm)
        return

    @pl.when(i < n_full)
    def _():
        body(tm)

    @pl.when(i == n_full)
    def _():
        body(rem)


def _row_split(rows, tm):
    n_full, tail = divmod(rows, tm)
    assert tail > 0 and (n_full * tm) % tail == 0
    return n_full, tail


def _mm_kernel(x_ref, xt_ref, w_ref, o_ref, ot_ref):
    @pl.when(pl.program_id(1) == 0)
    def _():
        ot_ref[...] = _dot(xt_ref[...], w_ref[...]).astype(ot_ref.dtype)

    o_ref[...] = _dot(x_ref[...], w_ref[...]).astype(o_ref.dtype)


def matmul(x, w, *, layer, n, tn, tm=2 * ROW_TILE):
    rows, k = x.shape
    n_full, tail = _row_split(rows, tm)
    tail_block = n_full * tm // tail
    return pl.pallas_call(
        _mm_kernel,
        grid=(n // tn, n_full),
        in_specs=[pl.BlockSpec((tm, k), lambda j, i: (i, 0)),
                  pl.BlockSpec((tail, k), lambda j, i: (tail_block, 0)),
                  pl.BlockSpec((None, k, tn), lambda j, i: (layer, 0, j))],
        out_specs=[pl.BlockSpec((tm, tn), lambda j, i: (i, j)),
                   pl.BlockSpec((tail, tn), lambda j, i: (0, j))],
        out_shape=[jax.ShapeDtypeStruct((n_full * tm, n), F32),
                   jax.ShapeDtypeStruct((tail, n), F32)],
        compiler_params=_params(2),
        name="matmul",
    )(x, x, w)


def _mm_tail_kernel(x_ref, w_ref, o_ref):
    o_ref[...] = _dot(x_ref[...], w_ref[...])


def matmul_tail(x, w, *, layer, n, tn, tm=2 * ROW_TILE):
    rows, k = x.shape
    n_full, tail = _row_split(rows, tm)
    tail_block = n_full * tm // tail
    return pl.pallas_call(
        _mm_tail_kernel,
        grid=(n // tn,),
        in_specs=[pl.BlockSpec((tail, k), lambda j: (tail_block, 0)),
                  pl.BlockSpec((None, k, tn), lambda j: (layer, 0, j))],
        out_specs=pl.BlockSpec((tail, tn), lambda j: (0, j)),
        out_shape=jax.ShapeDtypeStruct((tail, n), F32),
        compiler_params=_params(1),
        name="matmul_tail",
    )(x, w)


def _swiglu_kernel(x_ref, xt_ref, wg_ref, wu_ref, o_ref, ot_ref, wgb_ref, wub_ref):
    def hidden(x):
        return (_silu(_dot(x, wgb_ref[...])) * _dot(x, wub_ref[...])).astype(o_ref.dtype)

    @pl.when(pl.program_id(1) == 0)
    def _():
        wgb_ref[...] = wg_ref[...].astype(BF16)
        wub_ref[...] = wu_ref[...].astype(BF16)
        ot_ref[...] = hidden(xt_ref[...])

    o_ref[...] = hidden(x_ref[...])


def swiglu_up(x, w_gu, *, layer, tn=512, tm=2 * ROW_TILE):
    rows, k = x.shape
    f = w_gu.shape[2] // 2
    n_full, tail = _row_split(rows, tm)
    tail_block = n_full * tm // tail
    nj = f // tn
    return pl.pallas_call(
        _swiglu_kernel,
        grid=(nj, n_full),
        in_specs=[pl.BlockSpec((tm, k), lambda j, i: (i, 0)),
                  pl.BlockSpec((tail, k), lambda j, i: (tail_block, 0)),
                  pl.BlockSpec((None, k, tn), lambda j, i: (layer, 0, j)),
                  pl.BlockSpec((None, k, tn), lambda j, i: (layer, 0, j + nj))],
        out_specs=[pl.BlockSpec((tm, tn), lambda j, i: (i, j)),
                   pl.BlockSpec((tail, tn), lambda j, i: (0, j))],
        out_shape=[jax.ShapeDtypeStruct((n_full * tm, f), BF16),
                   jax.ShapeDtypeStruct((tail, f), BF16)],
        scratch_shapes=[pltpu.VMEM((k, tn), BF16), pltpu.VMEM((k, tn), BF16)],
        compiler_params=_params(2),
        name="swiglu_up",
    )(x, x, w_gu, w_gu)


def _postnorm_kernel(a_ref, *refs, scale, tm, n_full, rem, split):
    tail_ref = refs[0] if split else a_ref
    w_ref, x_ref, g_ref, b_ref, y_ref, yb_ref = refs[1:] if split else refs

    def body(r):
        a = a_ref[...] if r == tm else tail_ref[:r, :]
        y = DN_ALPHA * x_ref[:r, :] + scale * _dot(a, w_ref[...])
        mu = jnp.mean(y, axis=-1, keepdims=True)
        yc = y - mu
        var = jnp.mean(yc * yc, axis=-1, keepdims=True)
        out = yc * lax.rsqrt(var + LN_EPS) * g_ref[...] + b_ref[...]
        y_ref[:r, :] = out
        yb_ref[:r, :] = out.astype(BF16)

    _per_row_tile(pl.program_id(0), n_full, rem, tm, body)


def matmul_postnorm(a, w, x, g, b, *, layer, scale, a_tail=None, tm=POSTNORM_ROW_TILE):
    rows, d = x.shape
    kdim = a.shape[1]
    steps, n_full, rem = _row_tiles(rows, tm)
    split = a_tail is not None
    assert not split or (a.shape[0] == n_full * tm and a_tail.shape[0] == rem)
    a_specs = [pl.BlockSpec((tm, kdim), lambda i: (jnp.minimum(i, n_full - 1) if split else i, 0))]
    if split:
        a_specs.append(pl.BlockSpec((rem, kdim), lambda i: (0, 0)))
    return pl.pallas_call(
        functools.partial(_postnorm_kernel, scale=scale, tm=tm, n_full=n_full, rem=rem, split=split),
        grid=(steps,),
        in_specs=a_specs + [
            pl.BlockSpec((None, kdim, d), lambda i: (layer, 0, 0), pipeline_mode=pl.Buffered(1)),
            pl.BlockSpec((tm, d), lambda i: (i, 0)),
            pl.BlockSpec((1, d), lambda i: (0, 0)),
            pl.BlockSpec((1, d), lambda i: (0, 0))],
        out_specs=[pl.BlockSpec((tm, d), lambda i: (i, 0)),
                   pl.BlockSpec((tm, d), lambda i: (i, 0))],
        out_shape=[jax.ShapeDtypeStruct((rows, d), F32),
                   jax.ShapeDtypeStruct((rows, d), BF16)],
        compiler_params=_params(1),
        name="matmul_postnorm",
    )(a, *([a_tail] if split else []), w, x, g.reshape(1, d), b.reshape(1, d))


def _strided_rows(r, n, stride):
    return pl.ds(r, n, stride=stride) if stride > 1 else pl.ds(0, n)


def _qkv_split_kernel(x_ref, cos_ref, sin_ref, o_ref, *, dil, n):
    rotated = pl.program_id(2) < (A_SECTIONS - 1) * A_HEADS
    lane = lax.broadcasted_iota(jnp.int32, (n, A_HEAD_DIM), 1)

    def emit(rotate):
        for r in range(dil):
            rows = _strided_rows(r, n, dil)
            x = x_ref[rows, :]
            if rotate:
                partner = jnp.where(lane < A_ROT_HALF,
                                    pltpu.roll(x, A_HEAD_DIM - A_ROT_HALF, axis=1),
                                    pltpu.roll(x, A_ROT_HALF, axis=1))
                x = x * cos_ref[rows, :] + partner * sin_ref[rows, :]
            o_ref[r] = x

    @pl.when(rotated)
    def _():
        emit(True)

    @pl.when(jnp.logical_not(rotated))
    def _():
        emit(False)


def qkv_split(qkv, cos_full, sin_full, *, g, dil, row0, batch, seq, tile):
    n = tile // dil
    tps = seq // tile
    r0 = row0 // tile
    in_col = lambda c: (c // A_HEADS * A_GROUPS + g) * A_HEADS + c % A_HEADS
    return pl.pallas_call(
        functools.partial(_qkv_split_kernel, dil=dil, n=n),
        grid=(batch, tps, A_SECTIONS * A_HEADS),
        in_specs=[pl.BlockSpec((tile, A_HEAD_DIM), lambda b, i, c: (r0 + b * tps + i, in_col(c))),
                  pl.BlockSpec((tile, A_HEAD_DIM), lambda b, i, c: (r0 + b * tps + i, 0)),
                  pl.BlockSpec((tile, A_HEAD_DIM), lambda b, i, c: (r0 + b * tps + i, 0))],
        out_specs=pl.BlockSpec((None, dil, n, A_HEAD_DIM), lambda b, i, c: (b, 0, i, c)),
        out_shape=jax.ShapeDtypeStruct((batch, dil, seq // dil, A_SECTIONS * A_GROUP_COLS), F32),
        compiler_params=_params(3),
        name="qkv_split",
    )(qkv, cos_full, sin_full)


def _rotate_pairs(x, cos, sin, lane):
    partner = jnp.where(lane < A_ROT_HALF,
                        pltpu.roll(x, A_HEAD_DIM - A_ROT_HALF, axis=1),
                        pltpu.roll(x, A_ROT_HALF, axis=1))
    return x * cos + partner * sin


def _qkv_project_kernel(x_ref, w_ref, cos_ref, sin_ref, o_ref, head_ref, *, dil, tile):
    part = tile // A_PROJECT_PARTS
    n = part // dil
    lane = lax.broadcasted_iota(jnp.int32, (part, A_HEAD_DIM), 1)
    for p in range(A_PROJECT_PARTS):
        rows = slice(p * part, (p + 1) * part)
        res = _dot(x_ref[rows, :], w_ref[...])
        cos, sin = cos_ref[rows, :], sin_ref[rows, :]
        for h in range(A_HEADS):
            cols = slice(h * A_HEAD_DIM, (h + 1) * A_HEAD_DIM)
            rot = _rotate_pairs(res[:, cols], cos, sin, lane)
            if dil == 1:
                o_ref[0, rows, cols] = rot
                continue
            head_ref[p, h] = rot
            for r in range(dil):
                o_ref[r, p * n:(p + 1) * n, cols] = head_ref[p, h, pl.ds(r, n, stride=dil), :]


def qkv_project(xb, w_qkv, cos2, sin2, *, layer, g, batch, seq, tile=ROW_TILE):
    d = xb.shape[1]
    dil = A_DILATIONS[g]
    tps = seq // tile
    table = pl.BlockSpec((None, tile, A_HEAD_DIM),
                         lambda b, i, s: (jnp.where(s == A_SECTIONS - 1, 1, 0), b * tps + i, 0))
    return pl.pallas_call(
        functools.partial(_qkv_project_kernel, dil=dil, tile=tile),
        grid=(batch, tps, A_SECTIONS),
        in_specs=[pl.BlockSpec((tile, d), lambda b, i, s: (b * tps + i, 0)),
                  pl.BlockSpec((None, d, A_GROUP_COLS), lambda b, i, s: (layer, 0, s * A_GROUPS + g)),
                  table, table],
        out_specs=pl.BlockSpec((None, dil, tile // dil, A_GROUP_COLS), lambda b, i, s: (b, 0, i, s)),
        out_shape=jax.ShapeDtypeStruct((batch, dil, seq // dil, A_SECTIONS * A_GROUP_COLS), F32),
        scratch_shapes=[pltpu.VMEM((A_PROJECT_PARTS, A_HEADS, tile // A_PROJECT_PARTS, A_HEAD_DIM), F32)],
        compiler_params=_params(3),
        name="qkv_project",
    )(xb, w_qkv, cos2, sin2)


def _attn_prompt_kernel(q_ref, kp_ref, kc_ref, vp_ref, vc_ref, o_ref, lse_ref):
    lb = pl.program_id(2)
    n = A_BLOCK
    qi = lax.broadcasted_iota(jnp.int32, (n, n), 0)
    kj = lax.broadcasted_iota(jnp.int32, (n, n), 1)
    mask_cur = kj <= qi
    mask_prev = jnp.logical_and(kj >= qi, lb > 0)
    scale = A_HEAD_DIM ** -0.5
    heads = range(A_HEADS)
    cols = [slice(h * A_HEAD_DIM, (h + 1) * A_HEAD_DIM) for h in heads]
    q = [q_ref[:, sl].astype(BF16) for sl in cols]
    s_c = [jnp.where(mask_cur, _dot_nt(q[h], kc_ref[:, cols[h]].astype(BF16)) * scale, NEG_BIG) for h in heads]
    s_p = [jnp.where(mask_prev, _dot_nt(q[h], kp_ref[:, cols[h]].astype(BF16)) * scale, NEG_BIG) for h in heads]
    m = [jnp.maximum(jnp.max(s_c[h], axis=-1, keepdims=True), jnp.max(s_p[h], axis=-1, keepdims=True)) for h in heads]
    p_c = [jnp.exp(s_c[h] - m[h]) for h in heads]
    p_p = [jnp.exp(s_p[h] - m[h]) for h in heads]
    l = [jnp.sum(p_c[h], axis=-1, keepdims=True) + jnp.sum(p_p[h], axis=-1, keepdims=True) for h in heads]
    o = [_dot(p_c[h].astype(BF16), vc_ref[:, cols[h]].astype(BF16))
         + _dot(p_p[h].astype(BF16), vp_ref[:, cols[h]].astype(BF16)) for h in heads]
    lse_all = jnp.zeros((n, LANES), F32)
    for h in heads:
        o_ref[:, cols[h]] = o[h] / l[h]
        lse_all = jnp.where(kj == h, m[h] + jnp.log(l[h]), lse_all)
    lse_ref[...] = lse_all


def attn_prompt(qkv_r):
    batch, dil, length, _ = qkv_r.shape
    nb = length // A_BLOCK
    spec = lambda sec, back: pl.BlockSpec(
        (None, None, A_BLOCK, A_GROUP_COLS), lambda b, r, lb: (b, r, jnp.maximum(lb - back, 0), sec))
    return pl.pallas_call(
        _attn_prompt_kernel,
        grid=(batch, dil, nb),
        in_specs=[spec(0, 0), spec(1, 1), spec(1, 0), spec(2, 1), spec(2, 0)],
        out_specs=[pl.BlockSpec((None, None, A_BLOCK, A_GROUP_COLS), lambda b, r, lb: (b, r, lb, 0)),
                   pl.BlockSpec((None, None, A_BLOCK, LANES), lambda b, r, lb: (b, r, lb, 0))],
        out_shape=[jax.ShapeDtypeStruct((batch, dil, length, A_GROUP_COLS), F32),
                   jax.ShapeDtypeStruct((batch, dil, length, LANES), F32)],
        compiler_params=_params(3),
        name="attn_prompt",
    )(qkv_r, qkv_r, qkv_r, qkv_r, qkv_r)


def _attn_sample_kernel(q_ref, kn_ref, vn_ref, *refs, dil, ts):
    n_res = len(refs) // 2 - 1
    cache_refs, (o_ref, lse_ref) = refs[:2 * n_res], refs[2 * n_res:]
    scale = A_HEAD_DIM ** -0.5
    jc = lax.broadcasted_iota(jnp.int32, (A_BLOCK, 1, 1), 0)
    un = lax.broadcasted_iota(jnp.int32, (ts, 1, 1), 0)
    kn = kn_ref[...]
    vn = vn_ref[...]
    for t in range(ts):
        kc_ref, vc_ref = cache_refs[2 * (t % n_res)], cache_refs[2 * (t % n_res) + 1]
        valid_n = (un <= t) if dil == 1 else (un == t)
        q = q_ref[t:t + 1]
        s_c = jnp.sum(kc_ref[...] * q, axis=-1, keepdims=True) * scale
        if dil == 1:
            s_c = jnp.where(jc >= t, s_c, NEG_BIG)
        s_n = jnp.where(valid_n, jnp.sum(kn * q, axis=-1, keepdims=True) * scale, NEG_BIG)
        m = jnp.maximum(jnp.max(s_c, axis=0, keepdims=True), jnp.max(s_n, axis=0, keepdims=True))
        p_c = jnp.exp(s_c - m)
        p_n = jnp.exp(s_n - m)
        l = jnp.sum(p_c, axis=0, keepdims=True) + jnp.sum(p_n, axis=0, keepdims=True)
        o = jnp.sum(p_c * vc_ref[...], axis=0, keepdims=True) + jnp.sum(p_n * vn, axis=0, keepdims=True)
        o_ref[t:t + 1] = o / l
        lse_ref[t:t + 1] = jnp.broadcast_to(m + jnp.log(l), (1, A_HEADS, A_HEAD_DIM))


def attn_sample(q, k_new, v_new, caches, *, layer, dil):
    db, ts = q.shape[:2]
    wb = caches.shape[2]
    assert wb == A_BLOCK * dil and (dil == 1 or ts <= dil)
    cache_v = caches.reshape(caches.shape[0], db, A_BLOCK, dil, 2, A_HEADS, A_HEAD_DIM)
    n_res = 1 if dil == 1 else ts
    tok_spec = pl.BlockSpec((None, ts, A_HEADS, A_HEAD_DIM), lambda b: (b, 0, 0, 0))
    cache_specs = [pl.BlockSpec((None, None, A_BLOCK, None, None, A_HEADS, A_HEAD_DIM),
                                lambda b, r=r, kv=kv: (layer, b, 0, r, kv, 0, 0))
                   for r in range(n_res) for kv in range(2)]
    out = jax.ShapeDtypeStruct((db, ts, A_HEADS, A_HEAD_DIM), F32)
    return pl.pallas_call(
        functools.partial(_attn_sample_kernel, dil=dil, ts=ts),
        grid=(db,),
        in_specs=[tok_spec, tok_spec, tok_spec] + cache_specs,
        out_specs=[tok_spec, tok_spec],
        out_shape=[out, out],
        compiler_params=_params(1),
        name="attn_sample",
    )(q, k_new, v_new, *([cache_v] * (2 * n_res)))


def _merge_groups(head_out, lses, a_ref):
    m = functools.reduce(jnp.maximum, lses)
    es = [jnp.exp(l - m) for l in lses]
    den = sum(es)
    ws = [e / den for e in es]
    for h in range(A_HEADS):
        a = sum(w[:, h:h + 1] * head_out(g, h) for g, w in enumerate(ws))
        a_ref[:, h * A_HEAD_DIM:(h + 1) * A_HEAD_DIM] = a.astype(a_ref.dtype)


def _attn_merge_rows_kernel(*refs):
    outs = [r[...] for r in refs[:A_GROUPS]]
    lses = [r[...] for r in refs[A_GROUPS:2 * A_GROUPS]]
    m = functools.reduce(jnp.maximum, lses)
    es = [jnp.exp(l - m) for l in lses]
    den = sum(es)
    refs[2 * A_GROUPS][...] = sum(e / den * o for e, o in zip(es, outs))


def attn_merge_rows(outs, lses):
    spec = pl.BlockSpec(outs[0].shape, lambda i: (0, 0, 0, 0))
    return pl.pallas_call(
        _attn_merge_rows_kernel,
        grid=(1,),
        in_specs=[spec] * (2 * A_GROUPS),
        out_specs=spec,
        out_shape=jax.ShapeDtypeStruct(outs[0].shape, F32),
        compiler_params=_params(1),
        name="attn_merge_rows",
    )(*outs, *lses)


def _attn_merge_prompt_kernel(*refs, tile):
    o_refs, l_refs = refs[:A_GROUPS], refs[A_GROUPS:2 * A_GROUPS]
    a_ref, o_nat, l_nat = refs[2 * A_GROUPS:]
    lses = []
    for g, dil in enumerate(A_DILATIONS):
        if dil == 1:
            lses.append(l_refs[g][0])
            continue
        n = tile // dil
        for r in range(dil):
            rows = pl.ds(r, n, stride=dil)
            l_nat[g, rows, :] = l_refs[g][r]
            for h in range(A_HEADS):
                o_nat[g * A_HEADS + h, rows, :] = o_refs[g][r, :, h * A_HEAD_DIM:(h + 1) * A_HEAD_DIM]
        lses.append(l_nat[g])

    def head_out(g, h):
        if A_DILATIONS[g] == 1:
            return o_refs[g][0, :, h * A_HEAD_DIM:(h + 1) * A_HEAD_DIM]
        return o_nat[g * A_HEADS + h]

    _merge_groups(head_out, lses, a_ref)


def attn_merge_prompt(outs, lses, *, tile=ROW_TILE):
    batch, _, seq, cols = outs[0].shape
    tps = seq // tile
    in_specs = []
    for width in (cols, LANES):
        for dil in A_DILATIONS:
            in_specs.append(pl.BlockSpec((None, dil, tile // dil, width), lambda b, i: (b, 0, i, 0)))
    return pl.pallas_call(
        functools.partial(_attn_merge_prompt_kernel, tile=tile),
        grid=(batch, tps),
        in_specs=in_specs,
        out_specs=pl.BlockSpec((tile, cols), lambda b, i: (b * tps + i, 0)),
        out_shape=jax.ShapeDtypeStruct((batch * seq, cols), BF16),
        scratch_shapes=[pltpu.VMEM((A_GROUPS * A_HEADS, tile, A_HEAD_DIM), F32),
                        pltpu.VMEM((A_GROUPS, tile, LANES), F32)],
        compiler_params=_params(2),
        name="attn_merge_prompt",
    )(*outs, *lses)


def _conv_silu_chunk(first, u_ref, prev_ref, w_ref, bias, full_ref, out_ref, col0):
    c, width = u_ref.shape
    cols = slice(col0, col0 + width)

    @pl.when(first)
    def _():
        full_ref[0:SUBLANES, cols] = prev_ref[...]

    u = u_ref[...]
    full_ref[SUBLANES:SUBLANES + c, cols] = u
    acc = u * w_ref[CONV_TAPS - 1:CONV_TAPS, :]
    if bias is not None:
        acc = acc + bias
    for s in range(1, CONV_TAPS):
        acc = acc + full_ref[SUBLANES - s:SUBLANES - s + c, cols] * w_ref[CONV_TAPS - 1 - s:CONV_TAPS - s, :]
    out_ref[:, cols] = _silu(acc)
    full_ref[0:SUBLANES, cols] = u[c - SUBLANES:c, :]


def _unit_lower_inverses(mats, n, top):
    blk = min(B_INV_BLOCK, top)
    ii = lax.broadcasted_iota(jnp.int32, (n, n), 0)
    jj = lax.broadcasted_iota(jnp.int32, (n, n), 1)
    eye = jnp.where(ii == jj, 1.0, 0.0).astype(F32)
    shift = blk.bit_length() - 1
    same = (ii >> shift) == (jj >> shift)
    ps = [jnp.where(same, -a, 0.0) for a in mats]
    xs = [eye + p for p in ps]
    for _ in range(shift - 1):
        ps = [_mm(p, p, B_INV_PASSES) for p in ps]
        xs = [x + _mm(x, p, B_INV_PASSES) for x, p in zip(xs, ps)]
    size = blk
    while size < top:
        shift += 1
        same_next = (ii >> shift) == (jj >> shift)
        sel = jnp.logical_and(same_next, jnp.logical_not(same))
        ys = [_mm(x, jnp.where(sel, a, 0.0), B_INV_PASSES) for x, a in zip(xs, mats)]
        xs = [x - _mm(y, x, B_INV_PASSES) for x, y in zip(xs, ys)]
        same = same_next
        size *= 2
    return xs


def _gdn_kernel(qr_ref, kr_ref, vr_ref, qp_ref, kp_ref, vp_ref, qw_ref, kw_ref, vw_ref,
                z_ref, ba_ref, alog_ref, dtb_ref, nw_ref, s0_ref,
                o_ref, s_ref, full_ref, conv_ref, *, c, hps, t_valid):
    hb = pl.program_id(1)
    ci = pl.program_id(2)
    dk = B_HEAD_DIM
    c2 = 2 * c
    assert c2 == LANES

    @pl.when(ci == 0)
    def _():
        s_ref[...] = s0_ref[...]

    qw = hps // 2 * dk
    for raw, prev, taps, col0 in ((qr_ref, qp_ref, qw_ref, 0), (kr_ref, kp_ref, kw_ref, qw),
                                  (vr_ref, vp_ref, vw_ref, 2 * qw)):
        _conv_silu_chunk(ci == 0, raw, prev, taps, None, full_ref, conv_ref, col0)
    q_ref = conv_ref.at[:, 0:qw]
    k_ref = conv_ref.at[:, qw:2 * qw]
    v_ref = conv_ref.at[:, 2 * qw:]

    ba = ba_ref[...]
    valid = lax.broadcasted_iota(jnp.int32, (c, LANES), 0) + ci * c < t_valid
    beta_all = jnp.where(valid, jax.nn.sigmoid(ba), 0.0)
    g_all = jnp.where(valid, -jnp.exp(alog_ref[pl.ds(hb, 1), :]) * _softplus(ba + dtb_ref[pl.ds(hb, 1), :]), 0.0)
    ci_, cj_ = lax.broadcasted_iota(jnp.int32, (c, c), 0), lax.broadcasted_iota(jnp.int32, (c, c), 1)
    gc_all = _select_rows(jnp.where(ci_ >= cj_, 1.0, 0.0).astype(BF16), g_all)
    gc_t = jnp.concatenate([gc_all, gc_all], axis=0).T
    ii = lax.broadcasted_iota(jnp.int32, (c2, c2), 0)
    jj = lax.broadcasted_iota(jnp.int32, (c2, c2), 1)
    shift = c.bit_length() - 1
    same = (ii >> shift) == (jj >> shift)
    incl = jnp.logical_and(same, ii >= jj)
    strict = jnp.logical_and(same, ii > jj)
    first = lax.broadcasted_iota(jnp.int32, (1, c2), 1) < c
    top = lax.broadcasted_iota(jnp.int32, (c2, 1), 0) < c
    nw = nw_ref[...]

    def stack_cols(arr, l0, l1):
        return jnp.concatenate([arr[:, l0:l0 + 1], arr[:, l1:l1 + 1]], axis=0)

    def own_half(r):
        return jnp.where(top, r[:, :dk], r[:, dk:])

    pairs = range(hps // 2)
    cols = lambda h: slice(h * dk, (h + 1) * dk)

    def prepare(qh):
        h0, h1 = 2 * qh, 2 * qh + 1
        q = q_ref[:, cols(qh)]
        k = k_ref[:, cols(qh)]
        q = q * lax.rsqrt(jnp.sum(q * q, axis=-1, keepdims=True) + 1e-6) * (dk ** -0.5)
        k = k * lax.rsqrt(jnp.sum(k * k, axis=-1, keepdims=True) + 1e-6)
        q2 = jnp.concatenate([q, q], axis=0)
        k2 = jnp.concatenate([k, k], axis=0)
        k2_b = k2.astype(BF16)
        beta = stack_cols(beta_all, h0, h1)
        gc = stack_cols(gc_all, hps + h0, hps + h1)
        gr = jnp.where(first, gc_t[hps + h0:hps + h0 + 1, :], gc_t[hps + h1:hps + h1 + 1, :])
        gl0 = gc_all[c - 1:c, hps + h0:hps + h0 + 1]
        gl1 = gc_all[c - 1:c, hps + h1:hps + h1 + 1]
        decay = jnp.exp(jnp.where(incl, gc - gr, NEG_BIG))
        eg = jnp.exp(gc)
        v2 = jnp.concatenate([v_ref[:, cols(h0)], v_ref[:, cols(h1)]], axis=0)
        return dict(
            a=jnp.where(strict, _dot_nt(k2_b, k2_b) * decay, 0.0) * beta,
            qk=jnp.where(incl, _dot_nt(q2.astype(BF16), k2_b) * decay, 0.0).astype(BF16),
            rhs=jnp.concatenate([v2 * beta, k2 * (beta * eg)], axis=1),
            q_eg=(q2 * eg).astype(BF16),
            k_dec=(k2 * jnp.exp(jnp.where(top, gl0, gl1) - gc)).astype(BF16),
            dec_cat=jnp.concatenate([jnp.broadcast_to(jnp.exp(gl0), (1, dk)),
                                     jnp.broadcast_to(jnp.exp(gl1), (1, dk))], axis=1))

    st = [prepare(qh) for qh in pairs]
    t_inv = _unit_lower_inverses([s["a"] for s in st], c2, c)
    uw = [_mm(t, s["rhs"], B_INV_PASSES) for t, s in zip(t_inv, st)]
    s_cat = [jnp.concatenate([s_ref[2 * qh], s_ref[2 * qh + 1]], axis=1) for qh in pairs]
    s_cat_b = [s.astype(BF16) for s in s_cat]
    v_new = [r[:, :dk] - own_half(_dot(r[:, dk:].astype(BF16), sb)) for r, sb in zip(uw, s_cat_b)]
    v_new_b = [v.astype(BF16) for v in v_new]
    outs = [own_half(_dot(s["q_eg"], sb)) + _dot(s["qk"], vb) for s, sb, vb in zip(st, s_cat_b, v_new_b)]
    for qh in pairs:
        h0, h1 = 2 * qh, 2 * qh + 1
        v = v_new[qh]
        v_blk = jnp.concatenate([jnp.where(top, v, 0.0), jnp.where(top, 0.0, v)], axis=1).astype(BF16)
        s_new = s_cat[qh] * st[qh]["dec_cat"] + _dot_tn(st[qh]["k_dec"], v_blk)
        s_ref[h0] = s_new[:, :dk]
        s_ref[h1] = s_new[:, dk:]
        o = outs[qh]
        o = o * lax.rsqrt(jnp.mean(o * o, axis=-1, keepdims=True) + RMS_EPS) * nw
        o = (o * _silu(jnp.concatenate([z_ref[:, cols(h0)], z_ref[:, cols(h1)]], axis=0))).astype(o_ref.dtype)
        o_ref[:, cols(h0)] = o[:c]
        o_ref[:, cols(h1)] = o[c:]


def gdn_scan(proj, conv_prev, conv_taps, ba, alog_pad, dtb_pad, norm_w, s0, *, nseq, seq, t_valid):
    c = B_CHUNK
    hps = B_HEADS_PER_STEP
    nc = seq // c
    ng = B_V_HEADS // hps
    qw = hps // 2 * B_HEAD_DIM
    vw = hps * B_HEAD_DIM
    par_rows = alog_pad.shape[0]
    row = lambda s, h, ci: s * nc + ci
    k0, v0, z0 = B_KEY_DIM // qw, 2 * B_KEY_DIM // vw, B_CONV_DIM // vw
    chunk = lambda width, col0: pl.BlockSpec((c, width), lambda s, h, ci: (row(s, h, ci), col0 + h))
    state = lambda width, col0: pl.BlockSpec((None, SUBLANES, width), lambda s, h, ci: (s, 0, col0 + h))
    taps = lambda width, col0: pl.BlockSpec((SUBLANES, width), lambda s, h, ci: (0, col0 + h))
    params = pl.BlockSpec((par_rows, LANES), lambda s, h, ci: (0, 0))
    heads = pl.BlockSpec((None, hps, B_HEAD_DIM, B_HEAD_DIM), lambda s, h, ci: (s, h, 0, 0))
    return pl.pallas_call(
        functools.partial(_gdn_kernel, c=c, hps=hps, t_valid=t_valid),
        grid=(nseq, ng, nc),
        in_specs=[chunk(qw, 0), chunk(qw, k0), chunk(vw, v0),
                  state(qw, 0), state(qw, k0), state(vw, v0),
                  taps(qw, 0), taps(qw, k0), taps(vw, v0),
                  chunk(vw, z0), chunk(LANES, 0), params, params,
                  pl.BlockSpec((1, B_HEAD_DIM), lambda s, h, ci: (0, 0)), heads],
        out_specs=[chunk(vw, 0), heads],
        out_shape=[jax.ShapeDtypeStruct((nseq * seq, B_VAL_DIM), BF16),
                   jax.ShapeDtypeStruct((nseq, B_V_HEADS, B_HEAD_DIM, B_HEAD_DIM), F32)],
        scratch_shapes=[pltpu.VMEM((SUBLANES + c, 2 * qw + vw), F32),
                        pltpu.VMEM((c, 2 * qw + vw), F32)],
        compiler_params=_params(3),
        name="gdn_scan",
    )(proj, proj, proj, conv_prev, conv_prev, conv_prev, conv_taps, conv_taps, conv_taps,
      proj, ba, alog_pad, dtb_pad, norm_w, s0)


def _ssd_kernel(xr_ref, br_ref, cr_ref, xp_ref, bp_ref, cp_ref, xw_ref, bw_ref, cw_ref,
                xbias_ref, bbias_ref, cbias_ref, z_ref, dt_ref, dtb_ref, alog_ref, dskip_ref, nw_ref,
                h0_ref, y_ref, hout_ref, ht_ref, full_ref, conv_ref, *, c, nc, t_valid):
    g = pl.program_id(1)
    ci = pl.program_id(2)
    hpg = C_HEADS_PER_GROUP
    p = C_HEAD_DIM
    gcols = hpg * p

    @pl.when(ci == 0)
    def _():
        ht_ref[...] = h0_ref[...].reshape(gcols, C_D_STATE).T

    for raw, prev, taps, bias, col0 in ((xr_ref, xp_ref, xw_ref, xbias_ref, 0),
                                        (br_ref, bp_ref, bw_ref, bbias_ref, gcols),
                                        (cr_ref, cp_ref, cw_ref, cbias_ref, gcols + C_D_STATE)):
        _conv_silu_chunk(ci == 0, raw, prev, taps, bias[...], full_ref, conv_ref, col0)
    x_ref = conv_ref.at[:, 0:gcols]
    b_ref = conv_ref.at[:, gcols:gcols + C_D_STATE]
    c_ref = conv_ref.at[:, gcols + C_D_STATE:]

    row = lax.broadcasted_iota(jnp.int32, (c, LANES), 0) + ci * c
    dt = jnp.where(row < t_valid, _softplus(dt_ref[...] + dtb_ref[pl.ds(g, 1), :]), 0.0)
    la = dt * -jnp.exp(alog_ref[pl.ds(g, 1), :])
    ii = lax.broadcasted_iota(jnp.int32, (c, c), 0)
    jj = lax.broadcasted_iota(jnp.int32, (c, c), 1)
    incl = ii >= jj
    acs = _select_rows(jnp.where(incl, 1.0, 0.0).astype(BF16), la)
    acs_t = acs.T
    el = lax.broadcasted_iota(jnp.int32, (LANES, gcols), 0)
    ej = lax.broadcasted_iota(jnp.int32, (LANES, gcols), 1)
    expand = jnp.where(el == (ej >> (p.bit_length() - 1)), 1.0, 0.0).astype(BF16)
    dt_e = _select_cols(dt, expand)
    acs_e = _select_cols(acs, expand)
    dskip_e = _select_cols(jnp.broadcast_to(dskip_ref[pl.ds(g, 1), :], (SUBLANES, LANES)), expand)[0:1, :]

    x = x_ref[...]
    bm = b_ref[...].astype(BF16)
    cm = c_ref[...].astype(BF16)
    xdt = (x * dt_e).astype(BF16)
    cb = _dot_nt(cm, bm)
    ht = ht_ref[...]
    y = _dot(cm, ht.astype(BF16)) * jnp.exp(acs_e) + dskip_e * x
    lane = lax.broadcasted_iota(jnp.int32, (c, 2 * p), 1)
    decay = [jnp.exp(jnp.where(incl, acs[:, hd:hd + 1] - acs_t[hd:hd + 1, :], NEG_BIG)) for hd in range(hpg)]
    mats = [(cb * dec).astype(BF16) for dec in decay]
    ys = [_dot(mats[hd], xdt[:, (hd // 2) * 2 * p:(hd // 2 + 1) * 2 * p]) for hd in range(hpg)]
    y = y + jnp.concatenate([jnp.where(lane < p, ys[2 * j], ys[2 * j + 1]) for j in range(hpg // 2)], axis=1)
    y = y * _silu(z_ref[...])
    y = y * lax.rsqrt(jnp.mean(y * y, axis=-1, keepdims=True) + RMS_EPS) * nw_ref[...]
    y_ref[...] = y.astype(y_ref.dtype)

    last_e = acs_e[c - 1:c, :]
    xw = (x * (jnp.exp(last_e - acs_e) * dt_e)).astype(BF16)
    ht_new = ht * jnp.exp(last_e) + _dot_tn(bm, xw)
    ht_ref[...] = ht_new

    @pl.when(ci == nc - 1)
    def _():
        hout_ref[...] = ht_new.T.reshape(hpg, p, C_D_STATE)


def ssd_scan(proj, conv_prev, conv_taps, conv_bias, dt, dtb_pad, alog_pad, dskip_pad, norm_w, h0,
             *, nseq, seq, t_valid):
    c = C_CHUNK
    nc = seq // c
    gc = C_GROUP_COLS
    n = C_D_STATE
    row = lambda s, g, ci: s * nc + ci
    px, pb_, pc_ = C_D_INNER // gc, 2 * C_D_INNER // n, 2 * C_D_INNER // n + C_GROUPS
    cb_, cc_ = C_D_INNER // n, C_D_INNER // n + C_GROUPS
    chunk = lambda width, col0: pl.BlockSpec((c, width), lambda s, g, ci: (row(s, g, ci), col0 + g))
    state = lambda width, col0: pl.BlockSpec((None, SUBLANES, width), lambda s, g, ci: (s, 0, col0 + g))
    taps = lambda width, col0: pl.BlockSpec((SUBLANES, width), lambda s, g, ci: (0, col0 + g))
    bias = lambda width, col0: pl.BlockSpec((1, width), lambda s, g, ci: (0, col0 + g))
    par = pl.BlockSpec((C_GROUPS, LANES), lambda s, g, ci: (0, 0))
    heads = pl.BlockSpec((None, C_HEADS_PER_GROUP, C_HEAD_DIM, n), lambda s, g, ci: (s, g, 0, 0))
    width = gc + 2 * n
    return pl.pallas_call(
        functools.partial(_ssd_kernel, c=c, nc=nc, t_valid=t_valid),
        grid=(nseq, C_GROUPS, nc),
        in_specs=[chunk(gc, px), chunk(n, pb_), chunk(n, pc_),
                  state(gc, 0), state(n, cb_), state(n, cc_),
                  taps(gc, 0), taps(n, cb_), taps(n, cc_),
                  bias(gc, 0), bias(n, cb_), bias(n, cc_),
                  chunk(gc, 0), chunk(LANES, 0), par, par, par, bias(gc, 0), heads],
        out_specs=[chunk(gc, 0), heads],
        out_shape=[jax.ShapeDtypeStruct((nseq * seq, C_D_INNER), BF16),
                   jax.ShapeDtypeStruct((nseq, C_HEADS, C_HEAD_DIM, n), F32)],
        scratch_shapes=[pltpu.VMEM((n, gc), F32),
                        pltpu.VMEM((SUBLANES + c, width), F32),
                        pltpu.VMEM((c, width), F32)],
        compiler_params=_params(3),
        name="ssd_scan",
    )(proj, proj, proj, conv_prev, conv_prev, conv_prev, conv_taps, conv_taps, conv_taps,
      conv_bias, conv_bias, conv_bias, proj, dt, dtb_pad, alog_pad, dskip_pad, norm_w, h0)


def _rope_tables(mp, seq, n_sample, ts):
    pos = jnp.concatenate([jnp.arange(mp) % seq, PAST_LEN + jnp.arange(n_sample) % ts]).astype(F32)
    inv_freq = ROPE_THETA ** (-jnp.arange(A_ROT_HALF, dtype=F32) / A_ROT_HALF)
    ang = pos[:, None] * inv_freq[None, :]
    cos, sin = jnp.cos(ang), jnp.sin(ang)
    rest = A_HEAD_DIM - 2 * A_ROT_HALF
    rows = pos.shape[0]
    cos_full = jnp.concatenate([cos, cos, jnp.ones((rows, rest), F32)], axis=1)
    sin_full = jnp.concatenate([-sin, sin, jnp.zeros((rows, rest), F32)], axis=1)
    return cos_full, sin_full


def _lane_groups(cols, group):
    lead = cols.shape[:-1]
    n = cols.shape[-1]
    c = cols.reshape(*lead, n // group, group)
    c = jnp.pad(c, [(0, 0)] * (c.ndim - 1) + [(0, LANES - group)])
    return c.reshape(*lead, n // group * LANES)


def _pad_seq_rows(x, nseq, ts, seq):
    cols = x.shape[-1]
    x = jnp.pad(x.reshape(nseq, ts, cols), ((0, 0), (0, seq - ts), (0, 0)))
    return x.reshape(nseq * seq, cols)


def _conv_prev(state):
    return jnp.pad(state, ((0, 0), (SUBLANES - state.shape[1], 0), (0, 0)))


def _conv_taps(w):
    return jnp.pad(w, ((0, SUBLANES - w.shape[0]), (0, 0)))


def _last_rows(x, nseq, seq, n, col0, cols):
    return jnp.stack([x[(s + 1) * seq - n:(s + 1) * seq, col0:col0 + cols] for s in range(nseq)])


def kernel(x_prompt, x_sample, cache_a_kv_w128, cache_a_kv_w512, cache_a_kv_w2048, state_b_ssm, state_b_conv, state_c_ssm, state_c_conv, ln_g, ln_b, ffn1_w_gu, ffn1_w_down, ffn2_w_gu, ffn2_w_down, a_w_qkv, a_w_o, b_w_in, b_conv_w, b_a_log, b_dt_bias, b_norm_w, b_w_out, c_w_in, c_conv_w, c_conv_b, c_dt_bias, c_a_log, c_d, c_norm_w, c_w_out):
    a_bufs = (cache_a_kv_w128, cache_a_kv_w512, cache_a_kv_w2048)
    bp, seq, d = x_prompt.shape
    db, ts, _ = x_sample.shape
    mp, ms = bp * seq, db * ts
    assert seq % (A_BLOCK * max(A_DILATIONS)) == 0 and mp % ms == 0
    x = jnp.concatenate([x_prompt.reshape(mp, d), x_sample.reshape(ms, d)], axis=0)
    xb = x.astype(BF16)
    cos_full, sin_full = _rope_tables(mp, seq, ms, ts)
    cos2 = jnp.stack([cos_full, jnp.ones_like(cos_full)])
    sin2 = jnp.stack([sin_full, jnp.zeros_like(sin_full)])
    wb16 = {name: w.astype(BF16) for name, w in dict(
        ffn1_down=ffn1_w_down, ffn2_down=ffn2_w_down, a_qkv=a_w_qkv, a_o=a_w_o,
        b_in=b_w_in[:, :, :B_CONV_DIM + B_VAL_DIM], b_out=b_w_out,
        c_in=c_w_in[:, :, :C_D_INNER + C_CONV_DIM], c_out=c_w_out).items()}
    w_gu = dict(ffn1=ffn1_w_gu, ffn2=ffn2_w_gu)

    def valid_rows(a_s, rows):
        return a_s.reshape(db, rows, -1)[:, :ts].reshape(ms, -1)

    outs = {}

    def mixer_a(xb, j):
        qkv_s = matmul_tail(xb, wb16["a_qkv"], layer=j, n=a_w_qkv.shape[2], tn=1024)
        gw = A_GROUP_COLS
        o_p, l_p, o_s, l_s = [], [], [], []
        for g, dil in enumerate(A_DILATIONS):
            qkv_r = qkv_project(xb, wb16["a_qkv"], cos2, sin2, layer=j, g=g, batch=bp, seq=seq)
            o, lse = attn_prompt(qkv_r)
            o_p.append(o)
            l_p.append(lse)
            new = qkv_split(qkv_s, cos_full[mp:], sin_full[mp:], g=g, dil=1, row0=0, batch=1, seq=ms, tile=ms)
            q_s, k_s, v_s = (new[0, 0, :, sec * gw:(sec + 1) * gw].reshape(db, ts, A_HEADS, A_HEAD_DIM)
                             for sec in range(A_SECTIONS))
            o, lse = attn_sample(q_s, k_s, v_s, a_bufs[g], layer=j, dil=dil)
            o_s.append(o)
            l_s.append(lse)
            length = seq // dil
            tail = qkv_r[:, :, length - A_BLOCK:, gw:].transpose(0, 2, 1, 3)
            outs.setdefault("a_p%d" % g, []).append(tail.reshape(bp, A_BLOCK * dil, 2, A_HEADS, A_HEAD_DIM))
            outs.setdefault("a_new%d" % g, []).append(jnp.stack([k_s, v_s], axis=2))
        a_s = attn_merge_rows(o_s, l_s).reshape(ms, gw).astype(BF16)
        return attn_merge_prompt(o_p, l_p), a_s, wb16["a_o"]

    def mixer_b(xb, j):
        hps = B_HEADS_PER_STEP
        ng = B_V_HEADS // hps
        w_in = b_w_in[j]
        main = B_CONV_DIM + B_VAL_DIM
        pb, pb_tail = matmul(xb, wb16["b_in"], layer=j, n=main, tn=1024)
        w_tail = jnp.concatenate([w_in[:, main:main + B_V_HEADS].reshape(d, ng, hps),
                                  w_in[:, main + B_V_HEADS:].reshape(d, ng, hps)], axis=2)
        ba, ba_tail = matmul(xb, _lane_groups(w_tail.reshape(d, -1), 2 * hps).astype(BF16)[None],
                    layer=0, n=ng * LANES, tn=ng * LANES)
        gate_lanes = lambda v: jnp.pad(_lane_groups(
            jnp.concatenate([jnp.zeros((ng, hps), F32), v.reshape(ng, hps)], axis=1).reshape(-1), 2 * hps
        ).reshape(ng, LANES), ((0, SUBLANES - ng), (0, 0)))
        alog_pad, dtb_pad = gate_lanes(b_a_log[j]), gate_lanes(b_dt_bias[j])
        taps = _conv_taps(b_conv_w[j])
        nw = b_norm_w[j].reshape(1, B_HEAD_DIM)
        a_p, s_p = gdn_scan(pb, jnp.zeros((bp, SUBLANES, B_CONV_DIM), F32), taps, ba, alog_pad, dtb_pad, nw,
                            jnp.zeros((bp, B_V_HEADS, B_HEAD_DIM, B_HEAD_DIM), F32),
                            nseq=bp, seq=seq, t_valid=seq)
        pb_s = _pad_seq_rows(pb_tail, db, ts, B_CHUNK)
        ba_s = _pad_seq_rows(ba_tail, db, ts, B_CHUNK)
        a_s, s_s = gdn_scan(pb_s, _conv_prev(state_b_conv[j]), taps, ba_s, alog_pad, dtb_pad, nw, state_b_ssm[j],
                            nseq=db, seq=B_CHUNK, t_valid=ts)
        outs.setdefault("b_ssm_p", []).append(s_p)
        outs.setdefault("b_ssm_s", []).append(s_s)
        outs.setdefault("b_conv_p", []).append(_last_rows(pb, bp, seq, CONV_TAPS - 1, 0, B_CONV_DIM))
        outs.setdefault("b_conv_s", []).append(_last_rows(pb_tail, db, ts, CONV_TAPS - 1, 0, B_CONV_DIM))
        return a_p, valid_rows(a_s, B_CHUNK), wb16["b_out"]

    def mixer_c(xb, j):
        w_in = c_w_in[j]
        main = C_D_INNER + C_CONV_DIM
        pc, pc_tail = matmul(xb, wb16["c_in"], layer=j, n=main, tn=1024)
        dt, dt_tail = matmul(xb, _lane_groups(w_in[:, main:], C_HEADS_PER_GROUP).astype(BF16)[None],
                    layer=0, n=C_GROUPS * LANES, tn=C_GROUPS * LANES)
        head_lanes = lambda v: _lane_groups(v, C_HEADS_PER_GROUP).reshape(C_GROUPS, LANES)
        dtb_pad, alog_pad, dskip_pad = head_lanes(c_dt_bias[j]), head_lanes(c_a_log[j]), head_lanes(c_d[j])
        taps = _conv_taps(c_conv_w[j])
        bias = c_conv_b[j].reshape(1, C_CONV_DIM)
        nw = c_norm_w[j].reshape(1, C_D_INNER)
        a_p, h_p = ssd_scan(pc, jnp.zeros((bp, SUBLANES, C_CONV_DIM), F32), taps, bias, dt, dtb_pad, alog_pad,
                            dskip_pad, nw, jnp.zeros((bp, C_HEADS, C_HEAD_DIM, C_D_STATE), F32),
                            nseq=bp, seq=seq, t_valid=seq)
        pc_s = _pad_seq_rows(pc_tail, db, ts, C_CHUNK)
        dt_s = _pad_seq_rows(dt_tail, db, ts, C_CHUNK)
        a_s, h_s = ssd_scan(pc_s, _conv_prev(state_c_conv[j]), taps, bias, dt_s, dtb_pad, alog_pad, dskip_pad, nw,
                            state_c_ssm[j], nseq=db, seq=C_CHUNK, t_valid=ts)
        outs.setdefault("c_ssm_p", []).append(h_p)
        outs.setdefault("c_ssm_s", []).append(h_s)
        outs.setdefault("c_conv_p", []).append(_last_rows(pc, bp, seq, CONV_TAPS - 1, C_D_INNER, C_CONV_DIM))
        outs.setdefault("c_conv_s", []).append(_last_rows(pc_tail, db, ts, CONV_TAPS - 1, C_D_INNER, C_CONV_DIM))
        return a_p, valid_rows(a_s, C_CHUNK), wb16["c_out"]

    def ffn(x, xb, which, i, g, b):
        h, h_tail = swiglu_up(xb, w_gu[which], layer=i)
        return matmul_postnorm(h, wb16[which + "_down"], x, g, b, layer=i, scale=0.5, a_tail=h_tail)

    mixers = (mixer_a, mixer_b, mixer_c)
    for i in range(DEPTH):
        x, xb = ffn(x, xb, "ffn1", i, ln_g[i, 0], ln_b[i, 0])
        a_p, a_s, w_out = mixers[i % 3](xb, i // 3)
        x, xb = matmul_postnorm(a_p, w_out, x, ln_g[i, 1], ln_b[i, 1], layer=i // 3, scale=1.0, a_tail=a_s)
        x, xb = ffn(x, xb, "ffn2", i, ln_g[i, 2], ln_b[i, 2])

    st = lambda name: jnp.stack(outs[name])
    def shifted(buf, new):
        moved = lax.pad(buf, jnp.zeros((), buf.dtype), [(0, 0, 0), (0, 0, 0), (-ts, ts, 0)] + [(0, 0, 0)] * 3)
        return lax.dynamic_update_slice(moved, new, (0, 0, buf.shape[2] - ts, 0, 0, 0))

    a_s = [shifted(a_bufs[g], st("a_new%d" % g)) for g in range(A_GROUPS)]
    return (x[:mp].reshape(bp, seq, d), x[mp:].reshape(db, ts, d),
            st("a_p0"), a_s[0], st("a_p1"), a_s[1], st("a_p2"), a_s[2],
            st("b_ssm_p"), st("b_ssm_s"), st("b_conv_p"), st("b_conv_s"),
            st("c_ssm_p"), st("c_ssm_s"), st("c_conv_p"), st("c_conv_s"))
```

```python
import functools

import jax
import jax.numpy as jnp
from jax import lax
from jax.experimental import pallas as pl
from jax.experimental.pallas import tpu as pltpu

F32 = jnp.float32
BF16 = jnp.bfloat16

VMEM_LIMIT_BYTES = 56 * 1024 * 1024
LANES = 128
SUBLANES = 8
ROW_TILE = 512
POSTNORM_ROW_TILE = 256
POSTNORM_PARTS = 2
SWIGLU_PARTS = 1
NEG_BIG = -1e30

DEPTH = 4
DN_ALPHA = (2.0 * DEPTH) ** 0.25
LN_EPS = 1e-5
RMS_EPS = 1e-6
PAST_LEN = 16384

A_DILATIONS = (1, 4, 16)
A_GROUPS = 3
A_HEADS = 8
A_HEAD_DIM = 128
A_ROT_HALF = A_HEAD_DIM // 8
A_BLOCK = 128
A_GROUP_COLS = A_HEADS * A_HEAD_DIM
A_SECTIONS = 3
A_SPLIT_TILE = 2048
A_PROJECT_PARTS = 2
ROPE_THETA = 500000.0

B_QK_HEADS = 16
B_V_HEADS = 32
B_HEAD_DIM = 128
B_KEY_DIM = B_QK_HEADS * B_HEAD_DIM
B_VAL_DIM = B_V_HEADS * B_HEAD_DIM
B_CONV_DIM = 2 * B_KEY_DIM + B_VAL_DIM
B_CHUNK = 64
B_HEADS_PER_STEP = 32
B_INV_BLOCK = 16
B_INV_PASSES = 1

C_D_INNER = 4096
C_HEADS = 64
C_HEAD_DIM = 64
C_GROUPS = 8
C_HEADS_PER_GROUP = C_HEADS // C_GROUPS
C_GROUP_COLS = C_D_INNER // C_GROUPS
C_D_STATE = 128
C_CONV_DIM = C_D_INNER + 2 * C_GROUPS * C_D_STATE
C_CHUNK = 128
C_GROUPS_PER_STEP = 4

CONV_TAPS = 4


def _params(n_axes):
    return pltpu.CompilerParams(
        dimension_semantics=("arbitrary",) * n_axes,
        vmem_limit_bytes=VMEM_LIMIT_BYTES,
    )


def _dot(a, b):
    return jnp.dot(a, b, preferred_element_type=F32)


def _dot_nt(a, b):
    return lax.dot_general(a, b, (((1,), (1,)), ((), ())), preferred_element_type=F32)


def _dot_tn(a, b):
    return lax.dot_general(a, b, (((0,), (0,)), ((), ())), preferred_element_type=F32)


def _split3(x):
    hi = x.astype(BF16)
    r = x - hi.astype(F32)
    mid = r.astype(BF16)
    lo = (r - mid.astype(F32)).astype(BF16)
    return hi, mid, lo


def _select_rows(sel, x):
    return sum(_dot(sel, part) for part in _split3(x))


def _select_cols(x, sel):
    return sum(_dot(part, sel) for part in _split3(x))


def _mm(a, b, passes):
    a_hi, b_hi = a.astype(BF16), b.astype(BF16)
    out = _dot(a_hi, b_hi)
    if passes == 3:
        a_lo = (a - a_hi.astype(F32)).astype(BF16)
        b_lo = (b - b_hi.astype(F32)).astype(BF16)
        out = out + _dot(a_hi, b_lo) + _dot(a_lo, b_hi)
    return out


def _silu(x):
    return x * jax.nn.sigmoid(x)


def _softplus(x):
    return jnp.maximum(x, 0.0) + jnp.log(1.0 + jnp.exp(-jnp.abs(x)))


def _row_tiles(rows, tm):
    n_full, rem = divmod(rows, tm)
    return n_full + (1 if rem else 0), n_full, rem


def _per_row_tile(i, n_full, rem, tm, body):
    if rem == 0:
        body(tm)
        return

    @pl.when(i < n_full)
    def _():
        body(tm)

    @pl.when(i == n_full)
    def _():
        body(rem)


def _row_split(rows, tm):
    n_full, tail = divmod(rows, tm)
    assert tail > 0 and (n_full * tm) % tail == 0
    return n_full, tail


def _mm_kernel(x_ref, xt_ref, w_ref, o_ref, ot_ref):
    @pl.when(pl.program_id(1) == 0)
    def _():
        ot_ref[...] = _dot(xt_ref[...], w_ref[...]).astype(ot_ref.dtype)

    o_ref[...] = _dot(x_ref[...], w_ref[...]).astype(o_ref.dtype)


def matmul(x, w, *, layer, n, tn, tm=2 * ROW_TILE):
    rows, k = x.shape
    n_full, tail = _row_split(rows, tm)
    tail_block = n_full * tm // tail
    return pl.pallas_call(
        _mm_kernel,
        grid=(n // tn, n_full),
        in_specs=[pl.BlockSpec((tm, k), lambda j, i: (i, 0)),
                  pl.BlockSpec((tail, k), lambda j, i: (tail_block, 0)),
                  pl.BlockSpec((None, k, tn), lambda j, i: (layer, 0, j))],
        out_specs=[pl.BlockSpec((tm, tn), lambda j, i: (i, j)),
                   pl.BlockSpec((tail, tn), lambda j, i: (0, j))],
        out_shape=[jax.ShapeDtypeStruct((n_full * tm, n), F32),
                   jax.ShapeDtypeStruct((tail, n), F32)],
        compiler_params=_params(2),
        name="matmul",
    )(x, x, w)


def _mm_tail_kernel(x_ref, w_ref, o_ref):
    o_ref[...] = _dot(x_ref[...], w_ref[...])


def matmul_tail(x, w, *, layer, n, tn, tm=2 * ROW_TILE):
    rows, k = x.shape
    n_full, tail = _row_split(rows, tm)
    tail_block = n_full * tm // tail
    return pl.pallas_call(
        _mm_tail_kernel,
        grid=(n // tn,),
        in_specs=[pl.BlockSpec((tail, k), lambda j: (tail_block, 0)),
                  pl.BlockSpec((None, k, tn), lambda j: (layer, 0, j))],
        out_specs=pl.BlockSpec((tail, tn), lambda j: (0, j)),
        out_shape=jax.ShapeDtypeStruct((tail, n), F32),
        compiler_params=_params(1),
        name="matmul_tail",
    )(x, w)


def _swiglu_kernel(x_ref, xt_ref, wg_ref, wu_ref, o_ref, ot_ref, wgb_ref, wub_ref):
    def hidden(x):
        return (_silu(_dot(x, wgb_ref[...])) * _dot(x, wub_ref[...])).astype(o_ref.dtype)

    @pl.when(pl.program_id(1) == 0)
    def _():
        wgb_ref[...] = wg_ref[...].astype(BF16)
        wub_ref[...] = wu_ref[...].astype(BF16)
        ot_ref[...] = hidden(xt_ref[...])

    part = x_ref.shape[0] // SWIGLU_PARTS
    for p in range(SWIGLU_PARTS):
        rows = slice(p * part, (p + 1) * part)
        o_ref[rows, :] = hidden(x_ref[rows, :])


def swiglu_up(x, w_gu, *, layer, tn=512, tm=4 * ROW_TILE):
    rows, k = x.shape
    f = w_gu.shape[2] // 2
    n_full, tail = _row_split(rows, tm)
    tail_block = n_full * tm // tail
    nj = f // tn
    return pl.pallas_call(
        _swiglu_kernel,
        grid=(nj, n_full),
        in_specs=[pl.BlockSpec((tm, k), lambda j, i: (i, 0)),
                  pl.BlockSpec((tail, k), lambda j, i: (tail_block, 0)),
                  pl.BlockSpec((None, k, tn), lambda j, i: (layer, 0, j)),
                  pl.BlockSpec((None, k, tn), lambda j, i: (layer, 0, j + nj))],
        out_specs=[pl.BlockSpec((tm, tn), lambda j, i: (i, j)),
                   pl.BlockSpec((tail, tn), lambda j, i: (0, j))],
        out_shape=[jax.ShapeDtypeStruct((n_full * tm, f), BF16),
                   jax.ShapeDtypeStruct((tail, f), BF16)],
        scratch_shapes=[pltpu.VMEM((k, tn), BF16), pltpu.VMEM((k, tn), BF16)],
        compiler_params=_params(2),
        name="swiglu_up",
    )(x, x, w_gu, w_gu)


def _postnorm_kernel(a_ref, *refs, scale, tm, n_full, rem, split):
    tail_ref = refs[0] if split else a_ref
    w_ref, x_ref, g_ref, b_ref, y_ref, yb_ref = refs[1:] if split else refs

    def body(r):
        src = a_ref if r == tm else tail_ref
        parts = POSTNORM_PARTS if r == tm else 1
        part = r // parts
        for p in range(parts):
            rows = slice(p * part, (p + 1) * part)
            y = DN_ALPHA * x_ref[rows, :] + scale * _dot(src[rows, :], w_ref[...])
            mu = jnp.mean(y, axis=-1, keepdims=True)
            yc = y - mu
            var = jnp.mean(yc * yc, axis=-1, keepdims=True)
            out = yc * lax.rsqrt(var + LN_EPS) * g_ref[...] + b_ref[...]
            y_ref[rows, :] = out
            yb_ref[rows, :] = out.astype(BF16)

    _per_row_tile(pl.program_id(0), n_full, rem, tm, body)


def matmul_postnorm(a, w, x, g, b, *, layer, scale, a_tail=None, tm=POSTNORM_ROW_TILE):
    rows, d = x.shape
    kdim = a.shape[1]
    steps, n_full, rem = _row_tiles(rows, tm)
    split = a_tail is not None
    assert not split or (a.shape[0] == n_full * tm and a_tail.shape[0] == rem)
    a_specs = [pl.BlockSpec((tm, kdim), lambda i: (jnp.minimum(i, n_full - 1) if split else i, 0))]
    if split:
        a_specs.append(pl.BlockSpec((rem, kdim), lambda i: (0, 0)))
    return pl.pallas_call(
        functools.partial(_postnorm_kernel, scale=scale, tm=tm, n_full=n_full, rem=rem, split=split),
        grid=(steps,),
        in_specs=a_specs + [
            pl.BlockSpec((None, kdim, d), lambda i: (layer, 0, 0), pipeline_mode=pl.Buffered(1)),
            pl.BlockSpec((tm, d), lambda i: (i, 0)),
            pl.BlockSpec((1, d), lambda i: (0, 0)),
            pl.BlockSpec((1, d), lambda i: (0, 0))],
        out_specs=[pl.BlockSpec((tm, d), lambda i: (i, 0)),
                   pl.BlockSpec((tm, d), lambda i: (i, 0))],
        out_shape=[jax.ShapeDtypeStruct((rows, d), F32),
                   jax.ShapeDtypeStruct((rows, d), BF16)],
        compiler_params=_params(1),
        name="matmul_postnorm",
    )(a, *([a_tail] if split else []), w, x, g.reshape(1, d), b.reshape(1, d))


def _strided_rows(r, n, stride):
    return pl.ds(r, n, stride=stride) if stride > 1 else pl.ds(0, n)


def _qkv_split_kernel(x_ref, cos_ref, sin_ref, o_ref, *, dil, n):
    rotated = pl.program_id(2) < (A_SECTIONS - 1) * A_HEADS
    lane = lax.broadcasted_iota(jnp.int32, (n, A_HEAD_DIM), 1)

    def emit(rotate):
        for r in range(dil):
            rows = _strided_rows(r, n, dil)
            x = x_ref[rows, :]
            if rotate:
                partner = jnp.where(lane < A_ROT_HALF,
                                    pltpu.roll(x, A_HEAD_DIM - A_ROT_HALF, axis=1),
                                    pltpu.roll(x, A_ROT_HALF, axis=1))
                x = x * cos_ref[rows, :] + partner * sin_ref[rows, :]
            o_ref[r] = x

    @pl.when(rotated)
    def _():
        emit(True)

    @pl.when(jnp.logical_not(rotated))
    def _():
        emit(False)


def qkv_split(qkv, cos_full, sin_full, *, g, dil, row0, batch, seq, tile):
    n = tile // dil
    tps = seq // tile
    r0 = row0 // tile
    in_col = lambda c: (c // A_HEADS * A_GROUPS + g) * A_HEADS + c % A_HEADS
    return pl.pallas_call(
        functools.partial(_qkv_split_kernel, dil=dil, n=n),
        grid=(batch, tps, A_SECTIONS * A_HEADS),
        in_specs=[pl.BlockSpec((tile, A_HEAD_DIM), lambda b, i, c: (r0 + b * tps + i, in_col(c))),
                  pl.BlockSpec((tile, A_HEAD_DIM), lambda b, i, c: (r0 + b * tps + i, 0)),
                  pl.BlockSpec((tile, A_HEAD_DIM), lambda b, i, c: (r0 + b * tps + i, 0))],
        out_specs=pl.BlockSpec((None, dil, n, A_HEAD_DIM), lambda b, i, c: (b, 0, i, c)),
        out_shape=jax.ShapeDtypeStruct((batch, dil, seq // dil, A_SECTIONS * A_GROUP_COLS), F32),
        compiler_params=_params(3),
        name="qkv_split",
    )(qkv, cos_full, sin_full)


def _rotate_pairs(x, cos, sin, lane):
    partner = jnp.where(lane < A_ROT_HALF,
                        pltpu.roll(x, A_HEAD_DIM - A_ROT_HALF, axis=1),
                        pltpu.roll(x, A_ROT_HALF, axis=1))
    return x * cos + partner * sin


def _qkv_project_kernel(x_ref, w_ref, cos_ref, sin_ref, o_ref, head_ref, *, dil, tile):
    part = tile // A_PROJECT_PARTS
    n = part // dil
    lane = lax.broadcasted_iota(jnp.int32, (part, A_HEAD_DIM), 1)
    for p in range(A_PROJECT_PARTS):
        rows = slice(p * part, (p + 1) * part)
        res = _dot(x_ref[rows, :], w_ref[...])
        cos, sin = cos_ref[rows, :], sin_ref[rows, :]
        for h in range(A_HEADS):
            cols = slice(h * A_HEAD_DIM, (h + 1) * A_HEAD_DIM)
            rot = _rotate_pairs(res[:, cols], cos, sin, lane)
            if dil == 1:
                o_ref[0, rows, cols] = rot
                continue
            head_ref[p, h] = rot
            for r in range(dil):
                o_ref[r, p * n:(p + 1) * n, cols] = head_ref[p, h, pl.ds(r, n, stride=dil), :]


def qkv_project(xb, w_qkv, cos2, sin2, *, layer, g, batch, seq, tile=ROW_TILE):
    d = xb.shape[1]
    dil = A_DILATIONS[g]
    tps = seq // tile
    table = pl.BlockSpec((None, tile, A_HEAD_DIM),
                         lambda b, i, s: (jnp.where(s == A_SECTIONS - 1, 1, 0), b * tps + i, 0))
    return pl.pallas_call(
        functools.partial(_qkv_project_kernel, dil=dil, tile=tile),
        grid=(batch, tps, A_SECTIONS),
        in_specs=[pl.BlockSpec((tile, d), lambda b, i, s: (b * tps + i, 0)),
                  pl.BlockSpec((None, d, A_GROUP_COLS), lambda b, i, s: (layer, 0, s * A_GROUPS + g)),
                  table, table],
        out_specs=pl.BlockSpec((None, dil, tile // dil, A_GROUP_COLS), lambda b, i, s: (b, 0, i, s)),
        out_shape=jax.ShapeDtypeStruct((batch, dil, seq // dil, A_SECTIONS * A_GROUP_COLS), F32),
        scratch_shapes=[pltpu.VMEM((A_PROJECT_PARTS, A_HEADS, tile // A_PROJECT_PARTS, A_HEAD_DIM), F32)],
        compiler_params=_params(3),
        name="qkv_project",
    )(xb, w_qkv, cos2, sin2)


def _attn_prompt_kernel(q_ref, kp_ref, kc_ref, vp_ref, vc_ref, o_ref, lse_ref):
    lb = pl.program_id(2)
    n = A_BLOCK
    qi = lax.broadcasted_iota(jnp.int32, (n, n), 0)
    kj = lax.broadcasted_iota(jnp.int32, (n, n), 1)
    mask_cur = kj <= qi
    band_prev = kj >= qi
    mask_prev = [jnp.logical_and(band_prev, lb > 0), band_prev]
    scale = A_HEAD_DIM ** -0.5
    cols = [slice(h * A_HEAD_DIM, (h + 1) * A_HEAD_DIM) for h in range(A_HEADS)]
    rows = [slice(0, n), slice(n, 2 * n)]
    k = [[kp_ref[:, c].astype(BF16), kc_ref[rows[0], c].astype(BF16), kc_ref[rows[1], c].astype(BF16)] for c in cols]
    v = [[vp_ref[:, c].astype(BF16), vc_ref[rows[0], c].astype(BF16), vc_ref[rows[1], c].astype(BF16)] for c in cols]
    chains = [(h, s) for h in range(A_HEADS) for s in range(2)]
    q = [q_ref[rows[s], cols[h]].astype(BF16) for h, s in chains]
    s_c = [jnp.where(mask_cur, _dot_nt(q[i], k[h][s + 1]) * scale, NEG_BIG) for i, (h, s) in enumerate(chains)]
    s_p = [jnp.where(mask_prev[s], _dot_nt(q[i], k[h][s]) * scale, NEG_BIG) for i, (h, s) in enumerate(chains)]
    m = [jnp.maximum(jnp.max(a, axis=-1, keepdims=True), jnp.max(b, axis=-1, keepdims=True)) for a, b in zip(s_c, s_p)]
    p_c = [jnp.exp(a - mx) for a, mx in zip(s_c, m)]
    p_p = [jnp.exp(b - mx) for b, mx in zip(s_p, m)]
    l = [jnp.sum(a, axis=-1, keepdims=True) + jnp.sum(b, axis=-1, keepdims=True) for a, b in zip(p_c, p_p)]
    o = [_dot(p_c[i].astype(BF16), v[h][s + 1]) + _dot(p_p[i].astype(BF16), v[h][s]) for i, (h, s) in enumerate(chains)]
    lse_all = [jnp.zeros((n, LANES), F32), jnp.zeros((n, LANES), F32)]
    for i, (h, s) in enumerate(chains):
        o_ref[rows[s], cols[h]] = o[i] / l[i]
        lse_all[s] = jnp.where(kj == h, m[i] + jnp.log(l[i]), lse_all[s])
    for s in range(2):
        lse_ref[rows[s], :] = lse_all[s]


def attn_prompt(qkv_r):
    batch, dil, length, _ = qkv_r.shape
    nb = length // (2 * A_BLOCK)
    own = lambda sec: pl.BlockSpec((None, None, 2 * A_BLOCK, A_GROUP_COLS), lambda b, r, lb: (b, r, lb, sec))
    before = lambda sec: pl.BlockSpec(
        (None, None, A_BLOCK, A_GROUP_COLS), lambda b, r, lb: (b, r, jnp.maximum(2 * lb - 1, 0), sec))
    return pl.pallas_call(
        _attn_prompt_kernel,
        grid=(batch, dil, nb),
        in_specs=[own(0), before(1), own(1), before(2), own(2)],
        out_specs=[pl.BlockSpec((None, None, 2 * A_BLOCK, A_GROUP_COLS), lambda b, r, lb: (b, r, lb, 0)),
                   pl.BlockSpec((None, None, 2 * A_BLOCK, LANES), lambda b, r, lb: (b, r, lb, 0))],
        out_shape=[jax.ShapeDtypeStruct((batch, dil, length, A_GROUP_COLS), F32),
                   jax.ShapeDtypeStruct((batch, dil, length, LANES), F32)],
        compiler_params=_params(3),
        name="attn_prompt",
    )(qkv_r, qkv_r, qkv_r, qkv_r, qkv_r)


def _attn_sample_kernel(q_ref, kn_ref, vn_ref, *refs, dil, ts):
    n_res = len(refs) // 2 - 1
    cache_refs, (o_ref, lse_ref) = refs[:2 * n_res], refs[2 * n_res:]
    scale = A_HEAD_DIM ** -0.5
    jc = lax.broadcasted_iota(jnp.int32, (A_BLOCK, 1, 1), 0)
    un = lax.broadcasted_iota(jnp.int32, (ts, 1, 1), 0)
    kn = kn_ref[...]
    vn = vn_ref[...]
    for t in range(ts):
        kc_ref, vc_ref = cache_refs[2 * (t % n_res)], cache_refs[2 * (t % n_res) + 1]
        valid_n = (un <= t) if dil == 1 else (un == t)
        q = q_ref[t:t + 1]
        s_c = jnp.sum(kc_ref[...] * q, axis=-1, keepdims=True) * scale
        if dil == 1:
            s_c = jnp.where(jc >= t, s_c, NEG_BIG)
        s_n = jnp.where(valid_n, jnp.sum(kn * q, axis=-1, keepdims=True) * scale, NEG_BIG)
        m = jnp.maximum(jnp.max(s_c, axis=0, keepdims=True), jnp.max(s_n, axis=0, keepdims=True))
        p_c = jnp.exp(s_c - m)
        p_n = jnp.exp(s_n - m)
        l = jnp.sum(p_c, axis=0, keepdims=True) + jnp.sum(p_n, axis=0, keepdims=True)
        o = jnp.sum(p_c * vc_ref[...], axis=0, keepdims=True) + jnp.sum(p_n * vn, axis=0, keepdims=True)
        o_ref[t:t + 1] = o / l
        lse_ref[t:t + 1] = jnp.broadcast_to(m + jnp.log(l), (1, A_HEADS, A_HEAD_DIM))


def attn_sample(q, k_new, v_new, caches, *, layer, dil):
    db, ts = q.shape[:2]
    wb = caches.shape[2]
    assert wb == A_BLOCK * dil and (dil == 1 or ts <= dil)
    cache_v = caches.reshape(caches.shape[0], db, A_BLOCK, dil, 2, A_HEADS, A_HEAD_DIM)
    n_res = 1 if dil == 1 else ts
    tok_spec = pl.BlockSpec((None, ts, A_HEADS, A_HEAD_DIM), lambda b: (b, 0, 0, 0))
    cache_specs = [pl.BlockSpec((None, None, A_BLOCK, None, None, A_HEADS, A_HEAD_DIM),
                                lambda b, r=r, kv=kv: (layer, b, 0, r, kv, 0, 0))
                   for r in range(n_res) for kv in range(2)]
    out = jax.ShapeDtypeStruct((db, ts, A_HEADS, A_HEAD_DIM), F32)
    return pl.pallas_call(
        functools.partial(_attn_sample_kernel, dil=dil, ts=ts),
        grid=(db,),
        in_specs=[tok_spec, tok_spec, tok_spec] + cache_specs,
        out_specs=[tok_spec, tok_spec],
        out_shape=[out, out],
        compiler_params=_params(1),
        name="attn_sample",
    )(q, k_new, v_new, *([cache_v] * (2 * n_res)))


def _merge_groups(head_out, lses, a_ref):
    m = functools.reduce(jnp.maximum, lses)
    es = [jnp.exp(l - m) for l in lses]
    den = sum(es)
    ws = [e / den for e in es]
    for h in range(A_HEADS):
        a = sum(w[:, h:h + 1] * head_out(g, h) for g, w in enumerate(ws))
        a_ref[:, h * A_HEAD_DIM:(h + 1) * A_HEAD_DIM] = a.astype(a_ref.dtype)


def _attn_merge_rows_kernel(*refs):
    outs = [r[...] for r in refs[:A_GROUPS]]
    lses = [r[...] for r in refs[A_GROUPS:2 * A_GROUPS]]
    m = functools.reduce(jnp.maximum, lses)
    es = [jnp.exp(l - m) for l in lses]
    den = sum(es)
    refs[2 * A_GROUPS][...] = sum(e / den * o for e, o in zip(es, outs))


def attn_merge_rows(outs, lses):
    spec = pl.BlockSpec(outs[0].shape, lambda i: (0, 0, 0, 0))
    return pl.pallas_call(
        _attn_merge_rows_kernel,
        grid=(1,),
        in_specs=[spec] * (2 * A_GROUPS),
        out_specs=spec,
        out_shape=jax.ShapeDtypeStruct(outs[0].shape, F32),
        compiler_params=_params(1),
        name="attn_merge_rows",
    )(*outs, *lses)


def _attn_merge_prompt_kernel(*refs, tile):
    o_refs, l_refs = refs[:A_GROUPS], refs[A_GROUPS:2 * A_GROUPS]
    a_ref, o_nat, l_nat = refs[2 * A_GROUPS:]
    lses = []
    for g, dil in enumerate(A_DILATIONS):
        if dil == 1:
            lses.append(l_refs[g][0])
            continue
        n = tile // dil
        for r in range(dil):
            rows = pl.ds(r, n, stride=dil)
            l_nat[g, rows, :] = l_refs[g][r]
            for h in range(A_HEADS):
                o_nat[g * A_HEADS + h, rows, :] = o_refs[g][r, :, h * A_HEAD_DIM:(h + 1) * A_HEAD_DIM]
        lses.append(l_nat[g])

    def head_out(g, h):
        if A_DILATIONS[g] == 1:
            return o_refs[g][0, :, h * A_HEAD_DIM:(h + 1) * A_HEAD_DIM]
        return o_nat[g * A_HEADS + h]

    _merge_groups(head_out, lses, a_ref)


def attn_merge_prompt(outs, lses, *, tile=ROW_TILE):
    batch, _, seq, cols = outs[0].shape
    tps = seq // tile
    in_specs = []
    for width in (cols, LANES):
        for dil in A_DILATIONS:
            in_specs.append(pl.BlockSpec((None, dil, tile // dil, width), lambda b, i: (b, 0, i, 0)))
    return pl.pallas_call(
        functools.partial(_attn_merge_prompt_kernel, tile=tile),
        grid=(batch, tps),
        in_specs=in_specs,
        out_specs=pl.BlockSpec((tile, cols), lambda b, i: (b * tps + i, 0)),
        out_shape=jax.ShapeDtypeStruct((batch * seq, cols), BF16),
        scratch_shapes=[pltpu.VMEM((A_GROUPS * A_HEADS, tile, A_HEAD_DIM), F32),
                        pltpu.VMEM((A_GROUPS, tile, LANES), F32)],
        compiler_params=_params(2),
        name="attn_merge_prompt",
    )(*outs, *lses)


def _conv_silu_chunk(first, u_ref, prev_ref, w_ref, bias, full_ref, out_ref, col0):
    c, width = u_ref.shape
    cols = slice(col0, col0 + width)

    @pl.when(first)
    def _():
        full_ref[0:SUBLANES, cols] = prev_ref[...]

    u = u_ref[...]
    full_ref[SUBLANES:SUBLANES + c, cols] = u
    acc = u * w_ref[CONV_TAPS - 1:CONV_TAPS, :]
    if bias is not None:
        acc = acc + bias
    for s in range(1, CONV_TAPS):
        acc = acc + full_ref[SUBLANES - s:SUBLANES - s + c, cols] * w_ref[CONV_TAPS - 1 - s:CONV_TAPS - s, :]
    out_ref[:, cols] = _silu(acc)
    full_ref[0:SUBLANES, cols] = u[c - SUBLANES:c, :]


def _unit_lower_inverses(mats, n, top):
    blk = min(B_INV_BLOCK, top)
    ii = lax.broadcasted_iota(jnp.int32, (n, n), 0)
    jj = lax.broadcasted_iota(jnp.int32, (n, n), 1)
    eye = jnp.where(ii == jj, 1.0, 0.0).astype(F32)
    shift = blk.bit_length() - 1
    same = (ii >> shift) == (jj >> shift)
    ps = [jnp.where(same, -a, 0.0) for a in mats]
    xs = [eye + p for p in ps]
    for _ in range(shift - 1):
        ps = [_mm(p, p, B_INV_PASSES) for p in ps]
        xs = [x + _mm(x, p, B_INV_PASSES) for x, p in zip(xs, ps)]
    size = blk
    while size < top:
        shift += 1
        same_next = (ii >> shift) == (jj >> shift)
        sel = jnp.logical_and(same_next, jnp.logical_not(same))
        ys = [_mm(x, jnp.where(sel, a, 0.0), B_INV_PASSES) for x, a in zip(xs, mats)]
        xs = [x - _mm(y, x, B_INV_PASSES) for x, y in zip(xs, ys)]
        same = same_next
        size *= 2
    return xs


def _gdn_kernel(qr_ref, kr_ref, vr_ref, qp_ref, kp_ref, vp_ref, qw_ref, kw_ref, vw_ref,
                z_ref, ba_ref, alog_ref, dtb_ref, nw_ref, s0_ref,
                o_ref, s_ref, full_ref, conv_ref, *, c, hps, t_valid):
    hb = pl.program_id(1)
    ci = pl.program_id(2)
    dk = B_HEAD_DIM
    c2 = 2 * c
    assert c2 == LANES

    @pl.when(ci == 0)
    def _():
        s_ref[...] = s0_ref[...]

    qw = hps // 2 * dk
    for raw, prev, taps, col0 in ((qr_ref, qp_ref, qw_ref, 0), (kr_ref, kp_ref, kw_ref, qw),
                                  (vr_ref, vp_ref, vw_ref, 2 * qw)):
        _conv_silu_chunk(ci == 0, raw, prev, taps, None, full_ref, conv_ref, col0)
    q_ref = conv_ref.at[:, 0:qw]
    k_ref = conv_ref.at[:, qw:2 * qw]
    v_ref = conv_ref.at[:, 2 * qw:]

    ba = ba_ref[...]
    valid = lax.broadcasted_iota(jnp.int32, (c, LANES), 0) + ci * c < t_valid
    beta_all = jnp.where(valid, jax.nn.sigmoid(ba), 0.0)
    g_all = jnp.where(valid, -jnp.exp(alog_ref[pl.ds(hb, 1), :]) * _softplus(ba + dtb_ref[pl.ds(hb, 1), :]), 0.0)
    ci_, cj_ = lax.broadcasted_iota(jnp.int32, (c, c), 0), lax.broadcasted_iota(jnp.int32, (c, c), 1)
    gc_all = _select_rows(jnp.where(ci_ >= cj_, 1.0, 0.0).astype(BF16), g_all)
    gc_t = jnp.concatenate([gc_all, gc_all], axis=0).T
    ii = lax.broadcasted_iota(jnp.int32, (c2, c2), 0)
    jj = lax.broadcasted_iota(jnp.int32, (c2, c2), 1)
    shift = c.bit_length() - 1
    same = (ii >> shift) == (jj >> shift)
    incl = jnp.logical_and(same, ii >= jj)
    strict = jnp.logical_and(same, ii > jj)
    first = lax.broadcasted_iota(jnp.int32, (1, c2), 1) < c
    top = lax.broadcasted_iota(jnp.int32, (c2, 1), 0) < c
    nw = nw_ref[...]

    def stack_cols(arr, l0, l1):
        return jnp.concatenate([arr[:, l0:l0 + 1], arr[:, l1:l1 + 1]], axis=0)

    def own_half(r):
        return jnp.where(top, r[:, :dk], r[:, dk:])

    pairs = range(hps // 2)
    cols = lambda h: slice(h * dk, (h + 1) * dk)

    def prepare(qh):
        h0, h1 = 2 * qh, 2 * qh + 1
        q = q_ref[:, cols(qh)]
        k = k_ref[:, cols(qh)]
        q = q * lax.rsqrt(jnp.sum(q * q, axis=-1, keepdims=True) + 1e-6) * (dk ** -0.5)
        k = k * lax.rsqrt(jnp.sum(k * k, axis=-1, keepdims=True) + 1e-6)
        q2 = jnp.concatenate([q, q], axis=0)
        k2 = jnp.concatenate([k, k], axis=0)
        k2_b = k2.astype(BF16)
        beta = stack_cols(beta_all, h0, h1)
        gc = stack_cols(gc_all, hps + h0, hps + h1)
        gr = jnp.where(first, gc_t[hps + h0:hps + h0 + 1, :], gc_t[hps + h1:hps + h1 + 1, :])
        gl0 = gc_all[c - 1:c, hps + h0:hps + h0 + 1]
        gl1 = gc_all[c - 1:c, hps + h1:hps + h1 + 1]
        decay = jnp.exp(jnp.where(incl, gc - gr, NEG_BIG))
        eg = jnp.exp(gc)
        v2 = jnp.concatenate([v_ref[:, cols(h0)], v_ref[:, cols(h1)]], axis=0)
        return dict(
            a=jnp.where(strict, _dot_nt(k2_b, k2_b) * decay, 0.0) * beta,
            qk=jnp.where(incl, _dot_nt(q2.astype(BF16), k2_b) * decay, 0.0).astype(BF16),
            rhs=jnp.concatenate([v2 * beta, k2 * (beta * eg)], axis=1),
            q_eg=(q2 * eg).astype(BF16),
            k_dec=(k2 * jnp.exp(jnp.where(top, gl0, gl1) - gc)).astype(BF16),
            dec_cat=jnp.concatenate([jnp.broadcast_to(jnp.exp(gl0), (1, dk)),
                                     jnp.broadcast_to(jnp.exp(gl1), (1, dk))], axis=1))

    st = [prepare(qh) for qh in pairs]
    t_inv = _unit_lower_inverses([s["a"] for s in st], c2, c)
    uw = [_mm(t, s["rhs"], B_INV_PASSES) for t, s in zip(t_inv, st)]
    s_cat = [jnp.concatenate([s_ref[2 * qh], s_ref[2 * qh + 1]], axis=1) for qh in pairs]
    s_cat_b = [s.astype(BF16) for s in s_cat]
    v_new = [r[:, :dk] - own_half(_dot(r[:, dk:].astype(BF16), sb)) for r, sb in zip(uw, s_cat_b)]
    v_new_b = [v.astype(BF16) for v in v_new]
    outs = [own_half(_dot(s["q_eg"], sb)) + _dot(s["qk"], vb) for s, sb, vb in zip(st, s_cat_b, v_new_b)]
    for qh in pairs:
        h0, h1 = 2 * qh, 2 * qh + 1
        v = v_new[qh]
        v_blk = jnp.concatenate([jnp.where(top, v, 0.0), jnp.where(top, 0.0, v)], axis=1).astype(BF16)
        s_new = s_cat[qh] * st[qh]["dec_cat"] + _dot_tn(st[qh]["k_dec"], v_blk)
        s_ref[h0] = s_new[:, :dk]
        s_ref[h1] = s_new[:, dk:]
        o = outs[qh]
        o = o * lax.rsqrt(jnp.mean(o * o, axis=-1, keepdims=True) + RMS_EPS) * nw
        o = (o * _silu(jnp.concatenate([z_ref[:, cols(h0)], z_ref[:, cols(h1)]], axis=0))).astype(o_ref.dtype)
        o_ref[:, cols(h0)] = o[:c]
        o_ref[:, cols(h1)] = o[c:]


def gdn_scan(proj, conv_prev, conv_taps, ba, alog_pad, dtb_pad, norm_w, s0, *, nseq, seq, t_valid):
    c = B_CHUNK
    hps = B_HEADS_PER_STEP
    nc = seq // c
    ng = B_V_HEADS // hps
    qw = hps // 2 * B_HEAD_DIM
    vw = hps * B_HEAD_DIM
    par_rows = alog_pad.shape[0]
    row = lambda s, h, ci: s * nc + ci
    k0, v0, z0 = B_KEY_DIM // qw, 2 * B_KEY_DIM // vw, B_CONV_DIM // vw
    chunk = lambda width, col0: pl.BlockSpec((c, width), lambda s, h, ci: (row(s, h, ci), col0 + h))
    state = lambda width, col0: pl.BlockSpec((None, SUBLANES, width), lambda s, h, ci: (s, 0, col0 + h))
    taps = lambda width, col0: pl.BlockSpec((SUBLANES, width), lambda s, h, ci: (0, col0 + h))
    params = pl.BlockSpec((par_rows, LANES), lambda s, h, ci: (0, 0))
    heads = pl.BlockSpec((None, hps, B_HEAD_DIM, B_HEAD_DIM), lambda s, h, ci: (s, h, 0, 0))
    return pl.pallas_call(
        functools.partial(_gdn_kernel, c=c, hps=hps, t_valid=t_valid),
        grid=(nseq, ng, nc),
        in_specs=[chunk(qw, 0), chunk(qw, k0), chunk(vw, v0),
                  state(qw, 0), state(qw, k0), state(vw, v0),
                  taps(qw, 0), taps(qw, k0), taps(vw, v0),
                  chunk(vw, z0), chunk(LANES, 0), params, params,
                  pl.BlockSpec((1, B_HEAD_DIM), lambda s, h, ci: (0, 0)), heads],
        out_specs=[chunk(vw, 0), heads],
        out_shape=[jax.ShapeDtypeStruct((nseq * seq, B_VAL_DIM), BF16),
                   jax.ShapeDtypeStruct((nseq, B_V_HEADS, B_HEAD_DIM, B_HEAD_DIM), F32)],
        scratch_shapes=[pltpu.VMEM((SUBLANES + c, 2 * qw + vw), F32),
                        pltpu.VMEM((c, 2 * qw + vw), F32)],
        compiler_params=_params(3),
        name="gdn_scan",
    )(proj, proj, proj, conv_prev, conv_prev, conv_prev, conv_taps, conv_taps, conv_taps,
      proj, ba, alog_pad, dtb_pad, norm_w, s0)


def _ssd_kernel(xr_ref, br_ref, cr_ref, xp_ref, bp_ref, cp_ref, xw_ref, bw_ref, cw_ref,
                xbias_ref, bbias_ref, cbias_ref, z_ref, dt_ref, dtb_ref, alog_ref, dskip_ref, nw_ref,
                h0_ref, y_ref, hout_ref, ht_ref, full_ref, conv_ref, *, c, nc, t_valid):
    gi = pl.program_id(1)
    ci = pl.program_id(2)
    hpg = C_HEADS_PER_GROUP
    p = C_HEAD_DIM
    n = C_D_STATE
    gcols = hpg * p
    groups = range(C_GROUPS_PER_STEP)
    xw_all, bw_all = C_GROUPS_PER_STEP * gcols, C_GROUPS_PER_STEP * n

    @pl.when(ci == 0)
    def _():
        for k in groups:
            ht_ref[k] = h0_ref[k * hpg:(k + 1) * hpg].reshape(gcols, n).T

    for raw, prev, taps, bias, col0 in ((xr_ref, xp_ref, xw_ref, xbias_ref, 0),
                                        (br_ref, bp_ref, bw_ref, bbias_ref, xw_all),
                                        (cr_ref, cp_ref, cw_ref, cbias_ref, xw_all + bw_all)):
        _conv_silu_chunk(ci == 0, raw, prev, taps, bias[...], full_ref, conv_ref, col0)
    x = [conv_ref[:, k * gcols:(k + 1) * gcols] for k in groups]
    bm = [conv_ref[:, xw_all + k * n:xw_all + (k + 1) * n].astype(BF16) for k in groups]
    cm = [conv_ref[:, xw_all + bw_all + k * n:xw_all + bw_all + (k + 1) * n].astype(BF16) for k in groups]

    valid = lax.broadcasted_iota(jnp.int32, (c, LANES), 0) + ci * c < t_valid
    par = lambda ref, k: ref[pl.ds(gi * C_GROUPS_PER_STEP + k, 1), :]
    dt = [jnp.where(valid, _softplus(dt_ref[:, k * LANES:(k + 1) * LANES] + par(dtb_ref, k)), 0.0) for k in groups]
    la = [dt[k] * -jnp.exp(par(alog_ref, k)) for k in groups]
    ii = lax.broadcasted_iota(jnp.int32, (c, c), 0)
    jj = lax.broadcasted_iota(jnp.int32, (c, c), 1)
    incl = ii >= jj
    tril = jnp.where(incl, 1.0, 0.0).astype(BF16)
    acs = [_select_rows(tril, la[k]) for k in groups]
    acs_t = [a.T for a in acs]
    el = lax.broadcasted_iota(jnp.int32, (LANES, gcols), 0)
    ej = lax.broadcasted_iota(jnp.int32, (LANES, gcols), 1)
    expand = jnp.where(el == (ej >> (p.bit_length() - 1)), 1.0, 0.0).astype(BF16)
    dt_e = [_select_cols(dt[k], expand) for k in groups]
    acs_e = [_select_cols(acs[k], expand) for k in groups]
    dskip_e = [_select_cols(jnp.broadcast_to(par(dskip_ref, k), (SUBLANES, LANES)), expand)[0:1, :] for k in groups]

    xdt = [(x[k] * dt_e[k]).astype(BF16) for k in groups]
    cb = [_dot_nt(cm[k], bm[k]) for k in groups]
    ht = [ht_ref[k] for k in groups]
    y = [_dot(cm[k], ht[k].astype(BF16)) * jnp.exp(acs_e[k]) + dskip_e[k] * x[k] for k in groups]
    lane = lax.broadcasted_iota(jnp.int32, (c, 2 * p), 1)
    heads = [(k, hd) for k in groups for hd in range(hpg)]
    decay = [jnp.exp(jnp.where(incl, acs[k][:, hd:hd + 1] - acs_t[k][hd:hd + 1, :], NEG_BIG)) for k, hd in heads]
    mats = [(cb[k] * dec).astype(BF16) for (k, hd), dec in zip(heads, decay)]
    ys = [_dot(m, xdt[k][:, (hd // 2) * 2 * p:(hd // 2 + 1) * 2 * p]) for (k, hd), m in zip(heads, mats)]
    for k in groups:
        yk = ys[k * hpg:(k + 1) * hpg]
        yt = y[k] + jnp.concatenate([jnp.where(lane < p, yk[2 * j], yk[2 * j + 1]) for j in range(hpg // 2)], axis=1)
        cols = slice(k * gcols, (k + 1) * gcols)
        yt = yt * _silu(z_ref[:, cols])
        yt = yt * lax.rsqrt(jnp.mean(yt * yt, axis=-1, keepdims=True) + RMS_EPS) * nw_ref[:, cols]
        y_ref[:, cols] = yt.astype(y_ref.dtype)

    last_e = [acs_e[k][c - 1:c, :] for k in groups]
    xw = [(x[k] * (jnp.exp(last_e[k] - acs_e[k]) * dt_e[k])).astype(BF16) for k in groups]
    ht_new = [ht[k] * jnp.exp(last_e[k]) + _dot_tn(bm[k], xw[k]) for k in groups]
    for k in groups:
        ht_ref[k] = ht_new[k]

    @pl.when(ci == nc - 1)
    def _():
        for k in groups:
            hout_ref[k * hpg:(k + 1) * hpg] = ht_new[k].T.reshape(hpg, p, n)


def ssd_scan(proj, conv_prev, conv_taps, conv_bias, dt, dtb_pad, alog_pad, dskip_pad, norm_w, h0,
             *, nseq, seq, t_valid):
    c = C_CHUNK
    nc = seq // c
    gps = C_GROUPS_PER_STEP
    gc = C_GROUP_COLS * gps
    n = C_D_STATE * gps
    steps = C_GROUPS // gps
    row = lambda s, g, ci: s * nc + ci
    px, pb_, pc_ = C_D_INNER // gc, 2 * C_D_INNER // n, 2 * C_D_INNER // n + steps
    cb_, cc_ = C_D_INNER // n, C_D_INNER // n + steps
    chunk = lambda width, col0: pl.BlockSpec((c, width), lambda s, g, ci: (row(s, g, ci), col0 + g))
    state = lambda width, col0: pl.BlockSpec((None, SUBLANES, width), lambda s, g, ci: (s, 0, col0 + g))
    taps = lambda width, col0: pl.BlockSpec((SUBLANES, width), lambda s, g, ci: (0, col0 + g))
    bias = lambda width, col0: pl.BlockSpec((1, width), lambda s, g, ci: (0, col0 + g))
    par = pl.BlockSpec((C_GROUPS, LANES), lambda s, g, ci: (0, 0))
    heads = pl.BlockSpec((None, C_HEADS_PER_GROUP * gps, C_HEAD_DIM, C_D_STATE), lambda s, g, ci: (s, g, 0, 0))
    width = gc + 2 * n
    return pl.pallas_call(
        functools.partial(_ssd_kernel, c=c, nc=nc, t_valid=t_valid),
        grid=(nseq, steps, nc),
        in_specs=[chunk(gc, px), chunk(n, pb_), chunk(n, pc_),
                  state(gc, 0), state(n, cb_), state(n, cc_),
                  taps(gc, 0), taps(n, cb_), taps(n, cc_),
                  bias(gc, 0), bias(n, cb_), bias(n, cc_),
                  chunk(gc, 0), chunk(LANES * gps, 0), par, par, par, bias(gc, 0), heads],
        out_specs=[chunk(gc, 0), heads],
        out_shape=[jax.ShapeDtypeStruct((nseq * seq, C_D_INNER), BF16),
                   jax.ShapeDtypeStruct((nseq, C_HEADS, C_HEAD_DIM, C_D_STATE), F32)],
        scratch_shapes=[pltpu.VMEM((gps, C_D_STATE, C_GROUP_COLS), F32),
                        pltpu.VMEM((SUBLANES + c, width), F32),
                        pltpu.VMEM((c, width), F32)],
        compiler_params=_params(3),
        name="ssd_scan",
    )(proj, proj, proj, conv_prev, conv_prev, conv_prev, conv_taps, conv_taps, conv_taps,
      conv_bias, conv_bias, conv_bias, proj, dt, dtb_pad, alog_pad, dskip_pad, norm_w, h0)


def _rope_tables(mp, seq, n_sample, ts):
    pos = jnp.concatenate([jnp.arange(mp) % seq, PAST_LEN + jnp.arange(n_sample) % ts]).astype(F32)
    inv_freq = ROPE_THETA ** (-jnp.arange(A_ROT_HALF, dtype=F32) / A_ROT_HALF)
    ang = pos[:, None] * inv_freq[None, :]
    cos, sin = jnp.cos(ang), jnp.sin(ang)
    rest = A_HEAD_DIM - 2 * A_ROT_HALF
    rows = pos.shape[0]
    cos_full = jnp.concatenate([cos, cos, jnp.ones((rows, rest), F32)], axis=1)
    sin_full = jnp.concatenate([-sin, sin, jnp.zeros((rows, rest), F32)], axis=1)
    return cos_full, sin_full


def _lane_groups(cols, group):
    lead = cols.shape[:-1]
    n = cols.shape[-1]
    c = cols.reshape(*lead, n // group, group)
    c = jnp.pad(c, [(0, 0)] * (c.ndim - 1) + [(0, LANES - group)])
    return c.reshape(*lead, n // group * LANES)


def _pad_seq_rows(x, nseq, ts, seq):
    cols = x.shape[-1]
    x = jnp.pad(x.reshape(nseq, ts, cols), ((0, 0), (0, seq - ts), (0, 0)))
    return x.reshape(nseq * seq, cols)


def _conv_prev(state):
    return jnp.pad(state, ((0, 0), (SUBLANES - state.shape[1], 0), (0, 0)))


def _conv_taps(w):
    return jnp.pad(w, ((0, SUBLANES - w.shape[0]), (0, 0)))


def _last_rows(x, nseq, seq, n, col0, cols):
    return jnp.stack([x[(s + 1) * seq - n:(s + 1) * seq, col0:col0 + cols] for s in range(nseq)])


def kernel(x_prompt, x_sample, cache_a_kv_w128, cache_a_kv_w512, cache_a_kv_w2048, state_b_ssm, state_b_conv, state_c_ssm, state_c_conv, ln_g, ln_b, ffn1_w_gu, ffn1_w_down, ffn2_w_gu, ffn2_w_down, a_w_qkv, a_w_o, b_w_in, b_conv_w, b_a_log, b_dt_bias, b_norm_w, b_w_out, c_w_in, c_conv_w, c_conv_b, c_dt_bias, c_a_log, c_d, c_norm_w, c_w_out):
    a_bufs = (cache_a_kv_w128, cache_a_kv_w512, cache_a_kv_w2048)
    bp, seq, d = x_prompt.shape
    db, ts, _ = x_sample.shape
    mp, ms = bp * seq, db * ts
    assert seq % (A_BLOCK * max(A_DILATIONS)) == 0 and mp % ms == 0
    x = jnp.concatenate([x_prompt.reshape(mp, d), x_sample.reshape(ms, d)], axis=0)
    xb = x.astype(BF16)
    cos_full, sin_full = _rope_tables(mp, seq, ms, ts)
    cos2 = jnp.stack([cos_full, jnp.ones_like(cos_full)])
    sin2 = jnp.stack([sin_full, jnp.zeros_like(sin_full)])
    wb16 = {name: w.astype(BF16) for name, w in dict(
        ffn1_down=ffn1_w_down, ffn2_down=ffn2_w_down, a_qkv=a_w_qkv, a_o=a_w_o,
        b_in=b_w_in[:, :, :B_CONV_DIM + B_VAL_DIM], b_out=b_w_out,
        c_in=c_w_in[:, :, :C_D_INNER + C_CONV_DIM], c_out=c_w_out).items()}
    w_gu = dict(ffn1=ffn1_w_gu, ffn2=ffn2_w_gu)

    def valid_rows(a_s, rows):
        return a_s.reshape(db, rows, -1)[:, :ts].reshape(ms, -1)

    outs = {}

    def mixer_a(xb, j):
        qkv_s = matmul_tail(xb, wb16["a_qkv"], layer=j, n=a_w_qkv.shape[2], tn=1024)
        gw = A_GROUP_COLS
        o_p, l_p, o_s, l_s = [], [], [], []
        for g, dil in enumerate(A_DILATIONS):
            qkv_r = qkv_project(xb, wb16["a_qkv"], cos2, sin2, layer=j, g=g, batch=bp, seq=seq)
            o, lse = attn_prompt(qkv_r)
            o_p.append(o)
            l_p.append(lse)
            new = qkv_split(qkv_s, cos_full[mp:], sin_full[mp:], g=g, dil=1, row0=0, batch=1, seq=ms, tile=ms)
            q_s, k_s, v_s = (new[0, 0, :, sec * gw:(sec + 1) * gw].reshape(db, ts, A_HEADS, A_HEAD_DIM)
                             for sec in range(A_SECTIONS))
            o, lse = attn_sample(q_s, k_s, v_s, a_bufs[g], layer=j, dil=dil)
            o_s.append(o)
            l_s.append(lse)
            length = seq // dil
            tail = qkv_r[:, :, length - A_BLOCK:, gw:].transpose(0, 2, 1, 3)
            outs.setdefault("a_p%d" % g, []).append(tail.reshape(bp, A_BLOCK * dil, 2, A_HEADS, A_HEAD_DIM))
            outs.setdefault("a_new%d" % g, []).append(jnp.stack([k_s, v_s], axis=2))
        a_s = attn_merge_rows(o_s, l_s).reshape(ms, gw).astype(BF16)
        return attn_merge_prompt(o_p, l_p), a_s, wb16["a_o"]

    def mixer_b(xb, j):
        hps = B_HEADS_PER_STEP
        ng = B_V_HEADS // hps
        w_in = b_w_in[j]
        main = B_CONV_DIM + B_VAL_DIM
        pb, pb_tail = matmul(xb, wb16["b_in"], layer=j, n=main, tn=1024)
        w_tail = jnp.concatenate([w_in[:, main:main + B_V_HEADS].reshape(d, ng, hps),
                                  w_in[:, main + B_V_HEADS:].reshape(d, ng, hps)], axis=2)
        ba, ba_tail = matmul(xb, _lane_groups(w_tail.reshape(d, -1), 2 * hps).astype(BF16)[None],
                    layer=0, n=ng * LANES, tn=ng * LANES)
        gate_lanes = lambda v: jnp.pad(_lane_groups(
            jnp.concatenate([jnp.zeros((ng, hps), F32), v.reshape(ng, hps)], axis=1).reshape(-1), 2 * hps
        ).reshape(ng, LANES), ((0, SUBLANES - ng), (0, 0)))
        alog_pad, dtb_pad = gate_lanes(b_a_log[j]), gate_lanes(b_dt_bias[j])
        taps = _conv_taps(b_conv_w[j])
        nw = b_norm_w[j].reshape(1, B_HEAD_DIM)
        a_p, s_p = gdn_scan(pb, jnp.zeros((bp, SUBLANES, B_CONV_DIM), F32), taps, ba, alog_pad, dtb_pad, nw,
                            jnp.zeros((bp, B_V_HEADS, B_HEAD_DIM, B_HEAD_DIM), F32),
                            nseq=bp, seq=seq, t_valid=seq)
        pb_s = _pad_seq_rows(pb_tail, db, ts, B_CHUNK)
        ba_s = _pad_seq_rows(ba_tail, db, ts, B_CHUNK)
        a_s, s_s = gdn_scan(pb_s, _conv_prev(state_b_conv[j]), taps, ba_s, alog_pad, dtb_pad, nw, state_b_ssm[j],
                            nseq=db, seq=B_CHUNK, t_valid=ts)
        outs.setdefault("b_ssm_p", []).append(s_p)
        outs.setdefault("b_ssm_s", []).append(s_s)
        outs.setdefault("b_conv_p", []).append(_last_rows(pb, bp, seq, CONV_TAPS - 1, 0, B_CONV_DIM))
        outs.setdefault("b_conv_s", []).append(_last_rows(pb_tail, db, ts, CONV_TAPS - 1, 0, B_CONV_DIM))
        return a_p, valid_rows(a_s, B_CHUNK), wb16["b_out"]

    def mixer_c(xb, j):
        w_in = c_w_in[j]
        main = C_D_INNER + C_CONV_DIM
        pc, pc_tail = matmul(xb, wb16["c_in"], layer=j, n=main, tn=1024)
        dt, dt_tail = matmul(xb, _lane_groups(w_in[:, main:], C_HEADS_PER_GROUP).astype(BF16)[None],
                    layer=0, n=C_GROUPS * LANES, tn=C_GROUPS * LANES)
        head_lanes = lambda v: _lane_groups(v, C_HEADS_PER_GROUP).reshape(C_GROUPS, LANES)
        dtb_pad, alog_pad, dskip_pad = head_lanes(c_dt_bias[j]), head_lanes(c_a_log[j]), head_lanes(c_d[j])
        taps = _conv_taps(c_conv_w[j])
        bias = c_conv_b[j].reshape(1, C_CONV_DIM)
        nw = c_norm_w[j].reshape(1, C_D_INNER)
        a_p, h_p = ssd_scan(pc, jnp.zeros((bp, SUBLANES, C_CONV_DIM), F32), taps, bias, dt, dtb_pad, alog_pad,
                            dskip_pad, nw, jnp.zeros((bp, C_HEADS, C_HEAD_DIM, C_D_STATE), F32),
                            nseq=bp, seq=seq, t_valid=seq)
        pc_s = _pad_seq_rows(pc_tail, db, ts, C_CHUNK)
        dt_s = _pad_seq_rows(dt_tail, db, ts, C_CHUNK)
        a_s, h_s = ssd_scan(pc_s, _conv_prev(state_c_conv[j]), taps, bias, dt_s, dtb_pad, alog_pad, dskip_pad, nw,
                            state_c_ssm[j], nseq=db, seq=C_CHUNK, t_valid=ts)
        outs.setdefault("c_ssm_p", []).append(h_p)
        outs.setdefault("c_ssm_s", []).append(h_s)
        outs.setdefault("c_conv_p", []).append(_last_rows(pc, bp, seq, CONV_TAPS - 1, C_D_INNER, C_CONV_DIM))
        outs.setdefault("c_conv_s", []).append(_last_rows(pc_tail, db, ts, CONV_TAPS - 1, C_D_INNER, C_CONV_DIM))
        return a_p, valid_rows(a_s, C_CHUNK), wb16["c_out"]

    def ffn(x, xb, which, i, g, b):
        h, h_tail = swiglu_up(xb, w_gu[which], layer=i)
        return matmul_postnorm(h, wb16[which + "_down"], x, g, b, layer=i, scale=0.5, a_tail=h_tail)

    mixers = (mixer_a, mixer_b, mixer_c)
    for i in range(DEPTH):
        x, xb = ffn(x, xb, "ffn1", i, ln_g[i, 0], ln_b[i, 0])
        a_p, a_s, w_out = mixers[i % 3](xb, i // 3)
        x, xb = matmul_postnorm(a_p, w_out, x, ln_g[i, 1], ln_b[i, 1], layer=i // 3, scale=1.0, a_tail=a_s)
        x, xb = ffn(x, xb, "ffn2", i, ln_g[i, 2], ln_b[i, 2])

    st = lambda name: jnp.stack(outs[name])
    def shifted(buf, new):
        moved = lax.pad(buf, jnp.zeros((), buf.dtype), [(0, 0, 0), (0, 0, 0), (-ts, ts, 0)] + [(0, 0, 0)] * 3)
        return lax.dynamic_update_slice(moved, new, (0, 0, buf.shape[2] - ts, 0, 0, 0))

    a_s = [shifted(a_bufs[g], st("a_new%d" % g)) for g in range(A_GROUPS)]
    return (x[:mp].reshape(bp, seq, d), x[mp:].reshape(db, ts, d),
            st("a_p0"), a_s[0], st("a_p1"), a_s[1], st("a_p2"), a_s[2],
            st("b_ssm_p"), st("b_ssm_s"), st("b_conv_p"), st("b_conv_s"),
            st("c_ssm_p"), st("c_ssm_s"), st("c_conv_p"), st("c_conv_s"))
```
